```python
import math
import jax, jax.numpy as jnp
from jax import lax
import numpy as np

D_MODEL = 1024
BATCH = 8
SEQ = 4096
DEPTH = 4
DEC_BATCH = 16
DEC_SEQ = 16
PAST_LEN = 1024

CHUNK = 64
Q_BLOCK = 128
MIX_WIDTH = D_MODEL
SSM_WIDTH = MIX_WIDTH // 2
SSM_GROUP = 16
SSM_GROUPS = SSM_WIDTH // SSM_GROUP
SSM_STATE = 64
ATT_WIDTH = MIX_WIDTH - SSM_WIDTH
HEAD_DIM = 64
N_HEADS = ATT_WIDTH // HEAD_DIM
IN_WIDTH = 2 * SSM_WIDTH + 4 * ATT_WIDTH
EPS = 1e-6
DT_MIN = 1e-3
DT_MAX = 1e-1

kernel_name = "hymba_s5_stickbreaking_streaming_encoder"


def rms_norm(x, g):
    xf = x.astype(jnp.float32)
    y = xf * lax.rsqrt(jnp.mean(xf * xf, axis=-1, keepdims=True) + EPS)
    return (y * g.astype(jnp.float32)).astype(x.dtype)


def s5_branch(u, h0_re, h0_im, a_re, a_im, log_dt, b_re, b_im, c_re, c_im, d_skip):
    f32 = jnp.float32
    bsz, seq_len, _ = u.shape
    uf = u.astype(f32).reshape(bsz, seq_len, SSM_GROUPS, SSM_GROUP)
    dt = jnp.exp(log_dt.astype(f32))[:, None]
    lr = a_re.astype(f32)
    li = a_im.astype(f32)
    mag = jnp.exp(lr * dt)
    ang = li * dt
    ab_re = mag * jnp.cos(ang)
    ab_im = mag * jnp.sin(ang)
    den = lr * lr + li * li
    nr = ab_re - 1.0
    f_re = (nr * lr + ab_im * li) / den
    f_im = (ab_im * lr - nr * li) / den
    br = b_re.astype(f32)
    bi = b_im.astype(f32)
    bb_re = f_re[..., None] * br - f_im[..., None] * bi
    bb_im = f_re[..., None] * bi + f_im[..., None] * br
    bu_re = jnp.einsum('blgc,gpc->blgp', uf, bb_re)
    bu_im = jnp.einsum('blgc,gpc->blgp', uf, bb_im)
    h0r = h0_re.astype(f32)
    h0i = h0_im.astype(f32)
    bu_re = bu_re.at[:, 0].add(ab_re * h0r - ab_im * h0i)
    bu_im = bu_im.at[:, 0].add(ab_re * h0i + ab_im * h0r)
    a_el_re = jnp.broadcast_to(ab_re, (1, seq_len, SSM_GROUPS, SSM_STATE))
    a_el_im = jnp.broadcast_to(ab_im, (1, seq_len, SSM_GROUPS, SSM_STATE))

    def combine(e1, e2):
        a1r, a1i, b1r, b1i = e1
        a2r, a2i, b2r, b2i = e2
        return (a1r * a2r - a1i * a2i,
                a1r * a2i + a1i * a2r,
                a2r * b1r - a2i * b1i + b2r,
                a2r * b1i + a2i * b1r + b2i)

    _, _, hr, hi = lax.associative_scan(combine, (a_el_re, a_el_im, bu_re, bu_im), axis=1)
    y = (jnp.einsum('blgp,gcp->blgc', hr, c_re.astype(f32))
         - jnp.einsum('blgp,gcp->blgc', hi, c_im.astype(f32)))
    y = y.reshape(bsz, seq_len, SSM_WIDTH) + d_skip.astype(f32) * uf.reshape(bsz, seq_len, SSM_WIDTH)
    return y, hr[:, -1], hi[:, -1]


def stick_breaking(q, k, v, q_pos, k_pos):
    z = jnp.einsum('bqhd,bkhd->bhqk', q, k, preferred_element_type=jnp.float32) * (HEAD_DIM ** -0.5)
    mask = k_pos[None, :] < q_pos[:, None]
    log_1mb = jnp.where(mask, -jax.nn.softplus(z), 0.0)
    after = lax.cumsum(log_1mb, axis=3, reverse=True) - log_1mb
    a = jnp.where(mask, jnp.exp(jax.nn.log_sigmoid(z) + after), 0.0)
    return jnp.einsum('bhqk,bkhd->bqhd', a.astype(v.dtype), v)


def sb_prompt(q, k, v):
    bsz, seq_len = q.shape[0], q.shape[1]
    nb = seq_len // Q_BLOCK
    qb = q.reshape(bsz, nb, Q_BLOCK, N_HEADS, HEAD_DIM).transpose(1, 0, 2, 3, 4)
    pos = jnp.arange(seq_len, dtype=jnp.int32).reshape(nb, Q_BLOCK)
    k_pos = jnp.arange(seq_len, dtype=jnp.int32)
    out = lax.map(lambda a: stick_breaking(a[0], k, v, a[1], k_pos), (qb, pos))
    return out.transpose(1, 0, 2, 3, 4).reshape(bsz, seq_len, N_HEADS, HEAD_DIM)


def sb_sample(q, k, v, k_past, v_past):
    past = k_past.shape[1]
    t_new = q.shape[1]
    k_all = jnp.concatenate([k_past.astype(k.dtype), k], axis=1)
    v_all = jnp.concatenate([v_past.astype(v.dtype), v], axis=1)
    q_pos = past + jnp.arange(t_new, dtype=jnp.int32)
    k_pos = jnp.arange(past + t_new, dtype=jnp.int32)
    return stick_breaking(q, k_all, v_all, q_pos, k_pos)


def hybrid_layer(x, c, h0_re, h0_im, k_past, v_past, norm_g, w_mod, b_mod, w_in,
                 a_re, a_im, log_dt, b_re, b_im, c_re, c_im, d_skip, w_glu, b_glu,
                 q_norm_g, k_norm_g, w_out):
    bsz, seq_len, _ = x.shape
    mod = jax.nn.silu(c) @ w_mod + b_mod
    shift, scale, gate = jnp.split(mod, 3, axis=-1)
    h = rms_norm(x, norm_g) * (1.0 + scale[:, None]) + shift[:, None]
    proj = h @ w_in
    u, z_s, q, k, v, z_a = jnp.split(
        proj, [SSM_WIDTH, 2 * SSM_WIDTH, 2 * SSM_WIDTH + ATT_WIDTH,
               2 * SSM_WIDTH + 2 * ATT_WIDTH, 2 * SSM_WIDTH + 3 * ATT_WIDTH], axis=-1)
    y_s, hr, hi = s5_branch(u, h0_re, h0_im, a_re, a_im, log_dt, b_re, b_im, c_re, c_im, d_skip)
    g_s = jax.nn.gelu(y_s.astype(x.dtype))
    y_s = g_s * jax.nn.sigmoid(g_s @ w_glu + b_glu)
    y_s = y_s * jax.nn.silu(z_s)
    q = rms_norm(q.reshape(bsz, seq_len, N_HEADS, HEAD_DIM), q_norm_g)
    k = rms_norm(k.reshape(bsz, seq_len, N_HEADS, HEAD_DIM), k_norm_g)
    v = v.reshape(bsz, seq_len, N_HEADS, HEAD_DIM)
    if k_past is None:
        o = sb_prompt(q, k, v)
    else:
        o = sb_sample(q, k, v, k_past, v_past)
    y_a = o.reshape(bsz, seq_len, ATT_WIDTH) * jax.nn.silu(z_a)
    mix = jnp.concatenate([y_s, y_a], axis=-1) @ w_out
    x = x + gate[:, None] * mix
    return x, hr, hi, k, v


def setup_inputs(seed: int = 0) -> dict:
    key = jax.random.key(seed)
    ks = jax.random.split(key, 26)
    f32 = jnp.float32

    def nrm(k, shape, s):
        return jax.random.normal(k, shape, f32) * s

    n_idx = jnp.arange(SSM_STATE, dtype=f32)
    return {
        "x_prompt": nrm(ks[0], (BATCH, SEQ, D_MODEL), 1.0),
        "x_sample": nrm(ks[1], (DEC_BATCH, DEC_SEQ, D_MODEL), 1.0),
        "c_prompt": nrm(ks[2], (BATCH, D_MODEL), 1.0),
        "c_sample": nrm(ks[3], (DEC_BATCH, D_MODEL), 1.0),
        "cache_k": nrm(ks[4], (DEPTH, DEC_BATCH, PAST_LEN, N_HEADS, HEAD_DIM), 1.0),
        "cache_v": nrm(ks[5], (DEPTH, DEC_BATCH, PAST_LEN, N_HEADS, HEAD_DIM), 1.0),
        "state_ssm_re": nrm(ks[6], (DEPTH, DEC_BATCH, SSM_GROUPS, SSM_STATE), 0.1),
        "state_ssm_im": nrm(ks[7], (DEPTH, DEC_BATCH, SSM_GROUPS, SSM_STATE), 0.1),
        "norm_g": 1.0 + nrm(ks[8], (DEPTH, D_MODEL), 0.02),
        "w_mod": nrm(ks[9], (DEPTH, D_MODEL, 3 * D_MODEL), 0.5 * D_MODEL ** -0.5),
        "b_mod": nrm(ks[10], (DEPTH, 3 * D_MODEL), 0.01),
        "w_in": nrm(ks[11], (DEPTH, D_MODEL, IN_WIDTH), D_MODEL ** -0.5),
        "ssm_a_re": -0.5 + nrm(ks[12], (DEPTH, SSM_GROUPS, SSM_STATE), 0.01),
        "ssm_a_im": math.pi * n_idx[None, None, :] + nrm(ks[13], (DEPTH, SSM_GROUPS, SSM_STATE), 0.01),
        "ssm_log_dt": jax.random.uniform(ks[14], (DEPTH, SSM_GROUPS), f32,
                                         minval=math.log(DT_MIN), maxval=math.log(DT_MAX)),
        "ssm_b_re": nrm(ks[15], (DEPTH, SSM_GROUPS, SSM_STATE, SSM_GROUP), (2 * SSM_GROUP) ** -0.5),
        "ssm_b_im": nrm(ks[16], (DEPTH, SSM_GROUPS, SSM_STATE, SSM_GROUP), (2 * SSM_GROUP) ** -0.5),
        "ssm_c_re": nrm(ks[17], (DEPTH, SSM_GROUPS, SSM_GROUP, SSM_STATE), SSM_STATE ** -0.5),
        "ssm_c_im": nrm(ks[18], (DEPTH, SSM_GROUPS, SSM_GROUP, SSM_STATE), SSM_STATE ** -0.5),
        "ssm_d": nrm(ks[19], (DEPTH, SSM_WIDTH), 1.0),
        "w_glu": nrm(ks[20], (DEPTH, SSM_WIDTH, SSM_WIDTH), SSM_WIDTH ** -0.5),
        "b_glu": nrm(ks[21], (DEPTH, SSM_WIDTH), 0.01),
        "q_norm_g": 1.0 + nrm(ks[22], (DEPTH, HEAD_DIM), 0.02),
        "k_norm_g": 1.0 + nrm(ks[23], (DEPTH, HEAD_DIM), 0.02),
        "w_out": nrm(ks[24], (DEPTH, MIX_WIDTH, D_MODEL), MIX_WIDTH ** -0.5),
    }


def reference(x_prompt, x_sample, c_prompt, c_sample, cache_k, cache_v, state_ssm_re, state_ssm_im,
              norm_g, w_mod, b_mod, w_in, ssm_a_re, ssm_a_im, ssm_log_dt, ssm_b_re, ssm_b_im,
              ssm_c_re, ssm_c_im, ssm_d, w_glu, b_glu, q_norm_g, k_norm_g, w_out):
    xp = x_prompt
    xs = x_sample
    bsz = x_prompt.shape[0]
    zero_state = jnp.zeros((bsz, SSM_GROUPS, SSM_STATE), jnp.float32)
    pk, pv, pr, pi_, sk, sv, sr, si = [], [], [], [], [], [], [], []
    for l in range(DEPTH):
        lw = (norm_g[l], w_mod[l], b_mod[l], w_in[l], ssm_a_re[l], ssm_a_im[l], ssm_log_dt[l],
              ssm_b_re[l], ssm_b_im[l], ssm_c_re[l], ssm_c_im[l], ssm_d[l], w_glu[l], b_glu[l],
              q_norm_g[l], k_norm_g[l], w_out[l])
        xp, hr, hi, k_new, v_new = hybrid_layer(xp, c_prompt, zero_state, zero_state, None, None, *lw)
        pk.append(k_new); pv.append(v_new); pr.append(hr); pi_.append(hi)
        xs, hr, hi, k_new, v_new = hybrid_layer(xs, c_sample, state_ssm_re[l], state_ssm_im[l],
                                                cache_k[l], cache_v[l], *lw)
        sk.append(k_new); sv.append(v_new); sr.append(hr); si.append(hi)
    return (xp, xs,
            jnp.stack(pk), jnp.stack(pv), jnp.stack(pr), jnp.stack(pi_),
            jnp.stack(sk), jnp.stack(sv), jnp.stack(sr), jnp.stack(si))
```

```python
import functools
import math

import jax
import jax.numpy as jnp
from jax import lax
from jax.experimental import pallas as pl
from jax.experimental.pallas import tpu as pltpu

F32 = jnp.float32
BF16 = jnp.bfloat16

D_MODEL = 1024
SSM_WIDTH = 512
SSM_GROUP = 16
SSM_GROUPS = 32
SSM_STATE = 64
STATE_LANES = SSM_GROUPS * SSM_STATE
ATT_WIDTH = 512
HEAD_DIM = 64
N_HEADS = 8
IN_WIDTH = 2 * SSM_WIDTH + 4 * ATT_WIDTH
EPS = 1e-6

LANES = 128
SUBLANES = 8
HEADS_PER_LANE_BLOCK = LANES // HEAD_DIM
GROUPS_PER_LANE_BLOCK = LANES // SSM_GROUP
N_LANE_BLOCKS = SSM_WIDTH // LANES
KEY_TILE = 128
SB_LOG_CUTOFF = -40.0
VMEM_LIMIT = 56 * 1024 * 1024


def _cparams(n_axes):
    return pltpu.CompilerParams(dimension_semantics=("arbitrary",) * n_axes,
                                vmem_limit_bytes=VMEM_LIMIT)


def _silu(x):
    return x * (1.0 / (1.0 + jnp.exp(-x)))


def _sigmoid(x):
    return 1.0 / (1.0 + jnp.exp(-x))


def _gelu_tanh(x):
    c = math.sqrt(2.0 / math.pi)
    return 0.5 * x * (1.0 + jnp.tanh(c * (x + 0.044715 * (x * x * x))))


def _mod_kernel(c_ref, w_ref, b_ref, o_ref):
    c = c_ref[...]
    a = _silu(c)
    o_ref[0] = jnp.dot(a, w_ref[0], preferred_element_type=F32,
                       precision=lax.Precision.HIGHEST) + b_ref[0]


def _modulation(c_all, w_mod, b_mod):
    depth = w_mod.shape[0]
    n = c_all.shape[0]
    nt = 3
    return pl.pallas_call(
        _mod_kernel,
        grid=(depth, nt),
        in_specs=[pl.BlockSpec((n, D_MODEL), lambda l, j: (0, 0)),
                  pl.BlockSpec((1, D_MODEL, D_MODEL), lambda l, j: (l, 0, j)),
                  pl.BlockSpec((1, 1, D_MODEL), lambda l, j: (l, 0, j))],
        out_specs=pl.BlockSpec((1, n, D_MODEL), lambda l, j: (l, 0, j)),
        out_shape=jax.ShapeDtypeStruct((depth, n, 3 * D_MODEL), F32),
        compiler_params=_cparams(2),
        name="modulation",
    )(c_all, w_mod, b_mod.reshape(depth, 1, 3 * D_MODEL))


def _disc_kernel(lr_ref, li_ref, ldt_ref, bre_ref, bim_ref, pwr_ref, pwi_ref, bbr_ref, bbi_ref):
    lr = lr_ref[...]
    li = li_ref[...]
    dt = jnp.exp(ldt_ref[...])
    mag = jnp.exp(lr * dt)
    ang = li * dt
    ab_re = mag * jnp.cos(ang)
    ab_im = mag * jnp.sin(ang)
    den = lr * lr + li * li
    nr = ab_re - 1.0
    f_re = (nr * lr + ab_im * li) / den
    f_im = (ab_im * lr - nr * li) / den
    bre = bre_ref[...]
    bim = bim_ref[...]
    bbr_ref[...] = f_re[None] * bre - f_im[None] * bim
    bbi_ref[...] = f_re[None] * bim + f_im[None] * bre
    pr, pi_ = ab_re, ab_im
    pwr_ref[0] = pr
    pwi_ref[0] = pi_
    for k in range(1, SUBLANES):
        pr, pi_ = pr * ab_re - pi_ * ab_im, pr * ab_im + pi_ * ab_re
        pwr_ref[k] = pr
        pwi_ref[k] = pi_


def _discretise(a_re, a_im, log_dt, b_re, b_im):
    depth = a_re.shape[0]
    rows = depth * SSM_GROUPS
    lr = a_re.reshape(rows, SSM_STATE)
    li = a_im.reshape(rows, SSM_STATE)
    ldt = jnp.broadcast_to(log_dt.reshape(rows, 1), (rows, SSM_STATE))
    bre = b_re.transpose(3, 0, 1, 2).reshape(SSM_GROUP, rows, SSM_STATE)
    bim = b_im.transpose(3, 0, 1, 2).reshape(SSM_GROUP, rows, SSM_STATE)
    outs = pl.pallas_call(
        _disc_kernel,
        out_shape=(jax.ShapeDtypeStruct((SUBLANES, rows, SSM_STATE), F32),
                   jax.ShapeDtypeStruct((SUBLANES, rows, SSM_STATE), F32),
                   jax.ShapeDtypeStruct((SSM_GROUP, rows, SSM_STATE), F32),
                   jax.ShapeDtypeStruct((SSM_GROUP, rows, SSM_STATE), F32)),
        name="s5_discretise",
    )(lr, li, ldt, bre, bim)
    pw_re, pw_im, bb_re, bb_im = outs
    pw_re = pw_re.reshape(SUBLANES, depth, STATE_LANES).transpose(1, 0, 2)
    pw_im = pw_im.reshape(SUBLANES, depth, STATE_LANES).transpose(1, 0, 2)
    bb_re = bb_re.reshape(SSM_GROUP, depth, SSM_GROUPS, SSM_STATE)
    bb_im = bb_im.reshape(SSM_GROUP, depth, SSM_GROUPS, SSM_STATE)
    return pw_re, pw_im, bb_re, bb_im


def _s5_matrices(pw_re, pw_im, bb_re, bb_im, c_re, c_im):
    depth = pw_re.shape[0]
    eye = jnp.eye(GROUPS_PER_LANE_BLOCK, dtype=F32)

    def b_block(bb):
        x = bb.reshape(SSM_GROUP, depth, N_LANE_BLOCKS, GROUPS_PER_LANE_BLOCK, SSM_STATE)
        x = jnp.einsum('clbgp,gh->lbgchp', x, eye)
        return x.reshape(depth, N_LANE_BLOCKS, LANES, GROUPS_PER_LANE_BLOCK * SSM_STATE)

    def c_block(c):
        x = c.reshape(depth, N_LANE_BLOCKS, GROUPS_PER_LANE_BLOCK, SSM_GROUP, SSM_STATE)
        x = jnp.einsum('lbgcp,gh->lbgphc', x, eye)
        return x.reshape(depth, N_LANE_BLOCKS, GROUPS_PER_LANE_BLOCK * SSM_STATE, LANES)

    bmat = jnp.concatenate([b_block(bb_re), b_block(bb_im)], axis=-1).astype(BF16)
    cmat = jnp.concatenate([c_block(c_re), -c_block(c_im)], axis=-2).astype(BF16)

    row = jnp.arange(SUBLANES)[None, :, None]

    def step_mult(pw):
        return jnp.stack([jnp.where(row >= k, pw[:, k - 1][:, None, :], 0.0) for k in (1, 2, 4)], axis=1)

    return bmat, cmat, step_mult(pw_re), step_mult(pw_im), pw_re, pw_im


def _inproj_kernel(x_ref, shift_ref, scale_ref, g_ref, w_ref, hsum_ref, qg_ref, kg_ref,
                   u_ref, zs_ref, q_ref, k_ref, v_ref, kb_ref, vb_ref, za_ref):
    x = x_ref[...]
    ms = jnp.mean(x * x, axis=-1, keepdims=True)
    h = x * lax.rsqrt(ms + EPS) * g_ref[...]
    h = h * (1.0 + scale_ref[0]) + shift_ref[0]
    hb = h.astype(BF16)

    def proj(c):
        return jnp.dot(hb, w_ref[:, c * SSM_WIDTH:(c + 1) * SSM_WIDTH], preferred_element_type=F32)

    def head_norm(p, g):
        ss = jnp.dot((p * p).astype(BF16), hsum_ref[...], preferred_element_type=F32)
        return p * lax.rsqrt(ss * (1.0 / HEAD_DIM) + EPS) * g

    u_ref[...] = proj(0).astype(BF16)
    zs_ref[...] = _silu(proj(1)).astype(BF16)
    q = head_norm(proj(2), qg_ref[...])
    q_ref[...] = (q * (HEAD_DIM ** -0.5)).astype(BF16)
    k = head_norm(proj(3), kg_ref[...])
    k_ref[...] = k
    kb_ref[...] = k.astype(BF16)
    v = proj(4)
    v_ref[...] = v
    vb_ref[...] = v.astype(BF16)
    za_ref[...] = _silu(proj(5)).astype(BF16)


def _in_projection(x2d, shift, scale, norm_g, w_in_b, hsum, q_g, k_g, tm, rows_per_seq):
    n = x2d.shape[0]
    nt = n // tm
    if rows_per_seq is None:
        mod_spec = pl.BlockSpec((1, tm, D_MODEL), lambda i: (0, i, 0))
    else:
        tps = rows_per_seq // tm
        mod_spec = pl.BlockSpec((1, 1, D_MODEL), lambda i: (i // tps, 0, 0))
    row_spec = lambda w: pl.BlockSpec((tm, w), lambda i: (i, 0))
    full = lambda a: pl.BlockSpec(a.shape, lambda i: (0,) * a.ndim)
    bf = jax.ShapeDtypeStruct((n, SSM_WIDTH), BF16)
    f3 = jax.ShapeDtypeStruct((n, ATT_WIDTH), F32)
    return pl.pallas_call(
        _inproj_kernel,
        grid=(nt,),
        in_specs=[row_spec(D_MODEL), mod_spec, mod_spec, full(norm_g), full(w_in_b), full(hsum),
                  full(q_g), full(k_g)],
        out_specs=[row_spec(SSM_WIDTH)] * 8,
        out_shape=(bf, bf, bf, f3, f3, bf, bf, bf),
        compiler_params=_cparams(1),
        name="in_proj",
    )(x2d, shift, scale, norm_g, w_in_b, hsum, q_g, k_g)


def _s5_kernel(u_ref, zs_ref, h0r_ref, h0i_ref, bmat_ref, cmat_ref, mr_ref, mi_ref, pr_ref, pi_ref,
               d_ref, wglu_ref, bglu_ref,
               y_ref, hfr_ref, hfi_ref,
               vre, vim, car_re, car_im, ybuf, *, n_seq, seq_rows, lane_chunk):
    i = pl.program_id(1)
    rows = n_seq * seq_rows

    @pl.when(i == 0)
    def _():
        for s in range(n_seq):
            car_re[s] = jnp.broadcast_to(h0r_ref[s], (SUBLANES, STATE_LANES))
            car_im[s] = jnp.broadcast_to(h0i_ref[s], (SUBLANES, STATE_LANES))

    half = GROUPS_PER_LANE_BLOCK * SSM_STATE
    for lb in range(N_LANE_BLOCKS):
        ub = u_ref[:, lb * LANES:(lb + 1) * LANES]
        bu = jnp.dot(ub, bmat_ref[lb], preferred_element_type=F32)
        vre[:, lb * half:(lb + 1) * half] = bu[:, :half]
        vim[:, lb * half:(lb + 1) * half] = bu[:, half:]

    groups = seq_rows // SUBLANES
    for ch in range(STATE_LANES // lane_chunk):
        sl = slice(ch * lane_chunk, (ch + 1) * lane_chunk)
        prc = pr_ref[:, sl]
        pic = pi_ref[:, sl]

        def seq_body(s, _):
            def group_body(r, carry):
                cr, ci = carry
                row = pl.multiple_of(s * seq_rows + r * SUBLANES, SUBLANES)
                vr = vre[pl.ds(row, SUBLANES), sl]
                vi = vim[pl.ds(row, SUBLANES), sl]
                for idx, k in enumerate((1, 2, 4)):
                    mr = mr_ref[idx, :, sl]
                    mi = mi_ref[idx, :, sl]
                    sr = pltpu.roll(vr, k, 0)
                    si = pltpu.roll(vi, k, 0)
                    vr, vi = vr + (mr * sr - mi * si), vi + (mr * si + mi * sr)
                hr = vr + (prc * cr - pic * ci)
                hi = vi + (prc * ci + pic * cr)
                vre[pl.ds(row, SUBLANES), sl] = hr
                vim[pl.ds(row, SUBLANES), sl] = hi
                last = SUBLANES - 1
                return (jnp.broadcast_to(hr[last:last + 1], hr.shape),
                        jnp.broadcast_to(hi[last:last + 1], hi.shape))

            cr, ci = lax.fori_loop(0, groups, group_body, (car_re[s, :, sl], car_im[s, :, sl]))
            car_re[s, :, sl] = cr
            car_im[s, :, sl] = ci
            return 0

        lax.fori_loop(0, n_seq, seq_body, 0)

    for s in range(n_seq):
        hfr_ref[s] = car_re[s, 0:1, :]
        hfi_ref[s] = car_im[s, 0:1, :]

    for lb in range(N_LANE_BLOCKS):
        hcat = jnp.concatenate([vre[:, lb * half:(lb + 1) * half].astype(BF16),
                                vim[:, lb * half:(lb + 1) * half].astype(BF16)], axis=1)
        ybuf[:, lb * LANES:(lb + 1) * LANES] = jnp.dot(hcat, cmat_ref[lb], preferred_element_type=F32)
    y = ybuf[...] + d_ref[...] * u_ref[...].astype(F32)
    g = _gelu_tanh(y)
    gate = _sigmoid(jnp.dot(g.astype(BF16), wglu_ref[...], preferred_element_type=F32) + bglu_ref[...])
    y_ref[...] = (g * gate * zs_ref[...].astype(F32)).astype(BF16)


def _s5_branch(u, zs, h0_re, h0_im, bmat, cmat, mr, mi, pr, pi_, d_skip, w_glu_b, b_glu,
               n_batch_steps, n_seq, seq_rows, steps_per_seq):
    n = u.shape[0]
    tile = n_seq * seq_rows
    nb_total = h0_re.shape[0]
    row_spec = pl.BlockSpec((tile, SSM_WIDTH), lambda b, i: (b * steps_per_seq + i, 0))
    st_spec = pl.BlockSpec((n_seq, 1, STATE_LANES), lambda b, i: (b, 0, 0))
    full = lambda a: pl.BlockSpec(a.shape, lambda b, i: (0,) * a.ndim)
    kern = functools.partial(_s5_kernel, n_seq=n_seq, seq_rows=seq_rows, lane_chunk=256)
    return pl.pallas_call(
        kern,
        grid=(n_batch_steps, steps_per_seq),
        in_specs=[row_spec, row_spec, st_spec, st_spec, full(bmat), full(cmat), full(mr), full(mi),
                  full(pr), full(pi_), full(d_skip), full(w_glu_b), full(b_glu)],
        out_specs=[row_spec, st_spec, st_spec],
        out_shape=(jax.ShapeDtypeStruct((n, SSM_WIDTH), BF16),
                   jax.ShapeDtypeStruct((nb_total, 1, STATE_LANES), F32),
                   jax.ShapeDtypeStruct((nb_total, 1, STATE_LANES), F32)),
        scratch_shapes=[pltpu.VMEM((tile, STATE_LANES), F32), pltpu.VMEM((tile, STATE_LANES), F32),
                        pltpu.VMEM((n_seq, SUBLANES, STATE_LANES), F32),
                        pltpu.VMEM((n_seq, SUBLANES, STATE_LANES), F32),
                        pltpu.VMEM((tile, SSM_WIDTH), F32)],
        compiler_params=_cparams(2),
        name="s5_branch",
    )(u, zs, h0_re, h0_im, bmat, cmat, mr, mi, pr, pi_, d_skip, w_glu_b, b_glu)


def _sb_tile(qh, kt, vt, carry, tri_ref, mask):
    z = lax.dot_general(qh, kt, (((1,), (1,)), ((), ())), preferred_element_type=F32)
    l1p = jnp.log(1.0 + jnp.exp(-jnp.abs(z)))
    log_1mb = -(jnp.maximum(z, 0.0) + l1p)
    log_b = jnp.minimum(z, 0.0) - l1p
    if mask is not None:
        log_1mb = jnp.where(mask, log_1mb, 0.0)
    hi = log_1mb.astype(BF16)
    lo = (log_1mb - hi.astype(F32)).astype(BF16)
    cs = jnp.dot(jnp.concatenate([hi, lo], axis=1), tri_ref[...], preferred_element_type=F32)
    after = carry + cs[:, :KEY_TILE]
    a = jnp.exp(log_b + after)
    if mask is not None:
        a = jnp.where(mask, a, 0.0)
    pv = jnp.dot(a.astype(BF16), vt, preferred_element_type=F32)
    return pv, cs[:, KEY_TILE:]


def _sb_mask_queries(q_ref, qm_ref):
    tq = q_ref.shape[0]
    lane = lax.broadcasted_iota(jnp.int32, (tq, LANES), 1)
    for h in range(N_HEADS):
        j, hh = divmod(h, HEADS_PER_LANE_BLOCK)
        in_head = (lane >= hh * HEAD_DIM) & (lane < (hh + 1) * HEAD_DIM)
        qpair = q_ref[:, j * LANES:(j + 1) * LANES].astype(F32)
        qm_ref[h] = jnp.where(in_head, qpair, 0.0).astype(BF16)


def _sb_heads(qm_ref, k_tile, v_tile, tri_ref, acc_ref, car_ref, mask, first):
    tq = qm_ref.shape[1]
    lane = lax.broadcasted_iota(jnp.int32, (tq, LANES), 1)
    cmax = None
    for j in range(ATT_WIDTH // LANES):
        kt = k_tile(j)
        vt = v_tile(j)
        out = None
        for hh in range(HEADS_PER_LANE_BLOCK):
            h = j * HEADS_PER_LANE_BLOCK + hh
            in_head = (lane >= hh * HEAD_DIM) & (lane < (hh + 1) * HEAD_DIM)
            qh = qm_ref[h]
            carry = jnp.zeros((tq, LANES), F32) if first else car_ref[h]
            pv, tot = _sb_tile(qh, kt, vt, carry, tri_ref, mask)
            carry = carry + tot
            car_ref[h] = carry
            cmax = carry if cmax is None else jnp.maximum(cmax, carry)
            pv = jnp.where(in_head, pv, 0.0)
            out = pv if out is None else out + pv
        if first:
            acc_ref[:, j * LANES:(j + 1) * LANES] = out
        else:
            acc_ref[:, j * LANES:(j + 1) * LANES] += out
    return jnp.max(cmax)


def _sb_prompt_kernel(q_ref, k_ref, v_ref, za_ref, tri_ref, o_ref, acc_ref, car_ref, qm_ref):
    qi = pl.program_id(1)
    tq = q_ref.shape[0]
    _sb_mask_queries(q_ref, qm_ref)
    row = lax.broadcasted_iota(jnp.int32, (tq, KEY_TILE), 0)
    col = lax.broadcasted_iota(jnp.int32, (tq, KEY_TILE), 1)

    def tiles(ref, t):
        start = pl.multiple_of(t * KEY_TILE, KEY_TILE)
        return lambda j: ref[pl.ds(start, KEY_TILE), j * LANES:(j + 1) * LANES]

    m0 = _sb_heads(qm_ref, tiles(k_ref, qi), tiles(v_ref, qi), tri_ref, acc_ref, car_ref, col < row, True)

    def cond(state):
        t, m = state
        return (t >= 0) & (m > SB_LOG_CUTOFF)

    def body(state):
        t, _ = state
        m = _sb_heads(qm_ref, tiles(k_ref, t), tiles(v_ref, t), tri_ref, acc_ref, car_ref, None, False)
        return t - 1, m

    lax.while_loop(cond, body, (qi - 1, m0))
    o_ref[...] = (acc_ref[...] * za_ref[...].astype(F32)).astype(BF16)


def _sb_prompt(q, kb, vb, za, tri, n_batch, seq_len):
    tq = KEY_TILE
    nq = seq_len // tq
    row_spec = pl.BlockSpec((tq, ATT_WIDTH), lambda b, i: (b * nq + i, 0))
    seq_spec = pl.BlockSpec((seq_len, ATT_WIDTH), lambda b, i: (b, 0))
    return pl.pallas_call(
        _sb_prompt_kernel,
        grid=(n_batch, nq),
        in_specs=[row_spec, seq_spec, seq_spec, row_spec, pl.BlockSpec(tri.shape, lambda b, i: (0, 0))],
        out_specs=row_spec,
        out_shape=jax.ShapeDtypeStruct(q.shape, BF16),
        scratch_shapes=[pltpu.VMEM((tq, ATT_WIDTH), F32), pltpu.VMEM((N_HEADS, tq, LANES), F32),
                        pltpu.VMEM((N_HEADS, tq, LANES), BF16)],
        compiler_params=_cparams(2),
        name="sb_prompt",
    )(q, kb, vb, za, tri)


def _sb_sample_kernel(q_ref, kn_ref, vn_ref, kc_ref, vc_ref, za_ref, tri_ref, o_ref, acc_ref, car_ref,
                      qm_ref):
    tq = q_ref.shape[0]
    n_past = kc_ref.shape[2] // KEY_TILE
    _sb_mask_queries(q_ref, qm_ref)
    row = lax.broadcasted_iota(jnp.int32, (tq, KEY_TILE), 0)
    col = lax.broadcasted_iota(jnp.int32, (tq, KEY_TILE), 1)

    new_tile = lambda ref: (lambda j: ref[:, j * LANES:(j + 1) * LANES])
    m0 = _sb_heads(qm_ref, new_tile(kn_ref), new_tile(vn_ref), tri_ref, acc_ref, car_ref, col < row, True)

    def past_tile(ref, t):
        start = pl.multiple_of(t * KEY_TILE, KEY_TILE)
        return lambda j: ref[0, 0, pl.ds(start, KEY_TILE), j * LANES:(j + 1) * LANES].astype(BF16)

    def cond(state):
        t, m = state
        return (t >= 0) & (m > SB_LOG_CUTOFF)

    def body(state):
        t, _ = state
        m = _sb_heads(qm_ref, past_tile(kc_ref, t), past_tile(vc_ref, t), tri_ref, acc_ref, car_ref,
                      None, False)
        return t - 1, m

    lax.while_loop(cond, body, (n_past - 1, m0))
    o_ref[...] = (acc_ref[...] * za_ref[...].astype(F32)).astype(BF16)


def _sb_sample(q, k_new, v_new, k_past, v_past, layer, za, tri, n_batch, t_new):
    past = k_past.shape[2]
    row_spec = pl.BlockSpec((t_new, ATT_WIDTH), lambda b: (b, 0))
    new_spec = pl.BlockSpec((KEY_TILE, ATT_WIDTH), lambda b: (b, 0))
    past_spec = pl.BlockSpec((1, 1, past, ATT_WIDTH), lambda b: (layer, b, 0, 0))
    return pl.pallas_call(
        _sb_sample_kernel,
        grid=(n_batch,),
        in_specs=[row_spec, new_spec, new_spec, past_spec, past_spec, row_spec,
                  pl.BlockSpec(tri.shape, lambda b: (0, 0))],
        out_specs=row_spec,
        out_shape=jax.ShapeDtypeStruct(q.shape, BF16),
        scratch_shapes=[pltpu.VMEM((t_new, ATT_WIDTH), F32), pltpu.VMEM((N_HEADS, t_new, LANES), F32),
                        pltpu.VMEM((N_HEADS, t_new, LANES), BF16)],
        compiler_params=_cparams(1),
        name="sb_sample",
    )(q, k_new, v_new, k_past, v_past, za, tri)


def _outproj_kernel(x_ref, ys_ref, ya_ref, gate_ref, w_ref, o_ref):
    mix = jnp.dot(ys_ref[...], w_ref[:SSM_WIDTH, :], preferred_element_type=F32)
    mix = mix + jnp.dot(ya_ref[...], w_ref[SSM_WIDTH:, :], preferred_element_type=F32)
    o_ref[...] = x_ref[...] + gate_ref[0] * mix


def _out_projection(x2d, ys, ya, gate, w_out_b, tm, rows_per_seq):
    n = x2d.shape[0]
    if rows_per_seq is None:
        mod_spec = pl.BlockSpec((1, tm, D_MODEL), lambda i: (0, i, 0))
    else:
        tps = rows_per_seq // tm
        mod_spec = pl.BlockSpec((1, 1, D_MODEL), lambda i: (i // tps, 0, 0))
    row_spec = lambda w: pl.BlockSpec((tm, w), lambda i: (i, 0))
    return pl.pallas_call(
        _outproj_kernel,
        grid=(n // tm,),
        in_specs=[row_spec(D_MODEL), row_spec(SSM_WIDTH), row_spec(ATT_WIDTH), mod_spec,
                  pl.BlockSpec(w_out_b.shape, lambda i: (0, 0))],
        out_specs=row_spec(D_MODEL),
        out_shape=jax.ShapeDtypeStruct(x2d.shape, F32),
        compiler_params=_cparams(1),
        name="out_proj",
    )(x2d, ys, ya, gate, w_out_b)


def _sb_sum_matrix():
    j = jnp.arange(2 * KEY_TILE)[:, None] % KEY_TILE
    s = jnp.arange(2 * KEY_TILE)[None, :]
    return ((s >= KEY_TILE) | (j > s)).astype(BF16)


def _head_sum_matrix():
    a = jnp.arange(ATT_WIDTH)
    return (a[:, None] // HEAD_DIM == a[None, :] // HEAD_DIM).astype(BF16)


def kernel(x_prompt, x_sample, c_prompt, c_sample, cache_k, cache_v, state_ssm_re, state_ssm_im, norm_g, w_mod, b_mod, w_in, ssm_a_re, ssm_a_im, ssm_log_dt, ssm_b_re, ssm_b_im, ssm_c_re, ssm_c_im, ssm_d, w_glu, b_glu, q_norm_g, k_norm_g, w_out):
    depth = w_in.shape[0]
    nb, seq_len, _ = x_prompt.shape
    ns, t_new, _ = x_sample.shape
    past = cache_k.shape[2]
    tm = 512

    mod = _modulation(jnp.concatenate([c_prompt, c_sample], axis=0), w_mod, b_mod)
    pw_re, pw_im, bb_re, bb_im = _discretise(ssm_a_re, ssm_a_im, ssm_log_dt, ssm_b_re, ssm_b_im)
    bmat, cmat, mr, mi, pr, pi_ = _s5_matrices(pw_re, pw_im, bb_re, bb_im, ssm_c_re, ssm_c_im)

    w_in_b = w_in.astype(BF16)
    w_out_b = w_out.astype(BF16)
    w_glu_b = w_glu.astype(BF16)
    tri = _sb_sum_matrix()
    hsum = _head_sum_matrix()

    xp = x_prompt.reshape(nb * seq_len, D_MODEL)
    xs = x_sample.reshape(ns * t_new, D_MODEL)
    zero_state = jnp.zeros((nb, 1, STATE_LANES), F32)
    kc = cache_k.reshape(depth, ns, past, ATT_WIDTH)
    vc = cache_v.reshape(depth, ns, past, ATT_WIDTH)

    outs = {name: [] for name in ("pk", "pv", "pr", "pi", "sk", "sv", "sr", "si")}
    for l in range(depth):
        g2 = norm_g[l].reshape(1, D_MODEL)
        qg = jnp.tile(q_norm_g[l], N_HEADS).reshape(1, ATT_WIDTH)
        kg = jnp.tile(k_norm_g[l], N_HEADS).reshape(1, ATT_WIDTH)
        d2 = ssm_d[l].reshape(1, SSM_WIDTH)
        bg2 = b_glu[l].reshape(1, SSM_WIDTH)
        s5_w = (bmat[l], cmat[l], mr[l], mi[l], pr[l], pi_[l], d2, w_glu_b[l], bg2)

        mp = mod[l, :nb].reshape(nb, 1, 3 * D_MODEL)
        shift, scale, gate = (mp[:, :, i * D_MODEL:(i + 1) * D_MODEL] for i in range(3))
        u, zs, q, k, v, kb, vb, za = _in_projection(xp, shift, scale, g2, w_in_b[l], hsum, qg, kg,
                                                    tm, seq_len)
        ys, hfr, hfi = _s5_branch(u, zs, zero_state, zero_state, *s5_w,
                                  n_batch_steps=nb, n_seq=1, seq_rows=tm, steps_per_seq=seq_len // tm)
        ya = _sb_prompt(q, kb, vb, za, tri, nb, seq_len)
        xp = _out_projection(xp, ys, ya, gate, w_out_b[l], tm, seq_len)
        outs["pk"].append(k.reshape(nb, seq_len, N_HEADS, HEAD_DIM))
        outs["pv"].append(v.reshape(nb, seq_len, N_HEADS, HEAD_DIM))
        outs["pr"].append(hfr.reshape(nb, SSM_GROUPS, SSM_STATE))
        outs["pi"].append(hfi.reshape(nb, SSM_GROUPS, SSM_STATE))

        ms = jnp.repeat(mod[l, nb:], t_new, axis=0).reshape(1, ns * t_new, 3 * D_MODEL)
        shift, scale, gate = (ms[:, :, i * D_MODEL:(i + 1) * D_MODEL] for i in range(3))
        ts = ns * t_new
        u, zs, q, k, v, kb, vb, za = _in_projection(xs, shift, scale, g2, w_in_b[l], hsum, qg, kg,
                                                    ts, None)
        h0r = state_ssm_re[l].reshape(ns, 1, STATE_LANES)
        h0i = state_ssm_im[l].reshape(ns, 1, STATE_LANES)
        ys, hfr, hfi = _s5_branch(u, zs, h0r, h0i, *s5_w,
                                  n_batch_steps=1, n_seq=ns, seq_rows=t_new, steps_per_seq=1)
        pad = lambda a: jnp.pad(a.reshape(ns, t_new, ATT_WIDTH),
                                ((0, 0), (0, KEY_TILE - t_new), (0, 0))).reshape(ns * KEY_TILE, ATT_WIDTH)
        ya = _sb_sample(q, pad(kb), pad(vb), kc, vc, l, za, tri, ns, t_new)
        xs = _out_projection(xs, ys, ya, gate, w_out_b[l], ts, None)
        outs["sk"].append(k.reshape(ns, t_new, N_HEADS, HEAD_DIM))
        outs["sv"].append(v.reshape(ns, t_new, N_HEADS, HEAD_DIM))
        outs["sr"].append(hfr.reshape(ns, SSM_GROUPS, SSM_STATE))
        outs["si"].append(hfi.reshape(ns, SSM_GROUPS, SSM_STATE))

    st = lambda name: jnp.stack(outs[name])
    return (xp.reshape(nb, seq_len, D_MODEL), xs.reshape(ns, t_new, D_MODEL),
            st("pk"), st("pv"), st("pr"), st("pi"), st("sk"), st("sv"), st("sr"), st("si"))
```

```python
import functools
import math

import jax
import jax.numpy as jnp
from jax import lax
from jax.experimental import pallas as pl
from jax.experimental.pallas import tpu as pltpu

F32 = jnp.float32
BF16 = jnp.bfloat16

D_MODEL = 1024
SSM_WIDTH = 512
SSM_GROUP = 16
SSM_GROUPS = 32
SSM_STATE = 64
STATE_LANES = SSM_GROUPS * SSM_STATE
ATT_WIDTH = 512
HEAD_DIM = 64
N_HEADS = 8
IN_WIDTH = 2 * SSM_WIDTH + 4 * ATT_WIDTH
EPS = 1e-6

LANES = 128
SUBLANES = 8
HEADS_PER_LANE_BLOCK = LANES // HEAD_DIM
GROUPS_PER_LANE_BLOCK = LANES // SSM_GROUP
N_LANE_BLOCKS = SSM_WIDTH // LANES
N_PAIRS = N_HEADS // HEADS_PER_LANE_BLOCK
KEY_TILE = 128
SB_Q_ROWS = 64
SB_STEP_BLOCKS = 4
SB_LOG_CUTOFF = -40.0
VMEM_LIMIT = 56 * 1024 * 1024


def _cparams(n_axes):
    return pltpu.CompilerParams(dimension_semantics=("arbitrary",) * n_axes,
                                vmem_limit_bytes=VMEM_LIMIT)


def _silu(x):
    return x * (1.0 / (1.0 + jnp.exp(-x)))


def _sigmoid(x):
    return 1.0 / (1.0 + jnp.exp(-x))


def _gelu_tanh(x):
    c = math.sqrt(2.0 / math.pi)
    return 0.5 * x * (1.0 + jnp.tanh(c * (x + 0.044715 * (x * x * x))))


def _mod_kernel(c_ref, w_ref, b_ref, o_ref):
    c = c_ref[...]
    a = _silu(c)
    o_ref[0] = jnp.dot(a, w_ref[0], preferred_element_type=F32,
                       precision=lax.Precision.HIGHEST) + b_ref[0]


def _modulation(c_all, w_mod, b_mod):
    depth = w_mod.shape[0]
    n = c_all.shape[0]
    nt = 3
    return pl.pallas_call(
        _mod_kernel,
        grid=(depth, nt),
        in_specs=[pl.BlockSpec((n, D_MODEL), lambda l, j: (0, 0)),
                  pl.BlockSpec((1, D_MODEL, D_MODEL), lambda l, j: (l, 0, j)),
                  pl.BlockSpec((1, 1, D_MODEL), lambda l, j: (l, 0, j))],
        out_specs=pl.BlockSpec((1, n, D_MODEL), lambda l, j: (l, 0, j)),
        out_shape=jax.ShapeDtypeStruct((depth, n, 3 * D_MODEL), F32),
        compiler_params=_cparams(2),
        name="modulation",
    )(c_all, w_mod, b_mod.reshape(depth, 1, 3 * D_MODEL))


def _disc_kernel(lr_ref, li_ref, ldt_ref, bre_ref, bim_ref, pwr_ref, pwi_ref, bbr_ref, bbi_ref):
    lr = lr_ref[...]
    li = li_ref[...]
    dt = jnp.exp(ldt_ref[...])
    mag = jnp.exp(lr * dt)
    ang = li * dt
    ab_re = mag * jnp.cos(ang)
    ab_im = mag * jnp.sin(ang)
    den = lr * lr + li * li
    nr = ab_re - 1.0
    f_re = (nr * lr + ab_im * li) / den
    f_im = (ab_im * lr - nr * li) / den
    bre = bre_ref[...]
    bim = bim_ref[...]
    bbr_ref[...] = f_re[None] * bre - f_im[None] * bim
    bbi_ref[...] = f_re[None] * bim + f_im[None] * bre
    pr, pi_ = ab_re, ab_im
    pwr_ref[0] = pr
    pwi_ref[0] = pi_
    for k in range(1, SUBLANES):
        pr, pi_ = pr * ab_re - pi_ * ab_im, pr * ab_im + pi_ * ab_re
        pwr_ref[k] = pr
        pwi_ref[k] = pi_


def _discretise(a_re, a_im, log_dt, b_re, b_im):
    depth = a_re.shape[0]
    rows = depth * SSM_GROUPS
    lr = a_re.reshape(rows, SSM_STATE)
    li = a_im.reshape(rows, SSM_STATE)
    ldt = jnp.broadcast_to(log_dt.reshape(rows, 1), (rows, SSM_STATE))
    bre = b_re.transpose(3, 0, 1, 2).reshape(SSM_GROUP, rows, SSM_STATE)
    bim = b_im.transpose(3, 0, 1, 2).reshape(SSM_GROUP, rows, SSM_STATE)
    outs = pl.pallas_call(
        _disc_kernel,
        out_shape=(jax.ShapeDtypeStruct((SUBLANES, rows, SSM_STATE), F32),
                   jax.ShapeDtypeStruct((SUBLANES, rows, SSM_STATE), F32),
                   jax.ShapeDtypeStruct((SSM_GROUP, rows, SSM_STATE), F32),
                   jax.ShapeDtypeStruct((SSM_GROUP, rows, SSM_STATE), F32)),
        name="s5_discretise",
    )(lr, li, ldt, bre, bim)
    pw_re, pw_im, bb_re, bb_im = outs
    pw_re = pw_re.reshape(SUBLANES, depth, STATE_LANES).transpose(1, 0, 2)
    pw_im = pw_im.reshape(SUBLANES, depth, STATE_LANES).transpose(1, 0, 2)
    bb_re = bb_re.reshape(SSM_GROUP, depth, SSM_GROUPS, SSM_STATE)
    bb_im = bb_im.reshape(SSM_GROUP, depth, SSM_GROUPS, SSM_STATE)
    return pw_re, pw_im, bb_re, bb_im


def _s5_matrices(pw_re, pw_im, bb_re, bb_im, c_re, c_im):
    depth = pw_re.shape[0]
    eye = jnp.eye(GROUPS_PER_LANE_BLOCK, dtype=F32)

    def b_block(bb):
        x = bb.reshape(SSM_GROUP, depth, N_LANE_BLOCKS, GROUPS_PER_LANE_BLOCK, SSM_STATE)
        x = jnp.einsum('clbgp,gh->lbgchp', x, eye)
        return x.reshape(depth, N_LANE_BLOCKS, LANES, GROUPS_PER_LANE_BLOCK * SSM_STATE)

    def c_block(c):
        x = c.reshape(depth, N_LANE_BLOCKS, GROUPS_PER_LANE_BLOCK, SSM_GROUP, SSM_STATE)
        x = jnp.einsum('lbgcp,gh->lbgphc', x, eye)
        return x.reshape(depth, N_LANE_BLOCKS, GROUPS_PER_LANE_BLOCK * SSM_STATE, LANES)

    bmat = jnp.concatenate([b_block(bb_re), b_block(bb_im)], axis=-1).astype(BF16)
    cmat = jnp.concatenate([c_block(c_re), -c_block(c_im)], axis=-2).astype(BF16)

    row = jnp.arange(SUBLANES)[None, :, None]

    def step_mult(pw):
        return jnp.stack([jnp.where(row >= k, pw[:, k - 1][:, None, :], 0.0) for k in (1, 2, 4)], axis=1)

    return bmat, cmat, step_mult(pw_re), step_mult(pw_im), pw_re, pw_im


def _inproj_kernel(x_ref, shift_ref, scale_ref, g_ref, w_ref, hsum_ref, qg_ref, kg_ref, *rest):
    u_ref, zs_ref, q_ref, k_ref, v_ref, kb_ref, vb_ref, za_ref = rest[-8:]
    x = x_ref[...]
    ms = jnp.mean(x * x, axis=-1, keepdims=True)
    h = x * lax.rsqrt(ms + EPS) * g_ref[...]
    h = h * (1.0 + scale_ref[0]) + shift_ref[0]
    hb = h.astype(BF16)

    def proj(c):
        return jnp.dot(hb, w_ref[:, c * SSM_WIDTH:(c + 1) * SSM_WIDTH], preferred_element_type=F32)

    def head_norm(p, g):
        ss = jnp.dot((p * p).astype(BF16), hsum_ref[...], preferred_element_type=F32)
        return p * lax.rsqrt(ss * (1.0 / HEAD_DIM) + EPS) * g

    u_ref[...] = proj(0).astype(BF16)
    zs_ref[...] = _silu(proj(1)).astype(BF16)
    q = head_norm(proj(2), qg_ref[...])
    q_ref[...] = (q * (HEAD_DIM ** -0.5)).astype(BF16)
    k = head_norm(proj(3), kg_ref[...])
    k_ref[0] = k
    kb_ref[...] = k.astype(BF16)
    v = proj(4)
    v_ref[0] = v
    vb_ref[...] = v.astype(BF16)
    za_ref[...] = _silu(proj(5)).astype(BF16)


def _in_projection(x2d, shift, scale, norm_g, w_in_b, hsum, q_g, k_g, tm, rows_per_seq,
                   layer, depth, kv_all):
    n = x2d.shape[0]
    nt = n // tm
    if rows_per_seq is None:
        mod_spec = pl.BlockSpec((1, tm, D_MODEL), lambda i: (0, i, 0))
    else:
        tps = rows_per_seq // tm
        mod_spec = pl.BlockSpec((1, 1, D_MODEL), lambda i: (i // tps, 0, 0))
    row_spec = lambda w: pl.BlockSpec((tm, w), lambda i: (i, 0))
    full = lambda a: pl.BlockSpec(a.shape, lambda i: (0,) * a.ndim)
    bf = jax.ShapeDtypeStruct((n, SSM_WIDTH), BF16)
    f3 = jax.ShapeDtypeStruct((depth, n, ATT_WIDTH), F32)
    kv_spec = pl.BlockSpec((1, tm, ATT_WIDTH), lambda i: (layer, i, 0))
    in_specs = [row_spec(D_MODEL), mod_spec, mod_spec, full(norm_g), full(w_in_b), full(hsum),
                full(q_g), full(k_g)]
    args = [x2d, shift, scale, norm_g, w_in_b, hsum, q_g, k_g]
    aliases = {}
    if kv_all is not None:
        in_specs += [pl.BlockSpec(memory_space=pl.ANY)] * 2
        args += list(kv_all)
        aliases = {len(args) - 2: 3, len(args) - 1: 4}
    return pl.pallas_call(
        _inproj_kernel,
        grid=(nt,),
        in_specs=in_specs,
        out_specs=[row_spec(SSM_WIDTH)] * 3 + [kv_spec] * 2 + [row_spec(SSM_WIDTH)] * 3,
        out_shape=(bf, bf, bf, f3, f3, bf, bf, bf),
        input_output_aliases=aliases,
        compiler_params=_cparams(1),
        name="in_proj",
    )(*args)


def _s5_kernel(u_ref, zs_ref, h0r_ref, h0i_ref, bmat_ref, cmat_ref, mr_ref, mi_ref, pr_ref, pi_ref,
               d_ref, wglu_ref, bglu_ref,
               y_ref, hfr_ref, hfi_ref,
               vre, vim, hre, him, car_re, car_im, ybuf, *, n_seq, seq_rows, lane_chunk):
    i = pl.program_id(1)
    rows = n_seq * seq_rows

    @pl.when(i == 0)
    def _():
        for s in range(n_seq):
            car_re[s] = jnp.broadcast_to(h0r_ref[s], (SUBLANES, STATE_LANES))
            car_im[s] = jnp.broadcast_to(h0i_ref[s], (SUBLANES, STATE_LANES))

    half = GROUPS_PER_LANE_BLOCK * SSM_STATE
    for lb in range(N_LANE_BLOCKS):
        ub = u_ref[:, lb * LANES:(lb + 1) * LANES]
        bu = jnp.dot(ub, bmat_ref[lb], preferred_element_type=F32)
        vre[:, lb * half:(lb + 1) * half] = bu[:, :half]
        vim[:, lb * half:(lb + 1) * half] = bu[:, half:]

    groups = seq_rows // SUBLANES
    for ch in range(STATE_LANES // lane_chunk):
        sl = slice(ch * lane_chunk, (ch + 1) * lane_chunk)
        prc = pr_ref[:, sl]
        pic = pi_ref[:, sl]

        def seq_body(s, _):
            def group_body(r, carry):
                cr, ci = carry
                row = pl.multiple_of(s * seq_rows + r * SUBLANES, SUBLANES)
                vr = vre[pl.ds(row, SUBLANES), sl]
                vi = vim[pl.ds(row, SUBLANES), sl]
                for idx, k in enumerate((1, 2, 4)):
                    mr = mr_ref[idx, :, sl]
                    mi = mi_ref[idx, :, sl]
                    sr = pltpu.roll(vr, k, 0)
                    si = pltpu.roll(vi, k, 0)
                    vr, vi = vr + (mr * sr - mi * si), vi + (mr * si + mi * sr)
                hr = vr + (prc * cr - pic * ci)
                hi = vi + (prc * ci + pic * cr)
                hre[pl.ds(row, SUBLANES), sl] = hr
                him[pl.ds(row, SUBLANES), sl] = hi
                last = SUBLANES - 1
                return (jnp.broadcast_to(hr[last:last + 1], hr.shape),
                        jnp.broadcast_to(hi[last:last + 1], hi.shape))

            cr, ci = lax.fori_loop(0, groups, group_body, (car_re[s, :, sl], car_im[s, :, sl]),
                                   unroll=2)
            car_re[s, :, sl] = cr
            car_im[s, :, sl] = ci
            return 0

        lax.fori_loop(0, n_seq, seq_body, 0)

    for s in range(n_seq):
        hfr_ref[s] = car_re[s, 0:1, :]
        hfi_ref[s] = car_im[s, 0:1, :]

    for lb in range(N_LANE_BLOCKS):
        hcat = jnp.concatenate([hre[:, lb * half:(lb + 1) * half].astype(BF16),
                                him[:, lb * half:(lb + 1) * half].astype(BF16)], axis=1)
        ybuf[:, lb * LANES:(lb + 1) * LANES] = jnp.dot(hcat, cmat_ref[lb], preferred_element_type=F32)
    y = ybuf[...] + d_ref[...] * u_ref[...].astype(F32)
    g = _gelu_tanh(y)
    gate = _sigmoid(jnp.dot(g.astype(BF16), wglu_ref[...], preferred_element_type=F32) + bglu_ref[...])
    y_ref[...] = (g * gate * zs_ref[...].astype(F32)).astype(BF16)


def _s5_branch(u, zs, h0_re, h0_im, bmat, cmat, mr, mi, pr, pi_, d_skip, w_glu_b, b_glu,
               n_batch_steps, n_seq, seq_rows, steps_per_seq):
    n = u.shape[0]
    tile = n_seq * seq_rows
    nb_total = h0_re.shape[0]
    row_spec = pl.BlockSpec((tile, SSM_WIDTH), lambda b, i: (b * steps_per_seq + i, 0))
    st_spec = pl.BlockSpec((n_seq, 1, STATE_LANES), lambda b, i: (b, 0, 0))
    full = lambda a: pl.BlockSpec(a.shape, lambda b, i: (0,) * a.ndim)
    kern = functools.partial(_s5_kernel, n_seq=n_seq, seq_rows=seq_rows, lane_chunk=256)
    return pl.pallas_call(
        kern,
        grid=(n_batch_steps, steps_per_seq),
        in_specs=[row_spec, row_spec, st_spec, st_spec, full(bmat), full(cmat), full(mr), full(mi),
                  full(pr), full(pi_), full(d_skip), full(w_glu_b), full(b_glu)],
        out_specs=[row_spec, st_spec, st_spec],
        out_shape=(jax.ShapeDtypeStruct((n, SSM_WIDTH), BF16),
                   jax.ShapeDtypeStruct((nb_total, 1, STATE_LANES), F32),
                   jax.ShapeDtypeStruct((nb_total, 1, STATE_LANES), F32)),
        scratch_shapes=[pltpu.VMEM((tile, STATE_LANES), F32), pltpu.VMEM((tile, STATE_LANES), F32),
                        pltpu.VMEM((tile, STATE_LANES), F32), pltpu.VMEM((tile, STATE_LANES), F32),
                        pltpu.VMEM((n_seq, SUBLANES, STATE_LANES), F32),
                        pltpu.VMEM((n_seq, SUBLANES, STATE_LANES), F32),
                        pltpu.VMEM((tile, SSM_WIDTH), F32)],
        compiler_params=_cparams(2),
        name="s5_branch",
    )(u, zs, h0_re, h0_im, bmat, cmat, mr, mi, pr, pi_, d_skip, w_glu_b, b_glu)


def _sb_mask_queries(q_rows, qm_ref, tq):
    lane = lax.broadcasted_iota(jnp.int32, (tq, LANES), 1)
    for j in range(N_PAIRS):
        qpair = q_rows(j).astype(F32)
        for hh in range(HEADS_PER_LANE_BLOCK):
            h = j * HEADS_PER_LANE_BLOCK + hh
            in_head = (lane >= hh * HEAD_DIM) & (lane < (hh + 1) * HEAD_DIM)
            qm_ref[h * tq:(h + 1) * tq, :] = jnp.where(in_head, qpair, 0.0).astype(BF16)


def _sb_tile(qm_ref, k_tile, v_tile, valid, first, tri_ref, lb_ref, hl_ref, a_ref, car_ref, acc_ref, tq):
    pair = HEADS_PER_LANE_BLOCK * tq
    valid2 = None if valid is None else jnp.concatenate([valid] * HEADS_PER_LANE_BLOCK, axis=0)
    for j in range(N_PAIRS):
        rows = slice(j * pair, (j + 1) * pair)
        z = lax.dot_general(qm_ref[rows, :], k_tile(j), (((1,), (1,)), ((), ())),
                            preferred_element_type=F32)
        sp = jnp.maximum(z, 0.0) + jnp.log(1.0 + jnp.exp(-jnp.abs(z)))
        lb_ref[rows, :] = z - sp
        if valid2 is not None:
            sp = jnp.where(valid2, sp, 0.0)
        hi = sp.astype(BF16)
        hl_ref[rows, :KEY_TILE] = hi
        hl_ref[rows, KEY_TILE:] = (sp - hi.astype(F32)).astype(BF16)
    cs = jnp.dot(hl_ref[...], tri_ref[...], preferred_element_type=F32)
    cmax = None
    for h in range(N_HEADS):
        rows = slice(h * tq, (h + 1) * tq)
        after = cs[rows, :KEY_TILE]
        total = cs[rows, KEY_TILE:]
        if not first:
            carry = car_ref[rows, :]
            after = after + carry
            total = total + carry
        a = jnp.exp(lb_ref[rows, :] + after)
        if valid is not None:
            a = jnp.where(valid, a, 0.0)
        a_ref[rows, :] = a.astype(BF16)
        car_ref[rows, :] = total
        cmax = total if cmax is None else jnp.maximum(cmax, total)
    lane = lax.broadcasted_iota(jnp.int32, (tq, LANES), 1)
    for j in range(N_PAIRS):
        pv = jnp.dot(a_ref[j * pair:(j + 1) * pair, :], v_tile(j), preferred_element_type=F32)
        out = jnp.where(lane < HEAD_DIM, pv[:tq], pv[tq:])
        if first:
            acc_ref[:, j * LANES:(j + 1) * LANES] = out
        else:
            acc_ref[:, j * LANES:(j + 1) * LANES] += out
    return jnp.max(cmax)


def _sb_scratch(n_blocks, tq):
    rows = N_HEADS * tq
    return [pltpu.VMEM((n_blocks, rows, LANES), BF16),
            pltpu.VMEM((n_blocks, rows, LANES), F32),
            pltpu.VMEM((n_blocks, rows, 2 * KEY_TILE), BF16),
            pltpu.VMEM((n_blocks, rows, LANES), BF16),
            pltpu.VMEM((n_blocks, rows, LANES), F32),
            pltpu.VMEM((n_blocks, tq, ATT_WIDTH), F32),
            pltpu.SMEM((n_blocks,), F32)]


def _sb_prompt_kernel(q_ref, k_ref, v_ref, za_ref, tri_ref, o_ref,
                      qm_ref, lb_ref, hl_ref, a_ref, car_ref, acc_ref, m_ref):
    tq = SB_Q_ROWS
    n_blocks = q_ref.shape[0] // tq
    step_row0 = pl.program_id(1) * (n_blocks * tq)
    row = lax.broadcasted_iota(jnp.int32, (tq, KEY_TILE), 0)
    col = lax.broadcasted_iota(jnp.int32, (tq, KEY_TILE), 1)

    def tiles(ref, start):
        return lambda j: ref[pl.ds(start, KEY_TILE), j * LANES:(j + 1) * LANES]

    def band_start(s):
        return pl.multiple_of(jnp.maximum(step_row0 + (s - 1) * tq, 0), tq)

    for s in range(n_blocks):
        lo = band_start(s)
        _sb_mask_queries(lambda j: q_ref[s * tq:(s + 1) * tq, j * LANES:(j + 1) * LANES], qm_ref.at[s], tq)
        valid = (col - row) < (step_row0 + s * tq - lo)
        m_ref[s] = _sb_tile(qm_ref.at[s], tiles(k_ref, lo), tiles(v_ref, lo), valid, True, tri_ref,
                            lb_ref.at[s], hl_ref.at[s], a_ref.at[s], car_ref.at[s], acc_ref.at[s], tq)

    def more_tiles(s, _):
        def cond(state):
            hi, m = state
            return (hi > 0) & (m > SB_LOG_CUTOFF)

        def body(state):
            hi, _ = state
            lo = pl.multiple_of(jnp.maximum(hi - KEY_TILE, 0), tq)
            valid = col < (hi - lo)
            m = _sb_tile(qm_ref.at[s], tiles(k_ref, lo), tiles(v_ref, lo), valid, False, tri_ref,
                         lb_ref.at[s], hl_ref.at[s], a_ref.at[s], car_ref.at[s], acc_ref.at[s], tq)
            return lo, m

        lax.while_loop(cond, body, (band_start(s), m_ref[s]))
        return 0

    lax.fori_loop(0, n_blocks, more_tiles, 0)
    for s in range(n_blocks):
        rows = slice(s * tq, (s + 1) * tq)
        o_ref[rows, :] = (acc_ref[s] * za_ref[rows, :].astype(F32)).astype(BF16)


def _sb_prompt(q, kb, vb, za, tri, n_batch, seq_len):
    step_rows = SB_STEP_BLOCKS * SB_Q_ROWS
    nq = seq_len // step_rows
    row_spec = pl.BlockSpec((step_rows, ATT_WIDTH), lambda b, i: (b * nq + i, 0))
    seq_spec = pl.BlockSpec((seq_len, ATT_WIDTH), lambda b, i: (b, 0))
    return pl.pallas_call(
        _sb_prompt_kernel,
        grid=(n_batch, nq),
        in_specs=[row_spec, seq_spec, seq_spec, row_spec, pl.BlockSpec(tri.shape, lambda b, i: (0, 0))],
        out_specs=row_spec,
        out_shape=jax.ShapeDtypeStruct(q.shape, BF16),
        scratch_shapes=_sb_scratch(SB_STEP_BLOCKS, SB_Q_ROWS),
        compiler_params=_cparams(2),
        name="sb_prompt",
    )(q, kb, vb, za, tri)


def _sb_sample_kernel(q_ref, kn_ref, vn_ref, kc_ref, vc_ref, za_ref, tri_ref, o_ref,
                      qm_ref, lb_ref, hl_ref, a_ref, car_ref, acc_ref, m_ref):
    tq = q_ref.shape[0]
    n_past = kc_ref.shape[2] // KEY_TILE
    _sb_mask_queries(lambda j: q_ref[:, j * LANES:(j + 1) * LANES], qm_ref.at[0], tq)
    row = lax.broadcasted_iota(jnp.int32, (tq, KEY_TILE), 0)
    col = lax.broadcasted_iota(jnp.int32, (tq, KEY_TILE), 1)
    scratch = (tri_ref, lb_ref.at[0], hl_ref.at[0], a_ref.at[0], car_ref.at[0], acc_ref.at[0], tq)

    new_tile = lambda ref: (lambda j: ref[:, j * LANES:(j + 1) * LANES])
    m0 = _sb_tile(qm_ref.at[0], new_tile(kn_ref), new_tile(vn_ref), col < row, True, *scratch)

    def past_tile(ref, t):
        start = pl.multiple_of(t * KEY_TILE, KEY_TILE)
        return lambda j: ref[0, 0, pl.ds(start, KEY_TILE), j * LANES:(j + 1) * LANES].astype(BF16)

    def cond(state):
        t, m = state
        return (t >= 0) & (m > SB_LOG_CUTOFF)

    def body(state):
        t, _ = state
        m = _sb_tile(qm_ref.at[0], past_tile(kc_ref, t), past_tile(vc_ref, t), None, False, *scratch)
        return t - 1, m

    lax.while_loop(cond, body, (n_past - 1, m0))
    o_ref[...] = (acc_ref[0] * za_ref[...].astype(F32)).astype(BF16)


def _sb_sample(q, k_new, v_new, k_past, v_past, layer, za, tri, n_batch, t_new):
    past = k_past.shape[2]
    row_spec = pl.BlockSpec((t_new, ATT_WIDTH), lambda b: (b, 0))
    new_spec = pl.BlockSpec((KEY_TILE, ATT_WIDTH), lambda b: (b, 0))
    past_spec = pl.BlockSpec((1, 1, past, ATT_WIDTH), lambda b: (layer, b, 0, 0))
    return pl.pallas_call(
        _sb_sample_kernel,
        grid=(n_batch,),
        in_specs=[row_spec, new_spec, new_spec, past_spec, past_spec, row_spec,
                  pl.BlockSpec(tri.shape, lambda b: (0, 0))],
        out_specs=row_spec,
        out_shape=jax.ShapeDtypeStruct(q.shape, BF16),
        scratch_shapes=_sb_scratch(1, t_new),
        compiler_params=_cparams(1),
        name="sb_sample",
    )(q, k_new, v_new, k_past, v_past, za, tri)


def _outproj_kernel(x_ref, ys_ref, ya_ref, gate_ref, w_ref, o_ref):
    mix = jnp.dot(ys_ref[...], w_ref[:SSM_WIDTH, :], preferred_element_type=F32)
    mix = mix + jnp.dot(ya_ref[...], w_ref[SSM_WIDTH:, :], preferred_element_type=F32)
    o_ref[...] = x_ref[...] + gate_ref[0] * mix


def _out_projection(x2d, ys, ya, gate, w_out_b, tm, rows_per_seq):
    n = x2d.shape[0]
    if rows_per_seq is None:
        mod_spec = pl.BlockSpec((1, tm, D_MODEL), lambda i: (0, i, 0))
    else:
        tps = rows_per_seq // tm
        mod_spec = pl.BlockSpec((1, 1, D_MODEL), lambda i: (i // tps, 0, 0))
    row_spec = lambda w: pl.BlockSpec((tm, w), lambda i: (i, 0))
    return pl.pallas_call(
        _outproj_kernel,
        grid=(n // tm,),
        in_specs=[row_spec(D_MODEL), row_spec(SSM_WIDTH), row_spec(ATT_WIDTH), mod_spec,
                  pl.BlockSpec(w_out_b.shape, lambda i: (0, 0))],
        out_specs=row_spec(D_MODEL),
        out_shape=jax.ShapeDtypeStruct(x2d.shape, F32),
        compiler_params=_cparams(1),
        name="out_proj",
    )(x2d, ys, ya, gate, w_out_b)


def _sb_sum_matrix():
    j = jnp.arange(2 * KEY_TILE)[:, None] % KEY_TILE
    s = jnp.arange(2 * KEY_TILE)[None, :]
    return -((s >= KEY_TILE) | (j > s)).astype(BF16)


def _head_sum_matrix():
    a = jnp.arange(ATT_WIDTH)
    return (a[:, None] // HEAD_DIM == a[None, :] // HEAD_DIM).astype(BF16)


def kernel(x_prompt, x_sample, c_prompt, c_sample, cache_k, cache_v, state_ssm_re, state_ssm_im, norm_g, w_mod, b_mod, w_in, ssm_a_re, ssm_a_im, ssm_log_dt, ssm_b_re, ssm_b_im, ssm_c_re, ssm_c_im, ssm_d, w_glu, b_glu, q_norm_g, k_norm_g, w_out):
    depth = w_in.shape[0]
    nb, seq_len, _ = x_prompt.shape
    ns, t_new, _ = x_sample.shape
    past = cache_k.shape[2]
    tm = 512

    mod = _modulation(jnp.concatenate([c_prompt, c_sample], axis=0), w_mod, b_mod)
    pw_re, pw_im, bb_re, bb_im = _discretise(ssm_a_re, ssm_a_im, ssm_log_dt, ssm_b_re, ssm_b_im)
    bmat, cmat, mr, mi, pr, pi_ = _s5_matrices(pw_re, pw_im, bb_re, bb_im, ssm_c_re, ssm_c_im)

    w_in_b = w_in.astype(BF16)
    w_out_b = w_out.astype(BF16)
    w_glu_b = w_glu.astype(BF16)
    tri = _sb_sum_matrix()
    hsum = _head_sum_matrix()

    xp = x_prompt.reshape(nb * seq_len, D_MODEL)
    xs = x_sample.reshape(ns * t_new, D_MODEL)
    zero_state = jnp.zeros((nb, 1, STATE_LANES), F32)
    kc = cache_k.reshape(depth, ns, past, ATT_WIDTH)
    vc = cache_v.reshape(depth, ns, past, ATT_WIDTH)

    outs = {name: [] for name in ("pr", "pi", "sr", "si")}
    pkv = skv = None
    for l in range(depth):
        g2 = norm_g[l].reshape(1, D_MODEL)
        qg = jnp.tile(q_norm_g[l], N_HEADS).reshape(1, ATT_WIDTH)
        kg = jnp.tile(k_norm_g[l], N_HEADS).reshape(1, ATT_WIDTH)
        d2 = ssm_d[l].reshape(1, SSM_WIDTH)
        bg2 = b_glu[l].reshape(1, SSM_WIDTH)
        s5_w = (bmat[l], cmat[l], mr[l], mi[l], pr[l], pi_[l], d2, w_glu_b[l], bg2)

        mp = mod[l, :nb].reshape(nb, 1, 3 * D_MODEL)
        shift, scale, gate = (mp[:, :, i * D_MODEL:(i + 1) * D_MODEL] for i in range(3))
        u, zs, q, k_all, v_all, kb, vb, za = _in_projection(xp, shift, scale, g2, w_in_b[l], hsum, qg, kg,
                                                            tm, seq_len, l, depth, pkv)
        pkv = (k_all, v_all)
        ys, hfr, hfi = _s5_branch(u, zs, zero_state, zero_state, *s5_w,
                                  n_batch_steps=nb, n_seq=1, seq_rows=tm, steps_per_seq=seq_len // tm)
        ya = _sb_prompt(q, kb, vb, za, tri, nb, seq_len)
        xp = _out_projection(xp, ys, ya, gate, w_out_b[l], tm, seq_len)
        outs["pr"].append(hfr.reshape(nb, SSM_GROUPS, SSM_STATE))
        outs["pi"].append(hfi.reshape(nb, SSM_GROUPS, SSM_STATE))

        ms = jnp.repeat(mod[l, nb:], t_new, axis=0).reshape(1, ns * t_new, 3 * D_MODEL)
        shift, scale, gate = (ms[:, :, i * D_MODEL:(i + 1) * D_MODEL] for i in range(3))
        ts = ns * t_new
        u, zs, q, k_all, v_all, kb, vb, za = _in_projection(xs, shift, scale, g2, w_in_b[l], hsum, qg, kg,
                                                            ts, None, l, depth, skv)
        skv = (k_all, v_all)
        h0r = state_ssm_re[l].reshape(ns, 1, STATE_LANES)
        h0i = state_ssm_im[l].reshape(ns, 1, STATE_LANES)
        ys, hfr, hfi = _s5_branch(u, zs, h0r, h0i, *s5_w,
                                  n_batch_steps=1, n_seq=ns, seq_rows=t_new, steps_per_seq=1)
        pad = lambda a: jnp.pad(a.reshape(ns, t_new, ATT_WIDTH),
                                ((0, 0), (0, KEY_TILE - t_new), (0, 0))).reshape(ns * KEY_TILE, ATT_WIDTH)
        ya = _sb_sample(q, pad(kb), pad(vb), kc, vc, l, za, tri, ns, t_new)
        xs = _out_projection(xs, ys, ya, gate, w_out_b[l], ts, None)
        outs["sr"].append(hfr.reshape(ns, SSM_GROUPS, SSM_STATE))
        outs["si"].append(hfi.reshape(ns, SSM_GROUPS, SSM_STATE))

    st = lambda name: jnp.stack(outs[name])
    heads_p = lambda a: a.reshape(depth, nb, seq_len, N_HEADS, HEAD_DIM)
    heads_s = lambda a: a.reshape(depth, ns, t_new, N_HEADS, HEAD_DIM)
    return (xp.reshape(nb, seq_len, D_MODEL), xs.reshape(ns, t_new, D_MODEL),
            heads_p(pkv[0]), heads_p(pkv[1]), st("pr"), st("pi"),
            heads_s(skv[0]), heads_s(skv[1]), st("sr"), st("si"))
```

```python
import functools
import math
from typing import Any, NamedTuple

import jax
import jax.numpy as jnp
from jax import lax
from jax.experimental import pallas as pl
from jax.experimental.pallas import tpu as pltpu

F32 = jnp.float32
BF16 = jnp.bfloat16

D_MODEL = 1024
SSM_WIDTH = 512
SSM_GROUP = 16
SSM_GROUPS = 32
SSM_STATE = 64
STATE_LANES = SSM_GROUPS * SSM_STATE
ATT_WIDTH = 512
HEAD_DIM = 64
N_HEADS = 8
IN_WIDTH = 2 * SSM_WIDTH + 4 * ATT_WIDTH
EPS = 1e-6

LANES = 128
SUBLANES = 8
HEADS_PER_LANE_BLOCK = LANES // HEAD_DIM
GROUPS_PER_LANE_BLOCK = LANES // SSM_GROUP
N_LANE_BLOCKS = SSM_WIDTH // LANES
N_PAIRS = N_HEADS // HEADS_PER_LANE_BLOCK
KEY_TILE = 128
SB_Q_ROWS = 32
SB_STEP_BLOCKS = 8
SB_LOG_CUTOFF = -40.0
VMEM_LIMIT = 56 * 1024 * 1024


def _cparams(n_axes):
    return pltpu.CompilerParams(dimension_semantics=("arbitrary",) * n_axes,
                                vmem_limit_bytes=VMEM_LIMIT)


def _silu(x):
    return x * (1.0 / (1.0 + jnp.exp(-x)))


def _sigmoid(x):
    return 1.0 / (1.0 + jnp.exp(-x))


def _gelu_tanh(x):
    c = math.sqrt(2.0 / math.pi)
    return 0.5 * x * (1.0 + jnp.tanh(c * (x + 0.044715 * (x * x * x))))


def _mod_kernel(c_ref, w_ref, b_ref, o_ref):
    c = c_ref[...]
    a = _silu(c)
    o_ref[0] = jnp.dot(a, w_ref[0], preferred_element_type=F32,
                       precision=lax.Precision.HIGHEST) + b_ref[0]


def _modulation(c_all, w_mod, b_mod):
    depth = w_mod.shape[0]
    n = c_all.shape[0]
    nt = 3
    return pl.pallas_call(
        _mod_kernel,
        grid=(depth, nt),
        in_specs=[pl.BlockSpec((n, D_MODEL), lambda l, j: (0, 0)),
                  pl.BlockSpec((1, D_MODEL, D_MODEL), lambda l, j: (l, 0, j)),
                  pl.BlockSpec((1, 1, D_MODEL), lambda l, j: (l, 0, j))],
        out_specs=pl.BlockSpec((1, n, D_MODEL), lambda l, j: (l, 0, j)),
        out_shape=jax.ShapeDtypeStruct((depth, n, 3 * D_MODEL), F32),
        compiler_params=_cparams(2),
        name="modulation",
    )(c_all, w_mod, b_mod.reshape(depth, 1, 3 * D_MODEL))


def _disc_kernel(lr_ref, li_ref, ldt_ref, bre_ref, bim_ref, pwr_ref, pwi_ref, bbr_ref, bbi_ref):
    lr = lr_ref[...]
    li = li_ref[...]
    dt = jnp.exp(ldt_ref[...])
    mag = jnp.exp(lr * dt)
    ang = li * dt
    ab_re = mag * jnp.cos(ang)
    ab_im = mag * jnp.sin(ang)
    den = lr * lr + li * li
    nr = ab_re - 1.0
    f_re = (nr * lr + ab_im * li) / den
    f_im = (ab_im * lr - nr * li) / den
    bre = bre_ref[...]
    bim = bim_ref[...]
    bbr_ref[...] = f_re[None] * bre - f_im[None] * bim
    bbi_ref[...] = f_re[None] * bim + f_im[None] * bre
    pr, pi_ = ab_re, ab_im
    pwr_ref[0] = pr
    pwi_ref[0] = pi_
    for k in range(1, SUBLANES):
        pr, pi_ = pr * ab_re - pi_ * ab_im, pr * ab_im + pi_ * ab_re
        pwr_ref[k] = pr
        pwi_ref[k] = pi_


def _discretise(a_re, a_im, log_dt, b_re, b_im):
    depth = a_re.shape[0]
    rows = depth * SSM_GROUPS
    lr = a_re.reshape(rows, SSM_STATE)
    li = a_im.reshape(rows, SSM_STATE)
    ldt = jnp.broadcast_to(log_dt.reshape(rows, 1), (rows, SSM_STATE))
    bre = b_re.transpose(3, 0, 1, 2).reshape(SSM_GROUP, rows, SSM_STATE)
    bim = b_im.transpose(3, 0, 1, 2).reshape(SSM_GROUP, rows, SSM_STATE)
    outs = pl.pallas_call(
        _disc_kernel,
        out_shape=(jax.ShapeDtypeStruct((SUBLANES, rows, SSM_STATE), F32),
                   jax.ShapeDtypeStruct((SUBLANES, rows, SSM_STATE), F32),
                   jax.ShapeDtypeStruct((SSM_GROUP, rows, SSM_STATE), F32),
                   jax.ShapeDtypeStruct((SSM_GROUP, rows, SSM_STATE), F32)),
        name="s5_discretise",
    )(lr, li, ldt, bre, bim)
    pw_re, pw_im, bb_re, bb_im = outs
    pw_re = pw_re.reshape(SUBLANES, depth, STATE_LANES).transpose(1, 0, 2)
    pw_im = pw_im.reshape(SUBLANES, depth, STATE_LANES).transpose(1, 0, 2)
    bb_re = bb_re.reshape(SSM_GROUP, depth, SSM_GROUPS, SSM_STATE)
    bb_im = bb_im.reshape(SSM_GROUP, depth, SSM_GROUPS, SSM_STATE)
    return pw_re, pw_im, bb_re, bb_im


def _s5_matrices(pw_re, pw_im, bb_re, bb_im, c_re, c_im):
    depth = pw_re.shape[0]
    eye = jnp.eye(GROUPS_PER_LANE_BLOCK, dtype=F32)

    def b_block(bb):
        x = bb.reshape(SSM_GROUP, depth, N_LANE_BLOCKS, GROUPS_PER_LANE_BLOCK, SSM_STATE)
        x = jnp.einsum('clbgp,gh->lbgchp', x, eye)
        return x.reshape(depth, N_LANE_BLOCKS, LANES, GROUPS_PER_LANE_BLOCK * SSM_STATE)

    def c_block(c):
        x = c.reshape(depth, N_LANE_BLOCKS, GROUPS_PER_LANE_BLOCK, SSM_GROUP, SSM_STATE)
        x = jnp.einsum('lbgcp,gh->lbgphc', x, eye)
        return x.reshape(depth, N_LANE_BLOCKS, GROUPS_PER_LANE_BLOCK * SSM_STATE, LANES)

    bmat = jnp.concatenate([b_block(bb_re), b_block(bb_im)], axis=-1).astype(BF16)
    cmat = jnp.concatenate([c_block(c_re), -c_block(c_im)], axis=-2).astype(BF16)

    row = jnp.arange(SUBLANES)[None, :, None]

    def step_mult(pw):
        return jnp.stack([jnp.where(row >= k, pw[:, k - 1][:, None, :], 0.0) for k in (1, 2, 4)], axis=1)

    return bmat, cmat, step_mult(pw_re), step_mult(pw_im), pw_re, pw_im


def _heads_major(x):
    heads = jnp.stack([x[:, h * HEAD_DIM:(h + 1) * HEAD_DIM] for h in range(N_HEADS)], axis=0)
    return pltpu.einshape("htd->thd", heads)


def _inproj_kernel(x_ref, shift_ref, scale_ref, g_ref, w_ref, hsum_ref, qg_ref, kg_ref, *rest):
    u_ref, zs_ref, q_ref, k_ref, v_ref, kb_ref, vb_ref, za_ref = rest[-8:]
    x = x_ref[...]
    ms = jnp.mean(x * x, axis=-1, keepdims=True)
    h = x * lax.rsqrt(ms + EPS) * g_ref[...]
    h = h * (1.0 + scale_ref[0]) + shift_ref[0]
    hb = h.astype(BF16)

    def proj(c):
        return jnp.dot(hb, w_ref[:, c * SSM_WIDTH:(c + 1) * SSM_WIDTH], preferred_element_type=F32)

    def head_norm(p, g):
        ss = jnp.dot((p * p).astype(BF16), hsum_ref[...], preferred_element_type=F32)
        return p * lax.rsqrt(ss * (1.0 / HEAD_DIM) + EPS) * g

    u_ref[...] = proj(0).astype(BF16)
    zs_ref[...] = _silu(proj(1)).astype(BF16)
    q = head_norm(proj(2), qg_ref[...])
    q_ref[...] = (q * (HEAD_DIM ** -0.5)).astype(BF16)
    k = head_norm(proj(3), kg_ref[...])
    k_ref[0] = _heads_major(k)
    kb_ref[...] = k.astype(BF16)
    v = proj(4)
    v_ref[0] = _heads_major(v)
    vb_ref[...] = v.astype(BF16)
    za_ref[...] = _silu(proj(5)).astype(BF16)


def _in_projection(x2d, shift, scale, norm_g, w_in_b, hsum, q_g, k_g, tm, rows_per_seq,
                   layer, depth, kv_all):
    n = x2d.shape[0]
    nt = n // tm
    if rows_per_seq is None:
        mod_spec = pl.BlockSpec((1, tm, D_MODEL), lambda i: (0, i, 0))
    else:
        tps = rows_per_seq // tm
        mod_spec = pl.BlockSpec((1, 1, D_MODEL), lambda i: (i // tps, 0, 0))
    row_spec = lambda w: pl.BlockSpec((tm, w), lambda i: (i, 0))
    full = lambda a: pl.BlockSpec(a.shape, lambda i: (0,) * a.ndim)
    bf = jax.ShapeDtypeStruct((n, SSM_WIDTH), BF16)
    f3 = jax.ShapeDtypeStruct((depth, n, N_HEADS, HEAD_DIM), F32)
    kv_spec = pl.BlockSpec((1, tm, N_HEADS, HEAD_DIM), lambda i: (layer, i, 0, 0))
    in_specs = [row_spec(D_MODEL), mod_spec, mod_spec, full(norm_g), full(w_in_b), full(hsum),
                full(q_g), full(k_g)]
    args = [x2d, shift, scale, norm_g, w_in_b, hsum, q_g, k_g]
    aliases = {}
    if kv_all is not None:
        in_specs += [pl.BlockSpec(memory_space=pl.ANY)] * 2
        args += list(kv_all)
        aliases = {len(args) - 2: 3, len(args) - 1: 4}
    return pl.pallas_call(
        _inproj_kernel,
        grid=(nt,),
        in_specs=in_specs,
        out_specs=[row_spec(SSM_WIDTH)] * 3 + [kv_spec] * 2 + [row_spec(SSM_WIDTH)] * 3,
        out_shape=(bf, bf, bf, f3, f3, bf, bf, bf),
        input_output_aliases=aliases,
        compiler_params=_cparams(1),
        name="in_proj",
    )(*args)


def _s5_kernel(u_ref, zs_ref, h0r_ref, h0i_ref, bmat_ref, cmat_ref, mr_ref, mi_ref, pr_ref, pi_ref,
               d_ref, wglu_ref, bglu_ref,
               y_ref, hfr_ref, hfi_ref,
               vre, vim, hre, him, car_re, car_im, ybuf, *, n_seq, seq_rows, lane_chunk):
    i = pl.program_id(1)
    rows = n_seq * seq_rows

    @pl.when(i == 0)
    def _():
        for s in range(n_seq):
            car_re[s] = jnp.broadcast_to(h0r_ref[s], (SUBLANES, STATE_LANES))
            car_im[s] = jnp.broadcast_to(h0i_ref[s], (SUBLANES, STATE_LANES))

    half = GROUPS_PER_LANE_BLOCK * SSM_STATE
    for lb in range(N_LANE_BLOCKS):
        ub = u_ref[:, lb * LANES:(lb + 1) * LANES]
        bu = jnp.dot(ub, bmat_ref[lb], preferred_element_type=F32)
        vre[:, lb * half:(lb + 1) * half] = bu[:, :half]
        vim[:, lb * half:(lb + 1) * half] = bu[:, half:]

    groups = seq_rows // SUBLANES
    for ch in range(STATE_LANES // lane_chunk):
        sl = slice(ch * lane_chunk, (ch + 1) * lane_chunk)
        prc = pr_ref[:, sl]
        pic = pi_ref[:, sl]

        def seq_body(s, _):
            def group_body(r, carry):
                cr, ci = carry
                row = pl.multiple_of(s * seq_rows + r * SUBLANES, SUBLANES)
                vr = vre[pl.ds(row, SUBLANES), sl]
                vi = vim[pl.ds(row, SUBLANES), sl]
                for idx, k in enumerate((1, 2, 4)):
                    mr = mr_ref[idx, :, sl]
                    mi = mi_ref[idx, :, sl]
                    sr = pltpu.roll(vr, k, 0)
                    si = pltpu.roll(vi, k, 0)
                    vr, vi = vr + (mr * sr - mi * si), vi + (mr * si + mi * sr)
                hr = vr + (prc * cr - pic * ci)
                hi = vi + (prc * ci + pic * cr)
                hre[pl.ds(row, SUBLANES), sl] = hr
                him[pl.ds(row, SUBLANES), sl] = hi
                last = SUBLANES - 1
                return (jnp.broadcast_to(hr[last:last + 1], hr.shape),
                        jnp.broadcast_to(hi[last:last + 1], hi.shape))

            cr, ci = lax.fori_loop(0, groups, group_body, (car_re[s, :, sl], car_im[s, :, sl]),
                                   unroll=2)
            car_re[s, :, sl] = cr
            car_im[s, :, sl] = ci
            return 0

        lax.fori_loop(0, n_seq, seq_body, 0)

    for s in range(n_seq):
        hfr_ref[s] = car_re[s, 0:1, :]
        hfi_ref[s] = car_im[s, 0:1, :]

    for lb in range(N_LANE_BLOCKS):
        hcat = jnp.concatenate([hre[:, lb * half:(lb + 1) * half].astype(BF16),
                                him[:, lb * half:(lb + 1) * half].astype(BF16)], axis=1)
        ybuf[:, lb * LANES:(lb + 1) * LANES] = jnp.dot(hcat, cmat_ref[lb], preferred_element_type=F32)
    y = ybuf[...] + d_ref[...] * u_ref[...].astype(F32)
    g = _gelu_tanh(y)
    gate = _sigmoid(jnp.dot(g.astype(BF16), wglu_ref[...], preferred_element_type=F32) + bglu_ref[...])
    y_ref[...] = (g * gate * zs_ref[...].astype(F32)).astype(BF16)


def _s5_branch(u, zs, h0_re, h0_im, bmat, cmat, mr, mi, pr, pi_, d_skip, w_glu_b, b_glu,
               n_batch_steps, n_seq, seq_rows, steps_per_seq):
    n = u.shape[0]
    tile = n_seq * seq_rows
    nb_total = h0_re.shape[0]
    row_spec = pl.BlockSpec((tile, SSM_WIDTH), lambda b, i: (b * steps_per_seq + i, 0))
    st_spec = pl.BlockSpec((n_seq, 1, STATE_LANES), lambda b, i: (b, 0, 0))
    full = lambda a: pl.BlockSpec(a.shape, lambda b, i: (0,) * a.ndim)
    kern = functools.partial(_s5_kernel, n_seq=n_seq, seq_rows=seq_rows, lane_chunk=256)
    return pl.pallas_call(
        kern,
        grid=(n_batch_steps, steps_per_seq),
        in_specs=[row_spec, row_spec, st_spec, st_spec, full(bmat), full(cmat), full(mr), full(mi),
                  full(pr), full(pi_), full(d_skip), full(w_glu_b), full(b_glu)],
        out_specs=[row_spec, st_spec, st_spec],
        out_shape=(jax.ShapeDtypeStruct((n, SSM_WIDTH), BF16),
                   jax.ShapeDtypeStruct((nb_total, 1, STATE_LANES), F32),
                   jax.ShapeDtypeStruct((nb_total, 1, STATE_LANES), F32)),
        scratch_shapes=[pltpu.VMEM((tile, STATE_LANES), F32), pltpu.VMEM((tile, STATE_LANES), F32),
                        pltpu.VMEM((tile, STATE_LANES), F32), pltpu.VMEM((tile, STATE_LANES), F32),
                        pltpu.VMEM((n_seq, SUBLANES, STATE_LANES), F32),
                        pltpu.VMEM((n_seq, SUBLANES, STATE_LANES), F32),
                        pltpu.VMEM((tile, SSM_WIDTH), F32)],
        compiler_params=_cparams(2),
        name="s5_branch",
    )(u, zs, h0_re, h0_im, bmat, cmat, mr, mi, pr, pi_, d_skip, w_glu_b, b_glu)


def _sb_mask_queries(q_rows, qm_ref, tq):
    lane = lax.broadcasted_iota(jnp.int32, (tq, LANES), 1)
    for j in range(N_PAIRS):
        qpair = q_rows(j).astype(F32)
        for hh in range(HEADS_PER_LANE_BLOCK):
            h = j * HEADS_PER_LANE_BLOCK + hh
            in_head = (lane >= hh * HEAD_DIM) & (lane < (hh + 1) * HEAD_DIM)
            qm_ref[h * tq:(h + 1) * tq, :] = jnp.where(in_head, qpair, 0.0).astype(BF16)


class _SbTile(NamedTuple):
    tq: int
    k_tile: Any
    v_tile: Any
    valid: Any
    first: bool
    qm: Any
    lb: Any
    hl: Any
    a: Any
    car: Any
    acc: Any


def _sb_scores(t):
    pair = HEADS_PER_LANE_BLOCK * t.tq
    valid2 = None if t.valid is None else jnp.concatenate([t.valid] * HEADS_PER_LANE_BLOCK, axis=0)
    for j in range(N_PAIRS):
        rows = slice(j * pair, (j + 1) * pair)
        z = lax.dot_general(t.qm[rows, :], t.k_tile(j), (((1,), (1,)), ((), ())),
                            preferred_element_type=F32)
        sp = jnp.maximum(z, 0.0) + jnp.log(1.0 + jnp.exp(-jnp.abs(z)))
        t.lb[rows, :] = z - sp
        if valid2 is not None:
            sp = jnp.where(valid2, sp, 0.0)
        hi = sp.astype(BF16)
        t.hl[rows, :KEY_TILE] = hi
        t.hl[rows, KEY_TILE:] = (sp - hi.astype(F32)).astype(BF16)


def _sb_weights(t, tri_ref):
    cs = jnp.dot(t.hl[...], tri_ref[...], preferred_element_type=F32)
    cmax = None
    for h in range(N_HEADS):
        rows = slice(h * t.tq, (h + 1) * t.tq)
        after = cs[rows, :KEY_TILE]
        total = cs[rows, KEY_TILE:]
        if not t.first:
            carry = t.car[rows, :]
            after = after + carry
            total = total + carry
        a = jnp.exp(t.lb[rows, :] + after)
        if t.valid is not None:
            a = jnp.where(t.valid, a, 0.0)
        t.a[rows, :] = a.astype(BF16)
        t.car[rows, :] = total
        cmax = total if cmax is None else jnp.maximum(cmax, total)
    return jnp.max(cmax)


def _sb_output(t):
    pair = HEADS_PER_LANE_BLOCK * t.tq
    lane = lax.broadcasted_iota(jnp.int32, (t.tq, LANES), 1)
    for j in range(N_PAIRS):
        pv = jnp.dot(t.a[j * pair:(j + 1) * pair, :], t.v_tile(j), preferred_element_type=F32)
        out = jnp.where(lane < HEAD_DIM, pv[:t.tq], pv[t.tq:])
        if t.first:
            t.acc[:, j * LANES:(j + 1) * LANES] = out
        else:
            t.acc[:, j * LANES:(j + 1) * LANES] += out


def _sb_tiles(tiles, tri_ref):
    for t in tiles:
        _sb_scores(t)
    ms = [_sb_weights(t, tri_ref) for t in tiles]
    for t in tiles:
        _sb_output(t)
    return ms


def _sb_scratch(n_blocks, tq):
    rows = N_HEADS * tq
    return [pltpu.VMEM((n_blocks, rows, LANES), BF16),
            pltpu.VMEM((n_blocks, rows, LANES), F32),
            pltpu.VMEM((n_blocks, rows, 2 * KEY_TILE), BF16),
            pltpu.VMEM((n_blocks, rows, LANES), BF16),
            pltpu.VMEM((n_blocks, rows, LANES), F32),
            pltpu.VMEM((n_blocks, tq, ATT_WIDTH), F32),
            pltpu.SMEM((n_blocks,), F32)]


def _sb_prompt_kernel(q_ref, k_ref, v_ref, za_ref, tri_ref, o_ref,
                      qm_ref, lb_ref, hl_ref, a_ref, car_ref, acc_ref, m_ref):
    tq = SB_Q_ROWS
    n_blocks = q_ref.shape[0] // tq
    step_row0 = pl.program_id(1) * (n_blocks * tq)
    row = lax.broadcasted_iota(jnp.int32, (tq, KEY_TILE), 0)
    col = lax.broadcasted_iota(jnp.int32, (tq, KEY_TILE), 1)

    def tiles(ref, start):
        return lambda j: ref[pl.ds(start, KEY_TILE), j * LANES:(j + 1) * LANES]

    def band_start(s):
        return pl.multiple_of(jnp.maximum(step_row0 + (s + 1) * tq - KEY_TILE, 0), tq)

    def tile(s, lo, valid, first):
        return _SbTile(tq, tiles(k_ref, lo), tiles(v_ref, lo), valid, first, qm_ref.at[s], lb_ref.at[s],
                       hl_ref.at[s], a_ref.at[s], car_ref.at[s], acc_ref.at[s])

    band = []
    for s in range(n_blocks):
        lo = band_start(s)
        _sb_mask_queries(lambda j: q_ref[s * tq:(s + 1) * tq, j * LANES:(j + 1) * LANES], qm_ref.at[s], tq)
        valid = (col - row) < (step_row0 + s * tq - lo)
        band.append(tile(s, lo, valid, True))
    for s, m in enumerate(_sb_tiles(band, tri_ref)):
        m_ref[s] = m

    def more_tiles(s, _):
        def cond(state):
            hi, m = state
            return (hi > 0) & (m > SB_LOG_CUTOFF)

        def body(state):
            hi, _ = state
            lo = pl.multiple_of(jnp.maximum(hi - KEY_TILE, 0), tq)
            valid = col < (hi - lo)
            (m,) = _sb_tiles([tile(s, lo, valid, False)], tri_ref)
            return lo, m

        lax.while_loop(cond, body, (band_start(s), m_ref[s]))
        return 0

    lax.fori_loop(0, n_blocks, more_tiles, 0)
    for s in range(n_blocks):
        rows = slice(s * tq, (s + 1) * tq)
        o_ref[rows, :] = (acc_ref[s] * za_ref[rows, :].astype(F32)).astype(BF16)


def _sb_prompt(q, kb, vb, za, tri, n_batch, seq_len):
    step_rows = SB_STEP_BLOCKS * SB_Q_ROWS
    nq = seq_len // step_rows
    row_spec = pl.BlockSpec((step_rows, ATT_WIDTH), lambda b, i: (b * nq + i, 0))
    seq_spec = pl.BlockSpec((seq_len, ATT_WIDTH), lambda b, i: (b, 0))
    return pl.pallas_call(
        _sb_prompt_kernel,
        grid=(n_batch, nq),
        in_specs=[row_spec, seq_spec, seq_spec, row_spec, pl.BlockSpec(tri.shape, lambda b, i: (0, 0))],
        out_specs=row_spec,
        out_shape=jax.ShapeDtypeStruct(q.shape, BF16),
        scratch_shapes=_sb_scratch(SB_STEP_BLOCKS, SB_Q_ROWS),
        compiler_params=_cparams(2),
        name="sb_prompt",
    )(q, kb, vb, za, tri)


def _sb_sample_kernel(q_ref, kn_ref, vn_ref, kc_ref, vc_ref, za_ref, tri_ref, o_ref,
                      qm_ref, lb_ref, hl_ref, a_ref, car_ref, acc_ref, m_ref):
    tq = q_ref.shape[0]
    n_past = kc_ref.shape[2] // KEY_TILE
    _sb_mask_queries(lambda j: q_ref[:, j * LANES:(j + 1) * LANES], qm_ref.at[0], tq)
    row = lax.broadcasted_iota(jnp.int32, (tq, KEY_TILE), 0)
    col = lax.broadcasted_iota(jnp.int32, (tq, KEY_TILE), 1)
    scratch = (qm_ref.at[0], lb_ref.at[0], hl_ref.at[0], a_ref.at[0], car_ref.at[0], acc_ref.at[0])

    new_tile = lambda ref: (lambda j: ref[:, j * LANES:(j + 1) * LANES])
    (m0,) = _sb_tiles([_SbTile(tq, new_tile(kn_ref), new_tile(vn_ref), col < row, True, *scratch)], tri_ref)

    def past_tile(ref, t):
        start = pl.multiple_of(t * KEY_TILE, KEY_TILE)
        heads = pltpu.einshape("thd->htd", ref[0, 0, pl.ds(start, KEY_TILE), :, :])
        blocks = [jnp.concatenate([heads[j * HEADS_PER_LANE_BLOCK + hh] for hh in range(HEADS_PER_LANE_BLOCK)],
                                  axis=1).astype(BF16) for j in range(N_PAIRS)]
        return lambda j: blocks[j]

    def cond(state):
        t, m = state
        return (t >= 0) & (m > SB_LOG_CUTOFF)

    def body(state):
        t, _ = state
        (m,) = _sb_tiles([_SbTile(tq, past_tile(kc_ref, t), past_tile(vc_ref, t), None, False, *scratch)],
                         tri_ref)
        return t - 1, m

    lax.while_loop(cond, body, (n_past - 1, m0))
    o_ref[...] = (acc_ref[0] * za_ref[...].astype(F32)).astype(BF16)


def _sb_sample(q, k_new, v_new, k_past, v_past, layer, za, tri, n_batch, t_new):
    past = k_past.shape[2]
    row_spec = pl.BlockSpec((t_new, ATT_WIDTH), lambda b: (b, 0))
    new_spec = pl.BlockSpec((KEY_TILE, ATT_WIDTH), lambda b: (b, 0))
    past_spec = pl.BlockSpec((1, 1, past, N_HEADS, HEAD_DIM), lambda b: (layer, b, 0, 0, 0))
    return pl.pallas_call(
        _sb_sample_kernel,
        grid=(n_batch,),
        in_specs=[row_spec, new_spec, new_spec, past_spec, past_spec, row_spec,
                  pl.BlockSpec(tri.shape, lambda b: (0, 0))],
        out_specs=row_spec,
        out_shape=jax.ShapeDtypeStruct(q.shape, BF16),
        scratch_shapes=_sb_scratch(1, t_new),
        compiler_params=_cparams(1),
        name="sb_sample",
    )(q, k_new, v_new, k_past, v_past, za, tri)


def _outproj_kernel(x_ref, ys_ref, ya_ref, gate_ref, w_ref, o_ref):
    mix = jnp.dot(ys_ref[...], w_ref[:SSM_WIDTH, :], preferred_element_type=F32)
    mix = mix + jnp.dot(ya_ref[...], w_ref[SSM_WIDTH:, :], preferred_element_type=F32)
    o_ref[...] = x_ref[...] + gate_ref[0] * mix


def _out_projection(x2d, ys, ya, gate, w_out_b, tm, rows_per_seq):
    n = x2d.shape[0]
    if rows_per_seq is None:
        mod_spec = pl.BlockSpec((1, tm, D_MODEL), lambda i: (0, i, 0))
    else:
        tps = rows_per_seq // tm
        mod_spec = pl.BlockSpec((1, 1, D_MODEL), lambda i: (i // tps, 0, 0))
    row_spec = lambda w: pl.BlockSpec((tm, w), lambda i: (i, 0))
    return pl.pallas_call(
        _outproj_kernel,
        grid=(n // tm,),
        in_specs=[row_spec(D_MODEL), row_spec(SSM_WIDTH), row_spec(ATT_WIDTH), mod_spec,
                  pl.BlockSpec(w_out_b.shape, lambda i: (0, 0))],
        out_specs=row_spec(D_MODEL),
        out_shape=jax.ShapeDtypeStruct(x2d.shape, F32),
        compiler_params=_cparams(1),
        name="out_proj",
    )(x2d, ys, ya, gate, w_out_b)


def _sb_sum_matrix():
    j = jnp.arange(2 * KEY_TILE)[:, None] % KEY_TILE
    s = jnp.arange(2 * KEY_TILE)[None, :]
    return -((s >= KEY_TILE) | (j > s)).astype(BF16)


def _head_sum_matrix():
    a = jnp.arange(ATT_WIDTH)
    return (a[:, None] // HEAD_DIM == a[None, :] // HEAD_DIM).astype(BF16)


def kernel(x_prompt, x_sample, c_prompt, c_sample, cache_k, cache_v, state_ssm_re, state_ssm_im, norm_g, w_mod, b_mod, w_in, ssm_a_re, ssm_a_im, ssm_log_dt, ssm_b_re, ssm_b_im, ssm_c_re, ssm_c_im, ssm_d, w_glu, b_glu, q_norm_g, k_norm_g, w_out):
    depth = w_in.shape[0]
    nb, seq_len, _ = x_prompt.shape
    ns, t_new, _ = x_sample.shape
    past = cache_k.shape[2]
    tm = 512

    mod = _modulation(jnp.concatenate([c_prompt, c_sample], axis=0), w_mod, b_mod)
    pw_re, pw_im, bb_re, bb_im = _discretise(ssm_a_re, ssm_a_im, ssm_log_dt, ssm_b_re, ssm_b_im)
    bmat, cmat, mr, mi, pr, pi_ = _s5_matrices(pw_re, pw_im, bb_re, bb_im, ssm_c_re, ssm_c_im)

    w_in_b = w_in.astype(BF16)
    w_out_b = w_out.astype(BF16)
    w_glu_b = w_glu.astype(BF16)
    tri = _sb_sum_matrix()
    hsum = _head_sum_matrix()

    xp = x_prompt.reshape(nb * seq_len, D_MODEL)
    xs = x_sample.reshape(ns * t_new, D_MODEL)
    zero_state = jnp.zeros((nb, 1, STATE_LANES), F32)
    outs = {name: [] for name in ("pr", "pi", "sr", "si")}
    pkv = skv = None
    for l in range(depth):
        g2 = norm_g[l].reshape(1, D_MODEL)
        qg = jnp.tile(q_norm_g[l], N_HEADS).reshape(1, ATT_WIDTH)
        kg = jnp.tile(k_norm_g[l], N_HEADS).reshape(1, ATT_WIDTH)
        d2 = ssm_d[l].reshape(1, SSM_WIDTH)
        bg2 = b_glu[l].reshape(1, SSM_WIDTH)
        s5_w = (bmat[l], cmat[l], mr[l], mi[l], pr[l], pi_[l], d2, w_glu_b[l], bg2)

        mp = mod[l, :nb].reshape(nb, 1, 3 * D_MODEL)
        shift, scale, gate = (mp[:, :, i * D_MODEL:(i + 1) * D_MODEL] for i in range(3))
        u, zs, q, k_all, v_all, kb, vb, za = _in_projection(xp, shift, scale, g2, w_in_b[l], hsum, qg, kg,
                                                            tm, seq_len, l, depth, pkv)
        pkv = (k_all, v_all)
        ys, hfr, hfi = _s5_branch(u, zs, zero_state, zero_state, *s5_w,
                                  n_batch_steps=nb, n_seq=1, seq_rows=tm, steps_per_seq=seq_len // tm)
        ya = _sb_prompt(q, kb, vb, za, tri, nb, seq_len)
        xp = _out_projection(xp, ys, ya, gate, w_out_b[l], tm, seq_len)
        outs["pr"].append(hfr.reshape(nb, SSM_GROUPS, SSM_STATE))
        outs["pi"].append(hfi.reshape(nb, SSM_GROUPS, SSM_STATE))

        ms = jnp.repeat(mod[l, nb:], t_new, axis=0).reshape(1, ns * t_new, 3 * D_MODEL)
        shift, scale, gate = (ms[:, :, i * D_MODEL:(i + 1) * D_MODEL] for i in range(3))
        ts = ns * t_new
        u, zs, q, k_all, v_all, kb, vb, za = _in_projection(xs, shift, scale, g2, w_in_b[l], hsum, qg, kg,
                                                            ts, None, l, depth, skv)
        skv = (k_all, v_all)
        h0r = state_ssm_re[l].reshape(ns, 1, STATE_LANES)
        h0i = state_ssm_im[l].reshape(ns, 1, STATE_LANES)
        ys, hfr, hfi = _s5_branch(u, zs, h0r, h0i, *s5_w,
                                  n_batch_steps=1, n_seq=ns, seq_rows=t_new, steps_per_seq=1)
        pad = lambda a: jnp.pad(a.reshape(ns, t_new, ATT_WIDTH),
                                ((0, 0), (0, KEY_TILE - t_new), (0, 0))).reshape(ns * KEY_TILE, ATT_WIDTH)
        ya = _sb_sample(q, pad(kb), pad(vb), cache_k, cache_v, l, za, tri, ns, t_new)
        xs = _out_projection(xs, ys, ya, gate, w_out_b[l], ts, None)
        outs["sr"].append(hfr.reshape(ns, SSM_GROUPS, SSM_STATE))
        outs["si"].append(hfi.reshape(ns, SSM_GROUPS, SSM_STATE))

    st = lambda name: jnp.stack(outs[name])
    heads_p = lambda a: a.reshape(depth, nb, seq_len, N_HEADS, HEAD_DIM)
    heads_s = lambda a: a.reshape(depth, ns, t_new, N_HEADS, HEAD_DIM)
    return (xp.reshape(nb, seq_len, D_MODEL), xs.reshape(ns, t_new, D_MODEL),
            heads_p(pkv[0]), heads_p(pkv[1]), st("pr"), st("pi"),
            heads_s(skv[0]), heads_s(skv[1]), st("sr"), st("si"))
```

```python
import functools
import math
from typing import Any, NamedTuple

import jax
import jax.numpy as jnp
from jax import lax
from jax.experimental import pallas as pl
from jax.experimental.pallas import tpu as pltpu

F32 = jnp.float32
BF16 = jnp.bfloat16

D_MODEL = 1024
SSM_WIDTH = 512
SSM_GROUP = 16
SSM_GROUPS = 32
SSM_STATE = 64
STATE_LANES = SSM_GROUPS * SSM_STATE
ATT_WIDTH = 512
HEAD_DIM = 64
N_HEADS = 8
IN_WIDTH = 2 * SSM_WIDTH + 4 * ATT_WIDTH
EPS = 1e-6

LANES = 128
SUBLANES = 8
HEADS_PER_LANE_BLOCK = LANES // HEAD_DIM
GROUPS_PER_LANE_BLOCK = LANES // SSM_GROUP
N_LANE_BLOCKS = SSM_WIDTH // LANES
N_PAIRS = N_HEADS // HEADS_PER_LANE_BLOCK
KEY_TILE = 128
SB_Q_ROWS = 32
SB_STEP_BLOCKS = 8
SB_LOG_CUTOFF = -40.0
VMEM_LIMIT = 56 * 1024 * 1024


def _cparams(n_axes):
    return pltpu.CompilerParams(dimension_semantics=("arbitrary",) * n_axes,
                                vmem_limit_bytes=VMEM_LIMIT)


def _silu(x):
    return x * (1.0 / (1.0 + jnp.exp(-x)))


def _sigmoid(x):
    return 1.0 / (1.0 + jnp.exp(-x))


def _gelu_tanh(x):
    c = math.sqrt(2.0 / math.pi)
    return 0.5 * x * (1.0 + jnp.tanh(c * (x + 0.044715 * (x * x * x))))


def _mod_kernel(c_ref, w_ref, b_ref, o_ref):
    c = c_ref[...]
    a = _silu(c)
    o_ref[0] = jnp.dot(a, w_ref[0], preferred_element_type=F32,
                       precision=lax.Precision.HIGHEST) + b_ref[0]


def _modulation(c_all, w_mod, b_mod):
    depth = w_mod.shape[0]
    n = c_all.shape[0]
    nt = 3
    return pl.pallas_call(
        _mod_kernel,
        grid=(depth, nt),
        in_specs=[pl.BlockSpec((n, D_MODEL), lambda l, j: (0, 0)),
                  pl.BlockSpec((1, D_MODEL, D_MODEL), lambda l, j: (l, 0, j)),
                  pl.BlockSpec((1, 1, D_MODEL), lambda l, j: (l, 0, j))],
        out_specs=pl.BlockSpec((1, n, D_MODEL), lambda l, j: (l, 0, j)),
        out_shape=jax.ShapeDtypeStruct((depth, n, 3 * D_MODEL), F32),
        compiler_params=_cparams(2),
        name="modulation",
    )(c_all, w_mod, b_mod.reshape(depth, 1, 3 * D_MODEL))


def _disc_kernel(lr_ref, li_ref, ldt_ref, bre_ref, bim_ref, pwr_ref, pwi_ref, bbr_ref, bbi_ref):
    lr = lr_ref[...]
    li = li_ref[...]
    dt = jnp.exp(ldt_ref[...])
    mag = jnp.exp(lr * dt)
    ang = li * dt
    ab_re = mag * jnp.cos(ang)
    ab_im = mag * jnp.sin(ang)
    den = lr * lr + li * li
    nr = ab_re - 1.0
    f_re = (nr * lr + ab_im * li) / den
    f_im = (ab_im * lr - nr * li) / den
    bre = bre_ref[...]
    bim = bim_ref[...]
    bbr_ref[...] = f_re[None] * bre - f_im[None] * bim
    bbi_ref[...] = f_re[None] * bim + f_im[None] * bre
    pr, pi_ = ab_re, ab_im
    pwr_ref[0] = pr
    pwi_ref[0] = pi_
    for k in range(1, SUBLANES):
        pr, pi_ = pr * ab_re - pi_ * ab_im, pr * ab_im + pi_ * ab_re
        pwr_ref[k] = pr
        pwi_ref[k] = pi_


def _discretise(a_re, a_im, log_dt, b_re, b_im):
    depth = a_re.shape[0]
    rows = depth * SSM_GROUPS
    lr = a_re.reshape(rows, SSM_STATE)
    li = a_im.reshape(rows, SSM_STATE)
    ldt = jnp.broadcast_to(log_dt.reshape(rows, 1), (rows, SSM_STATE))
    bre = b_re.transpose(3, 0, 1, 2).reshape(SSM_GROUP, rows, SSM_STATE)
    bim = b_im.transpose(3, 0, 1, 2).reshape(SSM_GROUP, rows, SSM_STATE)
    outs = pl.pallas_call(
        _disc_kernel,
        out_shape=(jax.ShapeDtypeStruct((SUBLANES, rows, SSM_STATE), F32),
                   jax.ShapeDtypeStruct((SUBLANES, rows, SSM_STATE), F32),
                   jax.ShapeDtypeStruct((SSM_GROUP, rows, SSM_STATE), F32),
                   jax.ShapeDtypeStruct((SSM_GROUP, rows, SSM_STATE), F32)),
        name="s5_discretise",
    )(lr, li, ldt, bre, bim)
    pw_re, pw_im, bb_re, bb_im = outs
    pw_re = pw_re.reshape(SUBLANES, depth, STATE_LANES).transpose(1, 0, 2)
    pw_im = pw_im.reshape(SUBLANES, depth, STATE_LANES).transpose(1, 0, 2)
    bb_re = bb_re.reshape(SSM_GROUP, depth, SSM_GROUPS, SSM_STATE)
    bb_im = bb_im.reshape(SSM_GROUP, depth, SSM_GROUPS, SSM_STATE)
    return pw_re, pw_im, bb_re, bb_im


def _s5_matrices(pw_re, pw_im, bb_re, bb_im, c_re, c_im):
    depth = pw_re.shape[0]
    eye = jnp.eye(GROUPS_PER_LANE_BLOCK, dtype=F32)

    def b_block(bb):
        x = bb.reshape(SSM_GROUP, depth, N_LANE_BLOCKS, GROUPS_PER_LANE_BLOCK, SSM_STATE)
        x = jnp.einsum('clbgp,gh->lbgchp', x, eye)
        return x.reshape(depth, N_LANE_BLOCKS, LANES, GROUPS_PER_LANE_BLOCK * SSM_STATE)

    def c_block(c):
        x = c.reshape(depth, N_LANE_BLOCKS, GROUPS_PER_LANE_BLOCK, SSM_GROUP, SSM_STATE)
        x = jnp.einsum('lbgcp,gh->lbgphc', x, eye)
        return x.reshape(depth, N_LANE_BLOCKS, GROUPS_PER_LANE_BLOCK * SSM_STATE, LANES)

    bmat = jnp.concatenate([b_block(bb_re), b_block(bb_im)], axis=-1).astype(BF16)
    cmat = jnp.concatenate([c_block(c_re), -c_block(c_im)], axis=-2).astype(BF16)

    row = jnp.arange(SUBLANES)[None, :, None]

    def step_mult(pw):
        return jnp.stack([jnp.where(row >= k, pw[:, k - 1][:, None, :], 0.0) for k in (1, 2, 4)], axis=1)

    return bmat, cmat, step_mult(pw_re), step_mult(pw_im), pw_re, pw_im


def _heads_major(x, position_minor):
    if position_minor:
        return x.T.reshape(N_HEADS, HEAD_DIM, x.shape[0])
    heads = jnp.stack([x[:, h * HEAD_DIM:(h + 1) * HEAD_DIM] for h in range(N_HEADS)], axis=0)
    return pltpu.einshape("htd->thd", heads)


def _inproj_kernel(x_ref, shift_ref, scale_ref, g_ref, w_ref, hsum_ref, qg_ref, kg_ref, *rest,
                   position_minor):
    u_ref, zs_ref, q_ref, k_ref, v_ref, kb_ref, vb_ref, za_ref = rest[-8:]
    x = x_ref[...]
    ms = jnp.mean(x * x, axis=-1, keepdims=True)
    h = x * lax.rsqrt(ms + EPS) * g_ref[...]
    h = h * (1.0 + scale_ref[0]) + shift_ref[0]
    hb = h.astype(BF16)

    def proj(c):
        return jnp.dot(hb, w_ref[:, c * SSM_WIDTH:(c + 1) * SSM_WIDTH], preferred_element_type=F32)

    def head_norm(p, g):
        ss = jnp.dot((p * p).astype(BF16), hsum_ref[...], preferred_element_type=F32)
        return p * lax.rsqrt(ss * (1.0 / HEAD_DIM) + EPS) * g

    u_ref[...] = proj(0).astype(BF16)
    zs_ref[...] = _silu(proj(1)).astype(BF16)
    q = head_norm(proj(2), qg_ref[...])
    q_ref[...] = (q * (HEAD_DIM ** -0.5)).astype(BF16)
    k = head_norm(proj(3), kg_ref[...])
    k_ref[...] = _heads_major(k, position_minor).reshape(k_ref.shape)
    kb_ref[...] = k.astype(BF16)
    v = proj(4)
    v_ref[...] = _heads_major(v, position_minor).reshape(v_ref.shape)
    vb_ref[...] = v.astype(BF16)
    za_ref[...] = _silu(proj(5)).astype(BF16)


def _in_projection(x2d, shift, scale, norm_g, w_in_b, hsum, q_g, k_g, tm, rows_per_seq,
                   layer, depth, kv_all):
    n = x2d.shape[0]
    nt = n // tm
    if rows_per_seq is None:
        mod_spec = pl.BlockSpec((1, tm, D_MODEL), lambda i: (0, i, 0))
    else:
        tps = rows_per_seq // tm
        mod_spec = pl.BlockSpec((1, 1, D_MODEL), lambda i: (i // tps, 0, 0))
    row_spec = lambda w: pl.BlockSpec((tm, w), lambda i: (i, 0))
    full = lambda a: pl.BlockSpec(a.shape, lambda i: (0,) * a.ndim)
    bf = jax.ShapeDtypeStruct((n, SSM_WIDTH), BF16)
    position_minor = rows_per_seq is not None
    if position_minor:
        f3 = jax.ShapeDtypeStruct((depth, n // rows_per_seq, N_HEADS, HEAD_DIM, rows_per_seq), F32)
        kv_spec = pl.BlockSpec((1, 1, N_HEADS, HEAD_DIM, tm), lambda i: (layer, i // tps, 0, 0, i % tps))
    else:
        f3 = jax.ShapeDtypeStruct((depth, n, N_HEADS, HEAD_DIM), F32)
        kv_spec = pl.BlockSpec((1, tm, N_HEADS, HEAD_DIM), lambda i: (layer, i, 0, 0))
    in_specs = [row_spec(D_MODEL), mod_spec, mod_spec, full(norm_g), full(w_in_b), full(hsum),
                full(q_g), full(k_g)]
    args = [x2d, shift, scale, norm_g, w_in_b, hsum, q_g, k_g]
    aliases = {}
    if kv_all is not None:
        in_specs += [pl.BlockSpec(memory_space=pl.ANY)] * 2
        args += list(kv_all)
        aliases = {len(args) - 2: 3, len(args) - 1: 4}
    return pl.pallas_call(
        functools.partial(_inproj_kernel, position_minor=position_minor),
        grid=(nt,),
        in_specs=in_specs,
        out_specs=[row_spec(SSM_WIDTH)] * 3 + [kv_spec] * 2 + [row_spec(SSM_WIDTH)] * 3,
        out_shape=(bf, bf, bf, f3, f3, bf, bf, bf),
        input_output_aliases=aliases,
        compiler_params=_cparams(1),
        name="in_proj",
    )(*args)


def _s5_kernel(u_ref, zs_ref, h0r_ref, h0i_ref, bmat_ref, cmat_ref, mr_ref, mi_ref, pr_ref, pi_ref,
               d_ref, wglu_ref, bglu_ref,
               y_ref, hfr_ref, hfi_ref,
               vre, vim, hre, him, car_re, car_im, ybuf, *, n_seq, seq_rows, lane_chunk):
    i = pl.program_id(1)
    rows = n_seq * seq_rows

    @pl.when(i == 0)
    def _():
        for s in range(n_seq):
            car_re[s] = jnp.broadcast_to(h0r_ref[s], (SUBLANES, STATE_LANES))
            car_im[s] = jnp.broadcast_to(h0i_ref[s], (SUBLANES, STATE_LANES))

    half = GROUPS_PER_LANE_BLOCK * SSM_STATE
    for lb in range(N_LANE_BLOCKS):
        ub = u_ref[:, lb * LANES:(lb + 1) * LANES]
        bu = jnp.dot(ub, bmat_ref[lb], preferred_element_type=F32)
        vre[:, lb * half:(lb + 1) * half] = bu[:, :half]
        vim[:, lb * half:(lb + 1) * half] = bu[:, half:]

    groups = seq_rows // SUBLANES
    for ch in range(STATE_LANES // lane_chunk):
        sl = slice(ch * lane_chunk, (ch + 1) * lane_chunk)
        prc = pr_ref[:, sl]
        pic = pi_ref[:, sl]

        def seq_body(s, _):
            def group_body(r, carry):
                cr, ci = carry
                row = pl.multiple_of(s * seq_rows + r * SUBLANES, SUBLANES)
                vr = vre[pl.ds(row, SUBLANES), sl]
                vi = vim[pl.ds(row, SUBLANES), sl]
                for idx, k in enumerate((1, 2, 4)):
                    mr = mr_ref[idx, :, sl]
                    mi = mi_ref[idx, :, sl]
                    sr = pltpu.roll(vr, k, 0)
                    si = pltpu.roll(vi, k, 0)
                    vr, vi = vr + (mr * sr - mi * si), vi + (mr * si + mi * sr)
                hr = vr + (prc * cr - pic * ci)
                hi = vi + (prc * ci + pic * cr)
                hre[pl.ds(row, SUBLANES), sl] = hr
                him[pl.ds(row, SUBLANES), sl] = hi
                last = SUBLANES - 1
                return (jnp.broadcast_to(hr[last:last + 1], hr.shape),
                        jnp.broadcast_to(hi[last:last + 1], hi.shape))

            cr, ci = lax.fori_loop(0, groups, group_body, (car_re[s, :, sl], car_im[s, :, sl]),
                                   unroll=2)
            car_re[s, :, sl] = cr
            car_im[s, :, sl] = ci
            return 0

        lax.fori_loop(0, n_seq, seq_body, 0)

    for s in range(n_seq):
        hfr_ref[s] = car_re[s, 0:1, :]
        hfi_ref[s] = car_im[s, 0:1, :]

    for lb in range(N_LANE_BLOCKS):
        hcat = jnp.concatenate([hre[:, lb * half:(lb + 1) * half].astype(BF16),
                                him[:, lb * half:(lb + 1) * half].astype(BF16)], axis=1)
        ybuf[:, lb * LANES:(lb + 1) * LANES] = jnp.dot(hcat, cmat_ref[lb], preferred_element_type=F32)
    y = ybuf[...] + d_ref[...] * u_ref[...].astype(F32)
    g = _gelu_tanh(y)
    gate = _sigmoid(jnp.dot(g.astype(BF16), wglu_ref[...], preferred_element_type=F32) + bglu_ref[...])
    y_ref[...] = (g * gate * zs_ref[...].astype(F32)).astype(BF16)


def _s5_branch(u, zs, h0_re, h0_im, bmat, cmat, mr, mi, pr, pi_, d_skip, w_glu_b, b_glu,
               n_batch_steps, n_seq, seq_rows, steps_per_seq):
    n = u.shape[0]
    tile = n_seq * seq_rows
    nb_total = h0_re.shape[0]
    row_spec = pl.BlockSpec((tile, SSM_WIDTH), lambda b, i: (b * steps_per_seq + i, 0))
    st_spec = pl.BlockSpec((n_seq, 1, STATE_LANES), lambda b, i: (b, 0, 0))
    full = lambda a: pl.BlockSpec(a.shape, lambda b, i: (0,) * a.ndim)
    kern = functools.partial(_s5_kernel, n_seq=n_seq, seq_rows=seq_rows, lane_chunk=256)
    return pl.pallas_call(
        kern,
        grid=(n_batch_steps, steps_per_seq),
        in_specs=[row_spec, row_spec, st_spec, st_spec, full(bmat), full(cmat), full(mr), full(mi),
                  full(pr), full(pi_), full(d_skip), full(w_glu_b), full(b_glu)],
        out_specs=[row_spec, st_spec, st_spec],
        out_shape=(jax.ShapeDtypeStruct((n, SSM_WIDTH), BF16),
                   jax.ShapeDtypeStruct((nb_total, 1, STATE_LANES), F32),
                   jax.ShapeDtypeStruct((nb_total, 1, STATE_LANES), F32)),
        scratch_shapes=[pltpu.VMEM((tile, STATE_LANES), F32), pltpu.VMEM((tile, STATE_LANES), F32),
                        pltpu.VMEM((tile, STATE_LANES), F32), pltpu.VMEM((tile, STATE_LANES), F32),
                        pltpu.VMEM((n_seq, SUBLANES, STATE_LANES), F32),
                        pltpu.VMEM((n_seq, SUBLANES, STATE_LANES), F32),
                        pltpu.VMEM((tile, SSM_WIDTH), F32)],
        compiler_params=_cparams(2),
        name="s5_branch",
    )(u, zs, h0_re, h0_im, bmat, cmat, mr, mi, pr, pi_, d_skip, w_glu_b, b_glu)


def _sb_mask_queries(q_rows, qm_ref, tq):
    lane = lax.broadcasted_iota(jnp.int32, (tq, LANES), 1)
    for j in range(N_PAIRS):
        qpair = q_rows(j).astype(F32)
        for hh in range(HEADS_PER_LANE_BLOCK):
            h = j * HEADS_PER_LANE_BLOCK + hh
            in_head = (lane >= hh * HEAD_DIM) & (lane < (hh + 1) * HEAD_DIM)
            qm_ref[h * tq:(h + 1) * tq, :] = jnp.where(in_head, qpair, 0.0).astype(BF16)


class _SbTile(NamedTuple):
    tq: int
    k_tile: Any
    v_tile: Any
    keys_minor: bool
    valid: Any
    first: bool
    qm: Any
    lb: Any
    hl: Any
    a: Any
    car: Any
    acc: Any


def _sb_scores(t):
    pair = HEADS_PER_LANE_BLOCK * t.tq
    valid2 = None if t.valid is None else jnp.concatenate([t.valid] * HEADS_PER_LANE_BLOCK, axis=0)
    for j in range(N_PAIRS):
        rows = slice(j * pair, (j + 1) * pair)
        contract_k = 0 if t.keys_minor else 1
        z = lax.dot_general(t.qm[rows, :], t.k_tile(j), (((1,), (contract_k,)), ((), ())),
                            preferred_element_type=F32)
        sp = jnp.maximum(z, 0.0) + jnp.log(1.0 + jnp.exp(-jnp.abs(z)))
        t.lb[rows, :] = z - sp
        if valid2 is not None:
            sp = jnp.where(valid2, sp, 0.0)
        hi = sp.astype(BF16)
        t.hl[rows, :KEY_TILE] = hi
        t.hl[rows, KEY_TILE:] = (sp - hi.astype(F32)).astype(BF16)


def _sb_weights(t, tri_ref):
    cs = jnp.dot(t.hl[...], tri_ref[...], preferred_element_type=F32)
    cmax = None
    for h in range(N_HEADS):
        rows = slice(h * t.tq, (h + 1) * t.tq)
        after = cs[rows, :KEY_TILE]
        total = cs[rows, KEY_TILE:]
        if not t.first:
            carry = t.car[rows, :]
            after = after + carry
            total = total + carry
        a = jnp.exp(t.lb[rows, :] + after)
        if t.valid is not None:
            a = jnp.where(t.valid, a, 0.0)
        t.a[rows, :] = a.astype(BF16)
        t.car[rows, :] = total
        cmax = total if cmax is None else jnp.maximum(cmax, total)
    return jnp.max(cmax)


def _sb_output(t):
    pair = HEADS_PER_LANE_BLOCK * t.tq
    lane = lax.broadcasted_iota(jnp.int32, (t.tq, LANES), 1)
    for j in range(N_PAIRS):
        contract_v = 1 if t.keys_minor else 0
        pv = lax.dot_general(t.a[j * pair:(j + 1) * pair, :], t.v_tile(j), (((1,), (contract_v,)), ((), ())),
                             preferred_element_type=F32)
        out = jnp.where(lane < HEAD_DIM, pv[:t.tq], pv[t.tq:])
        if t.first:
            t.acc[:, j * LANES:(j + 1) * LANES] = out
        else:
            t.acc[:, j * LANES:(j + 1) * LANES] += out


def _sb_tiles(tiles, tri_ref):
    for t in tiles:
        _sb_scores(t)
    ms = [_sb_weights(t, tri_ref) for t in tiles]
    for t in tiles:
        _sb_output(t)
    return ms


def _sb_scratch(n_blocks, tq):
    rows = N_HEADS * tq
    return [pltpu.VMEM((n_blocks, rows, LANES), BF16),
            pltpu.VMEM((n_blocks, rows, LANES), F32),
            pltpu.VMEM((n_blocks, rows, 2 * KEY_TILE), BF16),
            pltpu.VMEM((n_blocks, rows, LANES), BF16),
            pltpu.VMEM((n_blocks, rows, LANES), F32),
            pltpu.VMEM((n_blocks, tq, ATT_WIDTH), F32),
            pltpu.SMEM((n_blocks,), F32)]


def _sb_prompt_kernel(q_ref, k_ref, v_ref, za_ref, tri_ref, o_ref,
                      qm_ref, lb_ref, hl_ref, a_ref, car_ref, acc_ref, m_ref):
    tq = SB_Q_ROWS
    n_blocks = q_ref.shape[0] // tq
    step_row0 = pl.program_id(1) * (n_blocks * tq)
    row = lax.broadcasted_iota(jnp.int32, (tq, KEY_TILE), 0)
    col = lax.broadcasted_iota(jnp.int32, (tq, KEY_TILE), 1)

    def tiles(ref, start):
        return lambda j: ref[pl.ds(start, KEY_TILE), j * LANES:(j + 1) * LANES]

    def band_start(s):
        return pl.multiple_of(jnp.maximum(step_row0 + (s + 1) * tq - KEY_TILE, 0), tq)

    def tile(s, lo, valid, first):
        return _SbTile(tq, tiles(k_ref, lo), tiles(v_ref, lo), False, valid, first, qm_ref.at[s], lb_ref.at[s],
                       hl_ref.at[s], a_ref.at[s], car_ref.at[s], acc_ref.at[s])

    band = []
    for s in range(n_blocks):
        lo = band_start(s)
        _sb_mask_queries(lambda j: q_ref[s * tq:(s + 1) * tq, j * LANES:(j + 1) * LANES], qm_ref.at[s], tq)
        valid = (col - row) < (step_row0 + s * tq - lo)
        band.append(tile(s, lo, valid, True))
    for s, m in enumerate(_sb_tiles(band, tri_ref)):
        m_ref[s] = m

    def more_tiles(s, _):
        def cond(state):
            hi, m = state
            return (hi > 0) & (m > SB_LOG_CUTOFF)

        def body(state):
            hi, _ = state
            lo = pl.multiple_of(jnp.maximum(hi - KEY_TILE, 0), tq)
            valid = col < (hi - lo)
            (m,) = _sb_tiles([tile(s, lo, valid, False)], tri_ref)
            return lo, m

        lax.while_loop(cond, body, (band_start(s), m_ref[s]))
        return 0

    lax.fori_loop(0, n_blocks, more_tiles, 0)
    for s in range(n_blocks):
        rows = slice(s * tq, (s + 1) * tq)
        o_ref[rows, :] = (acc_ref[s] * za_ref[rows, :].astype(F32)).astype(BF16)


def _sb_prompt(q, kb, vb, za, tri, n_batch, seq_len):
    step_rows = SB_STEP_BLOCKS * SB_Q_ROWS
    nq = seq_len // step_rows
    row_spec = pl.BlockSpec((step_rows, ATT_WIDTH), lambda b, i: (b * nq + i, 0))
    seq_spec = pl.BlockSpec((seq_len, ATT_WIDTH), lambda b, i: (b, 0))
    return pl.pallas_call(
        _sb_prompt_kernel,
        grid=(n_batch, nq),
        in_specs=[row_spec, seq_spec, seq_spec, row_spec, pl.BlockSpec(tri.shape, lambda b, i: (0, 0))],
        out_specs=row_spec,
        out_shape=jax.ShapeDtypeStruct(q.shape, BF16),
        scratch_shapes=_sb_scratch(SB_STEP_BLOCKS, SB_Q_ROWS),
        compiler_params=_cparams(2),
        name="sb_prompt",
    )(q, kb, vb, za, tri)


def _sb_sample_kernel(q_ref, kn_ref, vn_ref, kc_ref, vc_ref, za_ref, tri_ref, o_ref,
                      qm_ref, lb_ref, hl_ref, a_ref, car_ref, acc_ref, m_ref):
    tq = q_ref.shape[0]
    n_past = kc_ref.shape[-1] // KEY_TILE
    _sb_mask_queries(lambda j: q_ref[:, j * LANES:(j + 1) * LANES], qm_ref.at[0], tq)
    row = lax.broadcasted_iota(jnp.int32, (tq, KEY_TILE), 0)
    col = lax.broadcasted_iota(jnp.int32, (tq, KEY_TILE), 1)
    scratch = (qm_ref.at[0], lb_ref.at[0], hl_ref.at[0], a_ref.at[0], car_ref.at[0], acc_ref.at[0])

    new_tile = lambda ref: (lambda j: ref[:, j * LANES:(j + 1) * LANES])
    (m0,) = _sb_tiles([_SbTile(tq, new_tile(kn_ref), new_tile(vn_ref), False, col < row, True, *scratch)],
                      tri_ref)

    def past_tile(ref, t):
        start = pl.multiple_of(t * KEY_TILE, KEY_TILE)
        return lambda j: ref[0, 0, j * HEADS_PER_LANE_BLOCK:(j + 1) * HEADS_PER_LANE_BLOCK, :,
                             pl.ds(start, KEY_TILE)].reshape(LANES, KEY_TILE).astype(BF16)

    def cond(state):
        t, m = state
        return (t >= 0) & (m > SB_LOG_CUTOFF)

    def body(state):
        t, _ = state
        (m,) = _sb_tiles([_SbTile(tq, past_tile(kc_ref, t), past_tile(vc_ref, t), True, None, False,
                                  *scratch)], tri_ref)
        return t - 1, m

    lax.while_loop(cond, body, (n_past - 1, m0))
    o_ref[...] = (acc_ref[0] * za_ref[...].astype(F32)).astype(BF16)


def _sb_sample(q, k_new, v_new, k_past, v_past, layer, za, tri, n_batch, t_new):
    past = k_past.shape[-1]
    row_spec = pl.BlockSpec((t_new, ATT_WIDTH), lambda b: (b, 0))
    new_spec = pl.BlockSpec((KEY_TILE, ATT_WIDTH), lambda b: (b, 0))
    past_spec = pl.BlockSpec((1, 1, N_HEADS, HEAD_DIM, past), lambda b: (layer, b, 0, 0, 0))
    return pl.pallas_call(
        _sb_sample_kernel,
        grid=(n_batch,),
        in_specs=[row_spec, new_spec, new_spec, past_spec, past_spec, row_spec,
                  pl.BlockSpec(tri.shape, lambda b: (0, 0))],
        out_specs=row_spec,
        out_shape=jax.ShapeDtypeStruct(q.shape, BF16),
        scratch_shapes=_sb_scratch(1, t_new),
        compiler_params=_cparams(1),
        name="sb_sample",
    )(q, k_new, v_new, k_past, v_past, za, tri)


def _outproj_kernel(x_ref, ys_ref, ya_ref, gate_ref, w_ref, o_ref):
    mix = jnp.dot(ys_ref[...], w_ref[:SSM_WIDTH, :], preferred_element_type=F32)
    mix = mix + jnp.dot(ya_ref[...], w_ref[SSM_WIDTH:, :], preferred_element_type=F32)
    o_ref[...] = x_ref[...] + gate_ref[0] * mix


def _out_projection(x2d, ys, ya, gate, w_out_b, tm, rows_per_seq):
    n = x2d.shape[0]
    if rows_per_seq is None:
        mod_spec = pl.BlockSpec((1, tm, D_MODEL), lambda i: (0, i, 0))
    else:
        tps = rows_per_seq // tm
        mod_spec = pl.BlockSpec((1, 1, D_MODEL), lambda i: (i // tps, 0, 0))
    row_spec = lambda w: pl.BlockSpec((tm, w), lambda i: (i, 0))
    return pl.pallas_call(
        _outproj_kernel,
        grid=(n // tm,),
        in_specs=[row_spec(D_MODEL), row_spec(SSM_WIDTH), row_spec(ATT_WIDTH), mod_spec,
                  pl.BlockSpec(w_out_b.shape, lambda i: (0, 0))],
        out_specs=row_spec(D_MODEL),
        out_shape=jax.ShapeDtypeStruct(x2d.shape, F32),
        compiler_params=_cparams(1),
        name="out_proj",
    )(x2d, ys, ya, gate, w_out_b)


def _sb_sum_matrix():
    j = jnp.arange(2 * KEY_TILE)[:, None] % KEY_TILE
    s = jnp.arange(2 * KEY_TILE)[None, :]
    return -((s >= KEY_TILE) | (j > s)).astype(BF16)


def _head_sum_matrix():
    a = jnp.arange(ATT_WIDTH)
    return (a[:, None] // HEAD_DIM == a[None, :] // HEAD_DIM).astype(BF16)


def kernel(x_prompt, x_sample, c_prompt, c_sample, cache_k, cache_v, state_ssm_re, state_ssm_im, norm_g, w_mod, b_mod, w_in, ssm_a_re, ssm_a_im, ssm_log_dt, ssm_b_re, ssm_b_im, ssm_c_re, ssm_c_im, ssm_d, w_glu, b_glu, q_norm_g, k_norm_g, w_out):
    depth = w_in.shape[0]
    nb, seq_len, _ = x_prompt.shape
    ns, t_new, _ = x_sample.shape
    past = cache_k.shape[2]
    tm = 512

    mod = _modulation(jnp.concatenate([c_prompt, c_sample], axis=0), w_mod, b_mod)
    pw_re, pw_im, bb_re, bb_im = _discretise(ssm_a_re, ssm_a_im, ssm_log_dt, ssm_b_re, ssm_b_im)
    bmat, cmat, mr, mi, pr, pi_ = _s5_matrices(pw_re, pw_im, bb_re, bb_im, ssm_c_re, ssm_c_im)

    w_in_b = w_in.astype(BF16)
    w_out_b = w_out.astype(BF16)
    w_glu_b = w_glu.astype(BF16)
    tri = _sb_sum_matrix()
    hsum = _head_sum_matrix()

    xp = x_prompt.reshape(nb * seq_len, D_MODEL)
    xs = x_sample.reshape(ns * t_new, D_MODEL)
    zero_state = jnp.zeros((nb, 1, STATE_LANES), F32)
    kc = jnp.transpose(cache_k, (0, 1, 3, 4, 2))
    vc = jnp.transpose(cache_v, (0, 1, 3, 4, 2))
    outs = {name: [] for name in ("pr", "pi", "sr", "si")}
    pkv = skv = None
    for l in range(depth):
        g2 = norm_g[l].reshape(1, D_MODEL)
        qg = jnp.tile(q_norm_g[l], N_HEADS).reshape(1, ATT_WIDTH)
        kg = jnp.tile(k_norm_g[l], N_HEADS).reshape(1, ATT_WIDTH)
        d2 = ssm_d[l].reshape(1, SSM_WIDTH)
        bg2 = b_glu[l].reshape(1, SSM_WIDTH)
        s5_w = (bmat[l], cmat[l], mr[l], mi[l], pr[l], pi_[l], d2, w_glu_b[l], bg2)

        mp = mod[l, :nb].reshape(nb, 1, 3 * D_MODEL)
        shift, scale, gate = (mp[:, :, i * D_MODEL:(i + 1) * D_MODEL] for i in range(3))
        u, zs, q, k_all, v_all, kb, vb, za = _in_projection(xp, shift, scale, g2, w_in_b[l], hsum, qg, kg,
                                                            tm, seq_len, l, depth, pkv)
        pkv = (k_all, v_all)
        ys, hfr, hfi = _s5_branch(u, zs, zero_state, zero_state, *s5_w,
                                  n_batch_steps=nb, n_seq=1, seq_rows=tm, steps_per_seq=seq_len // tm)
        ya = _sb_prompt(q, kb, vb, za, tri, nb, seq_len)
        xp = _out_projection(xp, ys, ya, gate, w_out_b[l], tm, seq_len)
        outs["pr"].append(hfr.reshape(nb, SSM_GROUPS, SSM_STATE))
        outs["pi"].append(hfi.reshape(nb, SSM_GROUPS, SSM_STATE))

        ms = jnp.repeat(mod[l, nb:], t_new, axis=0).reshape(1, ns * t_new, 3 * D_MODEL)
        shift, scale, gate = (ms[:, :, i * D_MODEL:(i + 1) * D_MODEL] for i in range(3))
        ts = ns * t_new
        u, zs, q, k_all, v_all, kb, vb, za = _in_projection(xs, shift, scale, g2, w_in_b[l], hsum, qg, kg,
                                                            ts, None, l, depth, skv)
        skv = (k_all, v_all)
        h0r = state_ssm_re[l].reshape(ns, 1, STATE_LANES)
        h0i = state_ssm_im[l].reshape(ns, 1, STATE_LANES)
        ys, hfr, hfi = _s5_branch(u, zs, h0r, h0i, *s5_w,
                                  n_batch_steps=1, n_seq=ns, seq_rows=t_new, steps_per_seq=1)
        pad = lambda a: jnp.pad(a.reshape(ns, t_new, ATT_WIDTH),
                                ((0, 0), (0, KEY_TILE - t_new), (0, 0))).reshape(ns * KEY_TILE, ATT_WIDTH)
        ya = _sb_sample(q, pad(kb), pad(vb), kc, vc, l, za, tri, ns, t_new)
        xs = _out_projection(xs, ys, ya, gate, w_out_b[l], ts, None)
        outs["sr"].append(hfr.reshape(ns, SSM_GROUPS, SSM_STATE))
        outs["si"].append(hfi.reshape(ns, SSM_GROUPS, SSM_STATE))

    st = lambda name: jnp.stack(outs[name])
    heads_p = lambda a: jnp.transpose(a, (0, 1, 4, 2, 3))
    heads_s = lambda a: a.reshape(depth, ns, t_new, N_HEADS, HEAD_DIM)
    return (xp.reshape(nb, seq_len, D_MODEL), xs.reshape(ns, t_new, D_MODEL),
            heads_p(pkv[0]), heads_p(pkv[1]), st("pr"), st("pi"),
            heads_s(skv[0]), heads_s(skv[1]), st("sr"), st("si"))
```

```python
import functools
import math
from typing import Any, NamedTuple

import jax
import jax.numpy as jnp
from jax import lax
from jax.experimental import pallas as pl
from jax.experimental.pallas import tpu as pltpu

F32 = jnp.float32
BF16 = jnp.bfloat16

D_MODEL = 1024
SSM_WIDTH = 512
SSM_GROUP = 16
SSM_GROUPS = 32
SSM_STATE = 64
STATE_LANES = SSM_GROUPS * SSM_STATE
ATT_WIDTH = 512
HEAD_DIM = 64
N_HEADS = 8
IN_WIDTH = 2 * SSM_WIDTH + 4 * ATT_WIDTH
EPS = 1e-6

LANES = 128
SUBLANES = 8
HEADS_PER_LANE_BLOCK = LANES // HEAD_DIM
GROUPS_PER_LANE_BLOCK = LANES // SSM_GROUP
N_LANE_BLOCKS = SSM_WIDTH // LANES
N_PAIRS = N_HEADS // HEADS_PER_LANE_BLOCK
S5_CHUNK = SUBLANES
KEY_TILE = 128
SB_Q_ROWS = 32
SB_STEP_BLOCKS = 8
SB_LOG_CUTOFF = -40.0
VMEM_LIMIT = 56 * 1024 * 1024


def _cparams(n_axes):
    return pltpu.CompilerParams(dimension_semantics=("arbitrary",) * n_axes,
                                vmem_limit_bytes=VMEM_LIMIT)


def _silu(x):
    return x * (1.0 / (1.0 + jnp.exp(-x)))


def _sigmoid(x):
    return 1.0 / (1.0 + jnp.exp(-x))


def _gelu_tanh(x):
    c = math.sqrt(2.0 / math.pi)
    return 0.5 * x * (1.0 + jnp.tanh(c * (x + 0.044715 * (x * x * x))))


def _mod_kernel(c_ref, w_ref, b_ref, o_ref):
    c = c_ref[...]
    a = _silu(c)
    o_ref[0] = jnp.dot(a, w_ref[0], preferred_element_type=F32,
                       precision=lax.Precision.HIGHEST) + b_ref[0]


def _modulation(c_all, w_mod, b_mod):
    depth = w_mod.shape[0]
    n = c_all.shape[0]
    nt = 3
    return pl.pallas_call(
        _mod_kernel,
        grid=(depth, nt),
        in_specs=[pl.BlockSpec((n, D_MODEL), lambda l, j: (0, 0)),
                  pl.BlockSpec((1, D_MODEL, D_MODEL), lambda l, j: (l, 0, j)),
                  pl.BlockSpec((1, 1, D_MODEL), lambda l, j: (l, 0, j))],
        out_specs=pl.BlockSpec((1, n, D_MODEL), lambda l, j: (l, 0, j)),
        out_shape=jax.ShapeDtypeStruct((depth, n, 3 * D_MODEL), F32),
        compiler_params=_cparams(2),
        name="modulation",
    )(c_all, w_mod, b_mod.reshape(depth, 1, 3 * D_MODEL))


def _disc_kernel(lr_ref, li_ref, ldt_ref, bre_ref, bim_ref, cre_ref, cim_ref,
                 pw8r_ref, pw8i_ref, lbr_ref, lbi_ref, clr_ref, cli_ref):
    lr = lr_ref[...]
    li = li_ref[...]
    dt = jnp.exp(ldt_ref[...])
    mag = jnp.exp(lr * dt)
    ang = li * dt
    ab_re = mag * jnp.cos(ang)
    ab_im = mag * jnp.sin(ang)
    den = lr * lr + li * li
    nr = ab_re - 1.0
    f_re = (nr * lr + ab_im * li) / den
    f_im = (ab_im * lr - nr * li) / den
    bre = bre_ref[...]
    bim = bim_ref[...]
    cre = cre_ref[...]
    cim = cim_ref[...]
    xr = f_re[:, None, :] * bre - f_im[:, None, :] * bim
    xi = f_re[:, None, :] * bim + f_im[:, None, :] * bre
    pr, pi_ = ab_re, ab_im
    for t in range(S5_CHUNK):
        lbr_ref[t] = xr
        lbi_ref[t] = xi
        clr_ref[t] = cre * pr[:, None, :] - cim * pi_[:, None, :]
        cli_ref[t] = cre * pi_[:, None, :] + cim * pr[:, None, :]
        if t + 1 < S5_CHUNK:
            xr, xi = (xr * ab_re[:, None, :] - xi * ab_im[:, None, :],
                      xr * ab_im[:, None, :] + xi * ab_re[:, None, :])
            pr, pi_ = pr * ab_re - pi_ * ab_im, pr * ab_im + pi_ * ab_re
    a8r, a8i = pr, pi_
    qr, qi = a8r, a8i
    for m in range(SUBLANES):
        pw8r_ref[m] = qr
        pw8i_ref[m] = qi
        qr, qi = qr * a8r - qi * a8i, qr * a8i + qi * a8r


def _lag_kernel(lbr_ref, lbi_ref, cre_ref, cim_ref, k_ref):
    for g in range(lbr_ref.shape[1]):
        c_cat = jnp.concatenate([cre_ref[g], -cim_ref[g]], axis=1)
        for j in range(S5_CHUNK):
            b_cat = jnp.concatenate([lbr_ref[j, g], lbi_ref[j, g]], axis=1)
            k_ref[g, j] = lax.dot_general(b_cat, c_cat, (((1,), (1,)), ((), ())),
                                          preferred_element_type=F32,
                                          precision=lax.Precision.HIGHEST)


def _s5_operators(a_re, a_im, log_dt, b_re, b_im, c_re, c_im):
    depth = a_re.shape[0]
    rows = depth * SSM_GROUPS
    lr = a_re.reshape(rows, SSM_STATE)
    li = a_im.reshape(rows, SSM_STATE)
    ldt = jnp.broadcast_to(log_dt.reshape(rows, 1), (rows, SSM_STATE))
    bre = b_re.transpose(0, 1, 3, 2).reshape(rows, SSM_GROUP, SSM_STATE)
    bim = b_im.transpose(0, 1, 3, 2).reshape(rows, SSM_GROUP, SSM_STATE)
    cre = c_re.reshape(rows, SSM_GROUP, SSM_STATE)
    cim = c_im.reshape(rows, SSM_GROUP, SSM_STATE)
    pw_shape = jax.ShapeDtypeStruct((SUBLANES, rows, SSM_STATE), F32)
    op_shape = jax.ShapeDtypeStruct((S5_CHUNK, rows, SSM_GROUP, SSM_STATE), F32)
    pw8r, pw8i, lbr, lbi, clr, cli = pl.pallas_call(
        _disc_kernel,
        out_shape=(pw_shape, pw_shape, op_shape, op_shape, op_shape, op_shape),
        compiler_params=_cparams(0),
        name="s5_discretise",
    )(lr, li, ldt, bre, bim, cre, cim)
    gb = GROUPS_PER_LANE_BLOCK
    op_spec = pl.BlockSpec((S5_CHUNK, gb, SSM_GROUP, SSM_STATE), lambda i: (0, i, 0, 0))
    c_spec = pl.BlockSpec((gb, SSM_GROUP, SSM_STATE), lambda i: (i, 0, 0))
    lag = pl.pallas_call(
        _lag_kernel,
        grid=(rows // gb,),
        in_specs=[op_spec, op_spec, c_spec, c_spec],
        out_specs=pl.BlockSpec((gb, S5_CHUNK, SSM_GROUP, SSM_GROUP), lambda i: (i, 0, 0, 0)),
        out_shape=jax.ShapeDtypeStruct((rows, S5_CHUNK, SSM_GROUP, SSM_GROUP), F32),
        compiler_params=_cparams(1),
        name="s5_lag_kernels",
    )(lbr, lbi, cre, cim)

    nlb, T = N_LANE_BLOCKS, S5_CHUNK
    same = (jnp.arange(gb)[:, None] == jnp.arange(gb)[None, :])

    def chunk_in(x):
        x = x[::-1].reshape(T, depth, nlb, gb, SSM_GROUP, SSM_STATE).transpose(1, 2, 0, 3, 4, 5)
        x = jnp.where(same[None, None, None, :, None, :, None], x[:, :, :, :, :, None, :], 0.0)
        return x.reshape(depth, nlb, T * LANES, gb * SSM_STATE)

    def state_out(x):
        x = x.reshape(T, depth, nlb, gb, SSM_GROUP, SSM_STATE).transpose(1, 2, 3, 5, 0, 4)
        x = jnp.where(same[None, None, :, None, None, :, None], x[:, :, :, :, :, None, :], 0.0)
        return x.reshape(depth, nlb, gb * SSM_STATE, T * LANES)

    bmat = jnp.concatenate([chunk_in(lbr), chunk_in(lbi)], axis=-1).astype(BF16)
    cmat = jnp.concatenate([state_out(clr), -state_out(cli)], axis=-2)
    s_idx = jnp.arange(T)[:, None]
    t_idx = jnp.arange(T)[None, :]
    lagk = lag.reshape(depth, nlb, gb, T, SSM_GROUP, SSM_GROUP)
    kst = jnp.take(lagk, jnp.clip(t_idx - s_idx, 0, T - 1).reshape(-1), axis=3)
    kst = kst.reshape(depth, nlb, gb, T, T, SSM_GROUP, SSM_GROUP)
    kst = jnp.where((t_idx >= s_idx)[None, None, None, :, :, None, None], kst, 0.0)
    kst = kst.transpose(0, 1, 3, 2, 5, 4, 6)
    kst = jnp.where(same[None, None, None, :, None, None, :, None], kst[:, :, :, :, :, :, None, :], 0.0)
    ktoep = kst.reshape(depth, nlb, T * LANES, T * LANES)
    kc = jnp.concatenate([ktoep, cmat], axis=-2).astype(BF16)

    pw8r = pw8r.reshape(SUBLANES, depth, STATE_LANES).transpose(1, 0, 2)
    pw8i = pw8i.reshape(SUBLANES, depth, STATE_LANES).transpose(1, 0, 2)
    row = jnp.arange(SUBLANES)[None, :, None]

    def step_mult(pw):
        return jnp.stack([jnp.where(row >= k, pw[:, k - 1][:, None, :], 0.0) for k in (1, 2, 4)], axis=1)

    return bmat, kc, step_mult(pw8r), step_mult(pw8i), pw8r, pw8i


def _heads_major(x, position_minor):
    if position_minor:
        return x.T.reshape(N_HEADS, HEAD_DIM, x.shape[0])
    heads = jnp.stack([x[:, h * HEAD_DIM:(h + 1) * HEAD_DIM] for h in range(N_HEADS)], axis=0)
    return pltpu.einshape("htd->thd", heads)


def _inproj_kernel(x_ref, shift_ref, scale_ref, g_ref, w_ref, hsum_ref, qg_ref, kg_ref, *rest,
                   position_minor):
    u_ref, zs_ref, q_ref, k_ref, v_ref, kb_ref, vb_ref, za_ref = rest[-8:]
    x = x_ref[...]
    ms = jnp.mean(x * x, axis=-1, keepdims=True)
    h = x * lax.rsqrt(ms + EPS) * g_ref[...]
    h = h * (1.0 + scale_ref[0]) + shift_ref[0]
    hb = h.astype(BF16)

    def proj(c):
        return jnp.dot(hb, w_ref[:, c * SSM_WIDTH:(c + 1) * SSM_WIDTH], preferred_element_type=F32)

    def head_norm(p, g):
        ss = jnp.dot((p * p).astype(BF16), hsum_ref[...], preferred_element_type=F32)
        return p * lax.rsqrt(ss * (1.0 / HEAD_DIM) + EPS) * g

    u_ref[...] = proj(0)
    zs_ref[...] = _silu(proj(1)).astype(BF16)
    q = head_norm(proj(2), qg_ref[...])
    q_ref[...] = (q * (HEAD_DIM ** -0.5)).astype(BF16)
    k = head_norm(proj(3), kg_ref[...])
    k_ref[...] = _heads_major(k, position_minor).reshape(k_ref.shape)
    kb_ref[...] = k.astype(BF16)
    v = proj(4)
    v_ref[...] = _heads_major(v, position_minor).reshape(v_ref.shape)
    vb_ref[...] = v.astype(BF16)
    za_ref[...] = _silu(proj(5)).astype(BF16)


def _in_projection(x2d, shift, scale, norm_g, w_in_b, hsum, q_g, k_g, tm, rows_per_seq,
                   layer, depth, kv_all):
    n = x2d.shape[0]
    nt = n // tm
    if rows_per_seq is None:
        mod_spec = pl.BlockSpec((1, tm, D_MODEL), lambda i: (0, i, 0))
    else:
        tps = rows_per_seq // tm
        mod_spec = pl.BlockSpec((1, 1, D_MODEL), lambda i: (i // tps, 0, 0))
    row_spec = lambda w: pl.BlockSpec((tm, w), lambda i: (i, 0))
    full = lambda a: pl.BlockSpec(a.shape, lambda i: (0,) * a.ndim)
    bf = jax.ShapeDtypeStruct((n, SSM_WIDTH), BF16)
    position_minor = rows_per_seq is not None
    if position_minor:
        f3 = jax.ShapeDtypeStruct((depth, n // rows_per_seq, N_HEADS, HEAD_DIM, rows_per_seq), F32)
        kv_spec = pl.BlockSpec((1, 1, N_HEADS, HEAD_DIM, tm), lambda i: (layer, i // tps, 0, 0, i % tps))
    else:
        f3 = jax.ShapeDtypeStruct((depth, n, N_HEADS, HEAD_DIM), F32)
        kv_spec = pl.BlockSpec((1, tm, N_HEADS, HEAD_DIM), lambda i: (layer, i, 0, 0))
    assert kv_all[0].shape == f3.shape and kv_all[1].shape == f3.shape
    in_specs = [row_spec(D_MODEL), mod_spec, mod_spec, full(norm_g), full(w_in_b), full(hsum),
                full(q_g), full(k_g)] + [pl.BlockSpec(memory_space=pl.ANY)] * 2
    args = [x2d, shift, scale, norm_g, w_in_b, hsum, q_g, k_g, *kv_all]
    aliases = {len(args) - 2: 3, len(args) - 1: 4}
    return pl.pallas_call(
        functools.partial(_inproj_kernel, position_minor=position_minor),
        grid=(nt,),
        in_specs=in_specs,
        out_specs=[row_spec(SSM_WIDTH)] * 3 + [kv_spec] * 2 + [row_spec(SSM_WIDTH)] * 3,
        out_shape=(jax.ShapeDtypeStruct((n, SSM_WIDTH), F32), bf, bf, f3, f3, bf, bf, bf),
        input_output_aliases=aliases,
        compiler_params=_cparams(1),
        name="in_proj",
    )(*args)


def _s5_kernel(u_ref, h0r_ref, h0i_ref, bm_ref, kc_ref, mr_ref, mi_ref, pr_ref, pi_ref,
               y_ref, hfr_ref, hfi_ref,
               uflat, wre, wim, hpr, hpi, *, n_seq, lane_chunk):
    rows = u_ref.shape[0]
    n_chunks = rows // S5_CHUNK
    seq_chunks = n_chunks // n_seq
    half = GROUPS_PER_LANE_BLOCK * SSM_STATE

    def token_rows(c, t):
        if n_seq == 1:
            return pl.ds(t, n_chunks, stride=S5_CHUNK)
        return pl.ds(c * S5_CHUNK + t, n_seq, stride=seq_chunks * S5_CHUNK)

    fold_blocks = [(0, slice(None))] if n_seq == 1 else [(c, slice(c * n_seq, (c + 1) * n_seq))
                                                        for c in range(seq_chunks)]
    for c, frows in fold_blocks:
        for t in range(S5_CHUNK):
            uflat[frows, t * LANES:(t + 1) * LANES] = u_ref[token_rows(c, t), :].astype(BF16)
    w = jnp.dot(uflat[...], bm_ref[...], preferred_element_type=F32)
    wre[...] = w[:, :half]
    wim[...] = w[:, half:]

    if n_seq == 1:
        hpr[0:SUBLANES, :] = jnp.broadcast_to(h0r_ref[0], (SUBLANES, half))
        hpi[0:SUBLANES, :] = jnp.broadcast_to(h0i_ref[0], (SUBLANES, half))
        for ch in range(half // lane_chunk):
            sl = slice(ch * lane_chunk, (ch + 1) * lane_chunk)
            prc = pr_ref[:, sl]
            pic = pi_ref[:, sl]

            def group_body(r, carry):
                cr, ci = carry
                row = pl.multiple_of(r * SUBLANES, SUBLANES)
                vr = wre[pl.ds(row, SUBLANES), sl]
                vi = wim[pl.ds(row, SUBLANES), sl]
                for idx, k in enumerate((1, 2, 4)):
                    mr = mr_ref[idx, :, sl]
                    mi = mi_ref[idx, :, sl]
                    sr = pltpu.roll(vr, k, 0)
                    si = pltpu.roll(vi, k, 0)
                    vr, vi = vr + (mr * sr - mi * si), vi + (mr * si + mi * sr)
                hr = vr + (prc * cr - pic * ci)
                hi = vi + (prc * ci + pic * cr)
                hpr[pl.ds(row + SUBLANES, SUBLANES), sl] = hr
                hpi[pl.ds(row + SUBLANES, SUBLANES), sl] = hi
                last = SUBLANES - 1
                return (jnp.broadcast_to(hr[last:last + 1], hr.shape),
                        jnp.broadcast_to(hi[last:last + 1], hi.shape))

            cr, ci = lax.fori_loop(0, n_chunks // SUBLANES, group_body,
                                   (hpr[0:SUBLANES, sl], hpi[0:SUBLANES, sl]), unroll=2)
            hfr_ref[0, :, sl] = cr[0:1]
            hfi_ref[0, :, sl] = ci[0:1]
        h_prev_r = hpr[SUBLANES - 1:SUBLANES - 1 + n_chunks, :]
        h_prev_i = hpi[SUBLANES - 1:SUBLANES - 1 + n_chunks, :]
    else:
        a8r = pr_ref[0:1, :]
        a8i = pi_ref[0:1, :]
        hr = h0r_ref[...].reshape(n_seq, half)
        hi = h0i_ref[...].reshape(n_seq, half)
        for c, frows in fold_blocks:
            hpr[frows, :] = hr
            hpi[frows, :] = hi
            wr = wre[frows, :]
            wi = wim[frows, :]
            hr, hi = a8r * hr - a8i * hi + wr, a8r * hi + a8i * hr + wi
        hfr_ref[...] = hr.reshape(hfr_ref.shape)
        hfi_ref[...] = hi.reshape(hfi_ref.shape)
        h_prev_r = hpr[0:n_chunks, :]
        h_prev_i = hpi[0:n_chunks, :]

    lhs = jnp.concatenate([uflat[...], h_prev_r.astype(BF16), h_prev_i.astype(BF16)], axis=1)
    yf = jnp.dot(lhs, kc_ref[...], preferred_element_type=F32)
    for c, frows in fold_blocks:
        for t in range(S5_CHUNK):
            y_ref[token_rows(c, t), :] = yf[frows, t * LANES:(t + 1) * LANES]


def _s5_branch(u, h0_re, h0_im, bmat, kc, mr, mi, pr, pi_, layer, n_seq, seq_rows):
    n = u.shape[0]
    tile = n_seq * seq_rows
    n_total = h0_re.shape[0]
    n_chunks = tile // S5_CHUNK
    half = GROUPS_PER_LANE_BLOCK * SSM_STATE
    row_spec = pl.BlockSpec((tile, LANES), lambda lb, b: (b, lb))
    st_spec = pl.BlockSpec((n_seq, 1, half), lambda lb, b: (b, 0, lb))
    hp_rows = n_chunks + SUBLANES if n_seq == 1 else n_chunks
    kern = functools.partial(_s5_kernel, n_seq=n_seq, lane_chunk=256)
    return pl.pallas_call(
        kern,
        grid=(N_LANE_BLOCKS, n // tile),
        in_specs=[row_spec, st_spec, st_spec,
                  pl.BlockSpec((None, None) + bmat.shape[2:], lambda lb, b: (layer, lb, 0, 0)),
                  pl.BlockSpec((None, None) + kc.shape[2:], lambda lb, b: (layer, lb, 0, 0)),
                  pl.BlockSpec((None, 3, SUBLANES, half), lambda lb, b: (layer, 0, 0, lb)),
                  pl.BlockSpec((None, 3, SUBLANES, half), lambda lb, b: (layer, 0, 0, lb)),
                  pl.BlockSpec((None, SUBLANES, half), lambda lb, b: (layer, 0, lb)),
                  pl.BlockSpec((None, SUBLANES, half), lambda lb, b: (layer, 0, lb))],
        out_specs=[row_spec, st_spec, st_spec],
        out_shape=(jax.ShapeDtypeStruct((n, SSM_WIDTH), F32),
                   jax.ShapeDtypeStruct((n_total, 1, STATE_LANES), F32),
                   jax.ShapeDtypeStruct((n_total, 1, STATE_LANES), F32)),
        scratch_shapes=[pltpu.VMEM((n_chunks, S5_CHUNK * LANES), BF16),
                        pltpu.VMEM((n_chunks, half), F32), pltpu.VMEM((n_chunks, half), F32),
                        pltpu.VMEM((hp_rows, half), F32), pltpu.VMEM((hp_rows, half), F32)],
        compiler_params=_cparams(2),
        name="s5_branch",
    )(u, h0_re, h0_im, bmat, kc, mr, mi, pr, pi_)


def _sb_mask_queries(q_rows, qm_ref, tq):
    lane = lax.broadcasted_iota(jnp.int32, (tq, LANES), 1)
    for j in range(N_PAIRS):
        qpair = q_rows(j).astype(F32)
        for hh in range(HEADS_PER_LANE_BLOCK):
            h = j * HEADS_PER_LANE_BLOCK + hh
            in_head = (lane >= hh * HEAD_DIM) & (lane < (hh + 1) * HEAD_DIM)
            qm_ref[h * tq:(h + 1) * tq, :] = jnp.where(in_head, qpair, 0.0).astype(BF16)


class _SbTile(NamedTuple):
    tq: int
    k_tile: Any
    v_tile: Any
    keys_minor: bool
    valid: Any
    first: bool
    qm: Any
    lb: Any
    hl: Any
    a: Any
    car: Any
    acc: Any


def _sb_scores(t):
    pair = HEADS_PER_LANE_BLOCK * t.tq
    valid2 = None if t.valid is None else jnp.concatenate([t.valid] * HEADS_PER_LANE_BLOCK, axis=0)
    for j in range(N_PAIRS):
        rows = slice(j * pair, (j + 1) * pair)
        contract_k = 0 if t.keys_minor else 1
        z = lax.dot_general(t.qm[rows, :], t.k_tile(j), (((1,), (contract_k,)), ((), ())),
                            preferred_element_type=F32)
        sp = jnp.maximum(z, 0.0) + jnp.log(1.0 + jnp.exp(-jnp.abs(z)))
        t.lb[rows, :] = z - sp
        if valid2 is not None:
            sp = jnp.where(valid2, sp, 0.0)
        hi = sp.astype(BF16)
        t.hl[rows, :KEY_TILE] = hi
        t.hl[rows, KEY_TILE:] = (sp - hi.astype(F32)).astype(BF16)


def _sb_weights(t, tri_ref):
    cs = jnp.dot(t.hl[...], tri_ref[...], preferred_element_type=F32)
    cmax = None
    for h in range(N_HEADS):
        rows = slice(h * t.tq, (h + 1) * t.tq)
        after = cs[rows, :KEY_TILE]
        total = cs[rows, KEY_TILE:]
        if not t.first:
            carry = t.car[rows, :]
            after = after + carry
            total = total + carry
        a = jnp.exp(t.lb[rows, :] + after)
        if t.valid is not None:
            a = jnp.where(t.valid, a, 0.0)
        t.a[rows, :] = a.astype(BF16)
        t.car[rows, :] = total
        cmax = total if cmax is None else jnp.maximum(cmax, total)
    return jnp.max(cmax)


def _sb_output(t):
    pair = HEADS_PER_LANE_BLOCK * t.tq
    lane = lax.broadcasted_iota(jnp.int32, (t.tq, LANES), 1)
    for j in range(N_PAIRS):
        contract_v = 1 if t.keys_minor else 0
        pv = lax.dot_general(t.a[j * pair:(j + 1) * pair, :], t.v_tile(j), (((1,), (contract_v,)), ((), ())),
                             preferred_element_type=F32)
        out = jnp.where(lane < HEAD_DIM, pv[:t.tq], pv[t.tq:])
        if t.first:
            t.acc[:, j * LANES:(j + 1) * LANES] = out
        else:
            t.acc[:, j * LANES:(j + 1) * LANES] += out


def _sb_tiles(tiles, tri_ref):
    for t in tiles:
        _sb_scores(t)
    ms = [_sb_weights(t, tri_ref) for t in tiles]
    for t in tiles:
        _sb_output(t)
    return ms


def _sb_scratch(n_blocks, tq):
    rows = N_HEADS * tq
    return [pltpu.VMEM((n_blocks, rows, LANES), BF16),
            pltpu.VMEM((n_blocks, rows, LANES), F32),
            pltpu.VMEM((n_blocks, rows, 2 * KEY_TILE), BF16),
            pltpu.VMEM((n_blocks, rows, LANES), BF16),
            pltpu.VMEM((n_blocks, rows, LANES), F32),
            pltpu.VMEM((n_blocks, tq, ATT_WIDTH), F32),
            pltpu.SMEM((n_blocks,), F32)]


def _sb_prompt_kernel(q_ref, k_ref, v_ref, za_ref, tri_ref, o_ref,
                      qm_ref, lb_ref, hl_ref, a_ref, car_ref, acc_ref, m_ref):
    tq = SB_Q_ROWS
    n_blocks = q_ref.shape[0] // tq
    step_row0 = pl.program_id(1) * (n_blocks * tq)
    row = lax.broadcasted_iota(jnp.int32, (tq, KEY_TILE), 0)
    col = lax.broadcasted_iota(jnp.int32, (tq, KEY_TILE), 1)

    def tiles(ref, start):
        return lambda j: ref[pl.ds(start, KEY_TILE), j * LANES:(j + 1) * LANES]

    def band_start(s):
        return pl.multiple_of(jnp.maximum(step_row0 + (s + 1) * tq - KEY_TILE, 0), tq)

    def tile(s, lo, valid, first):
        return _SbTile(tq, tiles(k_ref, lo), tiles(v_ref, lo), False, valid, first, qm_ref.at[s], lb_ref.at[s],
                       hl_ref.at[s], a_ref.at[s], car_ref.at[s], acc_ref.at[s])

    band = []
    for s in range(n_blocks):
        lo = band_start(s)
        _sb_mask_queries(lambda j: q_ref[s * tq:(s + 1) * tq, j * LANES:(j + 1) * LANES], qm_ref.at[s], tq)
        valid = (col - row) < (step_row0 + s * tq - lo)
        band.append(tile(s, lo, valid, True))
    for s, m in enumerate(_sb_tiles(band, tri_ref)):
        m_ref[s] = m

    def more_tiles(s, _):
        def cond(state):
            hi, m = state
            return (hi > 0) & (m > SB_LOG_CUTOFF)

        def body(state):
            hi, _ = state
            lo = pl.multiple_of(jnp.maximum(hi - KEY_TILE, 0), tq)
            valid = col < (hi - lo)
            (m,) = _sb_tiles([tile(s, lo, valid, False)], tri_ref)
            return lo, m

        lax.while_loop(cond, body, (band_start(s), m_ref[s]))
        return 0

    lax.fori_loop(0, n_blocks, more_tiles, 0)
    for s in range(n_blocks):
        rows = slice(s * tq, (s + 1) * tq)
        o_ref[rows, :] = (acc_ref[s] * za_ref[rows, :].astype(F32)).astype(BF16)


def _sb_prompt(q, kb, vb, za, tri, n_batch, seq_len):
    step_rows = SB_STEP_BLOCKS * SB_Q_ROWS
    nq = seq_len // step_rows
    row_spec = pl.BlockSpec((step_rows, ATT_WIDTH), lambda b, i: (b * nq + i, 0))
    seq_spec = pl.BlockSpec((seq_len, ATT_WIDTH), lambda b, i: (b, 0))
    return pl.pallas_call(
        _sb_prompt_kernel,
        grid=(n_batch, nq),
        in_specs=[row_spec, seq_spec, seq_spec, row_spec, pl.BlockSpec(tri.shape, lambda b, i: (0, 0))],
        out_specs=row_spec,
        out_shape=jax.ShapeDtypeStruct(q.shape, BF16),
        scratch_shapes=_sb_scratch(SB_STEP_BLOCKS, SB_Q_ROWS),
        compiler_params=_cparams(2),
        name="sb_prompt",
    )(q, kb, vb, za, tri)


def _sb_sample_kernel(q_ref, kn_ref, vn_ref, kc_ref, vc_ref, za_ref, tri_ref, o_ref,
                      qm_ref, lb_ref, hl_ref, a_ref, car_ref, acc_ref, m_ref):
    tq = q_ref.shape[0]
    n_past = kc_ref.shape[-1] // KEY_TILE
    _sb_mask_queries(lambda j: q_ref[:, j * LANES:(j + 1) * LANES], qm_ref.at[0], tq)
    row = lax.broadcasted_iota(jnp.int32, (tq, KEY_TILE), 0)
    col = lax.broadcasted_iota(jnp.int32, (tq, KEY_TILE), 1)
    scratch = (qm_ref.at[0], lb_ref.at[0], hl_ref.at[0], a_ref.at[0], car_ref.at[0], acc_ref.at[0])

    new_tile = lambda ref: (lambda j: ref[:, j * LANES:(j + 1) * LANES])
    (m0,) = _sb_tiles([_SbTile(tq, new_tile(kn_ref), new_tile(vn_ref), False, col < row, True, *scratch)],
                      tri_ref)

    def past_tile(ref, t):
        start = pl.multiple_of(t * KEY_TILE, KEY_TILE)
        return lambda j: ref[0, 0, j * HEADS_PER_LANE_BLOCK:(j + 1) * HEADS_PER_LANE_BLOCK, :,
                             pl.ds(start, KEY_TILE)].reshape(LANES, KEY_TILE).astype(BF16)

    def cond(state):
        t, m = state
        return (t >= 0) & (m > SB_LOG_CUTOFF)

    def body(state):
        t, _ = state
        (m,) = _sb_tiles([_SbTile(tq, past_tile(kc_ref, t), past_tile(vc_ref, t), True, None, False,
                                  *scratch)], tri_ref)
        return t - 1, m

    lax.while_loop(cond, body, (n_past - 1, m0))
    o_ref[...] = (acc_ref[0] * za_ref[...].astype(F32)).astype(BF16)


def _sb_sample(q, k_new, v_new, k_past, v_past, layer, za, tri, n_batch, t_new):
    past = k_past.shape[-1]
    row_spec = pl.BlockSpec((t_new, ATT_WIDTH), lambda b: (b, 0))
    new_spec = pl.BlockSpec((KEY_TILE, ATT_WIDTH), lambda b: (b, 0))
    past_spec = pl.BlockSpec((1, 1, N_HEADS, HEAD_DIM, past), lambda b: (layer, b, 0, 0, 0))
    return pl.pallas_call(
        _sb_sample_kernel,
        grid=(n_batch,),
        in_specs=[row_spec, new_spec, new_spec, past_spec, past_spec, row_spec,
                  pl.BlockSpec(tri.shape, lambda b: (0, 0))],
        out_specs=row_spec,
        out_shape=jax.ShapeDtypeStruct(q.shape, BF16),
        scratch_shapes=_sb_scratch(1, t_new),
        compiler_params=_cparams(1),
        name="sb_sample",
    )(q, k_new, v_new, k_past, v_past, za, tri)


def _outproj_kernel(x_ref, ysc_ref, u_ref, zs_ref, ya_ref, gate_ref, d_ref, wglu_ref, bglu_ref, w_ref, o_ref):
    y = ysc_ref[...] + d_ref[...] * u_ref[...]
    g = _gelu_tanh(y)
    glu = _sigmoid(jnp.dot(g.astype(BF16), wglu_ref[...], preferred_element_type=F32) + bglu_ref[...])
    ys = (g * glu * zs_ref[...].astype(F32)).astype(BF16)
    mix = jnp.dot(ys, w_ref[:SSM_WIDTH, :], preferred_element_type=F32)
    mix = mix + jnp.dot(ya_ref[...], w_ref[SSM_WIDTH:, :], preferred_element_type=F32)
    o_ref[...] = x_ref[...] + gate_ref[0] * mix


def _out_projection(x2d, ysc, u, zs, ya, gate, d_skip, w_glu_b, b_glu, w_out_b, tm, rows_per_seq):
    n = x2d.shape[0]
    if rows_per_seq is None:
        mod_spec = pl.BlockSpec((1, tm, D_MODEL), lambda i: (0, i, 0))
    else:
        tps = rows_per_seq // tm
        mod_spec = pl.BlockSpec((1, 1, D_MODEL), lambda i: (i // tps, 0, 0))
    row_spec = lambda w: pl.BlockSpec((tm, w), lambda i: (i, 0))
    full = lambda a: pl.BlockSpec(a.shape, lambda i: (0,) * a.ndim)
    return pl.pallas_call(
        _outproj_kernel,
        grid=(n // tm,),
        in_specs=[row_spec(D_MODEL), row_spec(SSM_WIDTH), row_spec(SSM_WIDTH), row_spec(SSM_WIDTH),
                  row_spec(ATT_WIDTH), mod_spec, full(d_skip), full(w_glu_b), full(b_glu), full(w_out_b)],
        out_specs=row_spec(D_MODEL),
        out_shape=jax.ShapeDtypeStruct(x2d.shape, F32),
        compiler_params=_cparams(1),
        name="out_proj",
    )(x2d, ysc, u, zs, ya, gate, d_skip, w_glu_b, b_glu, w_out_b)


def _sb_sum_matrix():
    j = jnp.arange(2 * KEY_TILE)[:, None] % KEY_TILE
    s = jnp.arange(2 * KEY_TILE)[None, :]
    return -((s >= KEY_TILE) | (j > s)).astype(BF16)


def _head_sum_matrix():
    a = jnp.arange(ATT_WIDTH)
    return (a[:, None] // HEAD_DIM == a[None, :] // HEAD_DIM).astype(BF16)


def kernel(x_prompt, x_sample, c_prompt, c_sample, cache_k, cache_v, state_ssm_re, state_ssm_im, norm_g, w_mod, b_mod, w_in, ssm_a_re, ssm_a_im, ssm_log_dt, ssm_b_re, ssm_b_im, ssm_c_re, ssm_c_im, ssm_d, w_glu, b_glu, q_norm_g, k_norm_g, w_out):
    depth = w_in.shape[0]
    nb, seq_len, _ = x_prompt.shape
    ns, t_new, _ = x_sample.shape
    tm = 512

    mod = _modulation(jnp.concatenate([c_prompt, c_sample], axis=0), w_mod, b_mod)
    s5_ops = _s5_operators(ssm_a_re, ssm_a_im, ssm_log_dt, ssm_b_re, ssm_b_im, ssm_c_re, ssm_c_im)

    w_in_b = w_in.astype(BF16)
    w_out_b = w_out.astype(BF16)
    w_glu_b = w_glu.astype(BF16)
    tri = _sb_sum_matrix()
    hsum = _head_sum_matrix()

    xp = x_prompt.reshape(nb * seq_len, D_MODEL)
    xs = x_sample.reshape(ns * t_new, D_MODEL)
    zero_state = jnp.zeros((nb, 1, STATE_LANES), F32)
    kc = jnp.transpose(cache_k, (0, 1, 3, 4, 2))
    vc = jnp.transpose(cache_v, (0, 1, 3, 4, 2))
    outs = {name: [] for name in ("pr", "pi", "sr", "si")}
    pkv = tuple(jnp.zeros((depth, nb, N_HEADS, HEAD_DIM, seq_len), F32) for _ in range(2))
    skv = tuple(jnp.zeros((depth, ns * t_new, N_HEADS, HEAD_DIM), F32) for _ in range(2))
    for l in range(depth):
        g2 = norm_g[l].reshape(1, D_MODEL)
        qg = jnp.tile(q_norm_g[l], N_HEADS).reshape(1, ATT_WIDTH)
        kg = jnp.tile(k_norm_g[l], N_HEADS).reshape(1, ATT_WIDTH)
        d2 = ssm_d[l].reshape(1, SSM_WIDTH)
        bg2 = b_glu[l].reshape(1, SSM_WIDTH)
        glu_w = (d2, w_glu_b[l], bg2)

        mp = mod[l, :nb].reshape(nb, 1, 3 * D_MODEL)
        shift, scale, gate = (mp[:, :, i * D_MODEL:(i + 1) * D_MODEL] for i in range(3))
        u, zs, q, k_all, v_all, kb, vb, za = _in_projection(xp, shift, scale, g2, w_in_b[l], hsum, qg, kg,
                                                            tm, seq_len, l, depth, pkv)
        pkv = (k_all, v_all)
        ysc, hfr, hfi = _s5_branch(u, zero_state, zero_state, *s5_ops, l, 1, seq_len)
        ya = _sb_prompt(q, kb, vb, za, tri, nb, seq_len)
        xp = _out_projection(xp, ysc, u, zs, ya, gate, *glu_w, w_out_b[l], tm, seq_len)
        outs["pr"].append(hfr.reshape(nb, SSM_GROUPS, SSM_STATE))
        outs["pi"].append(hfi.reshape(nb, SSM_GROUPS, SSM_STATE))

        ms = jnp.repeat(mod[l, nb:], t_new, axis=0).reshape(1, ns * t_new, 3 * D_MODEL)
        shift, scale, gate = (ms[:, :, i * D_MODEL:(i + 1) * D_MODEL] for i in range(3))
        ts = ns * t_new
        u, zs, q, k_all, v_all, kb, vb, za = _in_projection(xs, shift, scale, g2, w_in_b[l], hsum, qg, kg,
                                                            ts, None, l, depth, skv)
        skv = (k_all, v_all)
        h0r = state_ssm_re[l].reshape(ns, 1, STATE_LANES)
        h0i = state_ssm_im[l].reshape(ns, 1, STATE_LANES)
        ysc, hfr, hfi = _s5_branch(u, h0r, h0i, *s5_ops, l, ns, t_new)
        pad = lambda a: jnp.pad(a.reshape(ns, t_new, ATT_WIDTH),
                                ((0, 0), (0, KEY_TILE - t_new), (0, 0))).reshape(ns * KEY_TILE, ATT_WIDTH)
        ya = _sb_sample(q, pad(kb), pad(vb), kc, vc, l, za, tri, ns, t_new)
        xs = _out_projection(xs, ysc, u, zs, ya, gate, *glu_w, w_out_b[l], ts, None)
        outs["sr"].append(hfr.reshape(ns, SSM_GROUPS, SSM_STATE))
        outs["si"].append(hfi.reshape(ns, SSM_GROUPS, SSM_STATE))

    st = lambda name: jnp.stack(outs[name])
    heads_p = lambda a: jnp.transpose(a, (0, 1, 4, 2, 3))
    heads_s = lambda a: a.reshape(depth, ns, t_new, N_HEADS, HEAD_DIM)
    return (xp.reshape(nb, seq_len, D_MODEL), xs.reshape(ns, t_new, D_MODEL),
            heads_p(pkv[0]), heads_p(pkv[1]), st("pr"), st("pi"),
            heads_s(skv[0]), heads_s(skv[1]), st("sr"), st("si"))
```

```python
import functools
import math
from typing import Any, NamedTuple

import jax
import jax.numpy as jnp
from jax import lax
from jax.experimental import pallas as pl
from jax.experimental.pallas import tpu as pltpu

F32 = jnp.float32
BF16 = jnp.bfloat16

D_MODEL = 1024
SSM_WIDTH = 512
SSM_GROUP = 16
SSM_GROUPS = 32
SSM_STATE = 64
STATE_LANES = SSM_GROUPS * SSM_STATE
ATT_WIDTH = 512
HEAD_DIM = 64
N_HEADS = 8
IN_WIDTH = 2 * SSM_WIDTH + 4 * ATT_WIDTH
EPS = 1e-6

LANES = 128
SUBLANES = 8
HEADS_PER_LANE_BLOCK = LANES // HEAD_DIM
GROUPS_PER_LANE_BLOCK = LANES // SSM_GROUP
N_LANE_BLOCKS = SSM_WIDTH // LANES
N_PAIRS = N_HEADS // HEADS_PER_LANE_BLOCK
S5_CHUNK = SUBLANES
KEY_TILE = 128
SB_Q_ROWS = 32
SB_STEP_BLOCKS = 8
SB_LOG_CUTOFF = -40.0
VMEM_LIMIT = 56 * 1024 * 1024


def _cparams(n_axes):
    return pltpu.CompilerParams(dimension_semantics=("arbitrary",) * n_axes,
                                vmem_limit_bytes=VMEM_LIMIT)


def _silu(x):
    return x * (1.0 / (1.0 + jnp.exp(-x)))


def _sigmoid(x):
    return 1.0 / (1.0 + jnp.exp(-x))


def _gelu_tanh(x):
    c = math.sqrt(2.0 / math.pi)
    return 0.5 * x * (1.0 + jnp.tanh(c * (x + 0.044715 * (x * x * x))))


def _mod_kernel(c_ref, w_ref, b_ref, o_ref):
    c = c_ref[...]
    a = _silu(c)
    o_ref[0] = jnp.dot(a, w_ref[0], preferred_element_type=F32,
                       precision=lax.Precision.HIGHEST) + b_ref[0]


def _modulation(c_all, w_mod, b_mod):
    depth = w_mod.shape[0]
    n = c_all.shape[0]
    nt = 3
    return pl.pallas_call(
        _mod_kernel,
        grid=(depth, nt),
        in_specs=[pl.BlockSpec((n, D_MODEL), lambda l, j: (0, 0)),
                  pl.BlockSpec((1, D_MODEL, D_MODEL), lambda l, j: (l, 0, j)),
                  pl.BlockSpec((1, 1, D_MODEL), lambda l, j: (l, 0, j))],
        out_specs=pl.BlockSpec((1, n, D_MODEL), lambda l, j: (l, 0, j)),
        out_shape=jax.ShapeDtypeStruct((depth, n, 3 * D_MODEL), F32),
        compiler_params=_cparams(2),
        name="modulation",
    )(c_all, w_mod, b_mod.reshape(depth, 1, 3 * D_MODEL))


def _disc_kernel(lr_ref, li_ref, ldt_ref, bre_ref, bim_ref, cre_ref, cim_ref,
                 pw8r_ref, pw8i_ref, lbr_ref, lbi_ref, clr_ref, cli_ref):
    lr = lr_ref[...]
    li = li_ref[...]
    dt = jnp.exp(ldt_ref[...])
    mag = jnp.exp(lr * dt)
    ang = li * dt
    ab_re = mag * jnp.cos(ang)
    ab_im = mag * jnp.sin(ang)
    den = lr * lr + li * li
    nr = ab_re - 1.0
    f_re = (nr * lr + ab_im * li) / den
    f_im = (ab_im * lr - nr * li) / den
    bre = bre_ref[...]
    bim = bim_ref[...]
    cre = cre_ref[...]
    cim = cim_ref[...]
    xr = f_re[:, None, :] * bre - f_im[:, None, :] * bim
    xi = f_re[:, None, :] * bim + f_im[:, None, :] * bre
    pr, pi_ = ab_re, ab_im
    for t in range(S5_CHUNK):
        lbr_ref[t] = xr
        lbi_ref[t] = xi
        clr_ref[t] = cre * pr[:, None, :] - cim * pi_[:, None, :]
        cli_ref[t] = cre * pi_[:, None, :] + cim * pr[:, None, :]
        if t + 1 < S5_CHUNK:
            xr, xi = (xr * ab_re[:, None, :] - xi * ab_im[:, None, :],
                      xr * ab_im[:, None, :] + xi * ab_re[:, None, :])
            pr, pi_ = pr * ab_re - pi_ * ab_im, pr * ab_im + pi_ * ab_re
    a8r, a8i = pr, pi_
    qr, qi = a8r, a8i
    for m in range(SUBLANES):
        pw8r_ref[m] = qr
        pw8i_ref[m] = qi
        qr, qi = qr * a8r - qi * a8i, qr * a8i + qi * a8r


def _place_kernel(lbr_ref, lbi_ref, clr_ref, cli_ref, cre_ref, cim_ref, bm_ref, ct_ref, kt_ref, kcat):
    gb = GROUPS_PER_LANE_BLOCK
    half = gb * SSM_STATE
    rep_p = (lax.broadcasted_iota(jnp.int32, (SSM_STATE, half), 0)
             == lax.broadcasted_iota(jnp.int32, (SSM_STATE, half), 1) % SSM_STATE).astype(BF16)
    rep_c = (lax.broadcasted_iota(jnp.int32, (SSM_GROUP, LANES), 0)
             == lax.broadcasted_iota(jnp.int32, (SSM_GROUP, LANES), 1) % SSM_GROUP).astype(BF16)
    same_p = (lax.broadcasted_iota(jnp.int32, (LANES, half), 0) // SSM_GROUP
              == lax.broadcasted_iota(jnp.int32, (LANES, half), 1) // SSM_STATE)
    same_c = (lax.broadcasted_iota(jnp.int32, (LANES, LANES), 0) // SSM_GROUP
              == lax.broadcasted_iota(jnp.int32, (LANES, LANES), 1) // SSM_GROUP)

    def place_states(x):
        tiled = jnp.dot(x.reshape(LANES, SSM_STATE).astype(BF16), rep_p, preferred_element_type=F32)
        return jnp.where(same_p, tiled, 0.0).astype(BF16)

    for s in range(S5_CHUNK):
        rows = slice(s * LANES, (s + 1) * LANES)
        bm_ref[0, rows, :half] = place_states(lbr_ref[S5_CHUNK - 1 - s])
        bm_ref[0, rows, half:] = place_states(lbi_ref[S5_CHUNK - 1 - s])
        ct_ref[0, rows, :half] = place_states(clr_ref[s])
        ct_ref[0, rows, half:] = place_states(-cli_ref[s])

    for g in range(gb):
        b_cat = jnp.concatenate([lbr_ref[:, g].reshape(LANES, SSM_STATE),
                                 lbi_ref[:, g].reshape(LANES, SSM_STATE)], axis=1)
        c_cat = jnp.concatenate([cre_ref[g], -cim_ref[g]], axis=1)
        kg = lax.dot_general(b_cat, c_cat, (((1,), (1,)), ((), ())), preferred_element_type=F32,
                             precision=lax.Precision.HIGHEST)
        for j in range(S5_CHUNK):
            kcat[j, g * SSM_GROUP:(g + 1) * SSM_GROUP, :] = kg[j * SSM_GROUP:(j + 1) * SSM_GROUP, :]
    zero = jnp.zeros((LANES, LANES), BF16)
    for j in range(S5_CHUNK):
        tiled = jnp.dot(kcat[j].astype(BF16), rep_c, preferred_element_type=F32)
        blk = jnp.where(same_c, tiled, 0.0).astype(BF16)
        for s in range(S5_CHUNK - j):
            kt_ref[0, s * LANES:(s + 1) * LANES, (s + j) * LANES:(s + j + 1) * LANES] = blk
        if j > 0:
            for t in range(S5_CHUNK - j):
                kt_ref[0, (t + j) * LANES:(t + j + 1) * LANES, t * LANES:(t + 1) * LANES] = zero


def _s5_operators(a_re, a_im, log_dt, b_re, b_im, c_re, c_im):
    depth = a_re.shape[0]
    rows = depth * SSM_GROUPS
    lr = a_re.reshape(rows, SSM_STATE)
    li = a_im.reshape(rows, SSM_STATE)
    ldt = jnp.broadcast_to(log_dt.reshape(rows, 1), (rows, SSM_STATE))
    bre = b_re.transpose(0, 1, 3, 2).reshape(rows, SSM_GROUP, SSM_STATE)
    bim = b_im.transpose(0, 1, 3, 2).reshape(rows, SSM_GROUP, SSM_STATE)
    cre = c_re.reshape(rows, SSM_GROUP, SSM_STATE)
    cim = c_im.reshape(rows, SSM_GROUP, SSM_STATE)
    pw_shape = jax.ShapeDtypeStruct((SUBLANES, rows, SSM_STATE), F32)
    op_shape = jax.ShapeDtypeStruct((S5_CHUNK, rows, SSM_GROUP, SSM_STATE), F32)
    pw8r, pw8i, lbr, lbi, clr, cli = pl.pallas_call(
        _disc_kernel,
        out_shape=(pw_shape, pw_shape, op_shape, op_shape, op_shape, op_shape),
        compiler_params=_cparams(0),
        name="s5_discretise",
    )(lr, li, ldt, bre, bim, cre, cim)
    gb = GROUPS_PER_LANE_BLOCK
    n_blk = rows // gb
    width = S5_CHUNK * LANES
    op_spec = pl.BlockSpec((S5_CHUNK, gb, SSM_GROUP, SSM_STATE), lambda i: (0, i, 0, 0))
    c_spec = pl.BlockSpec((gb, SSM_GROUP, SSM_STATE), lambda i: (i, 0, 0))
    mat_spec = pl.BlockSpec((1, width, width), lambda i: (i, 0, 0))
    mat_shape = jax.ShapeDtypeStruct((n_blk, width, width), BF16)
    bmat, ctm, ktoep = pl.pallas_call(
        _place_kernel,
        grid=(n_blk,),
        in_specs=[op_spec] * 4 + [c_spec] * 2,
        out_specs=[mat_spec] * 3,
        out_shape=(mat_shape,) * 3,
        scratch_shapes=[pltpu.VMEM((S5_CHUNK, LANES, SSM_GROUP), F32)],
        compiler_params=_cparams(1),
        name="s5_place_operators",
    )(lbr, lbi, clr, cli, cre, cim)

    pw8r = pw8r.reshape(SUBLANES, depth, STATE_LANES).transpose(1, 0, 2)
    pw8i = pw8i.reshape(SUBLANES, depth, STATE_LANES).transpose(1, 0, 2)
    row = jnp.arange(SUBLANES)[None, :, None]

    def step_mult(pw):
        return jnp.stack([jnp.where(row >= k, pw[:, k - 1][:, None, :], 0.0) for k in (1, 2, 4)], axis=1)

    return bmat, ktoep, ctm, step_mult(pw8r), step_mult(pw8i), pw8r, pw8i


def _heads_major(x, position_minor):
    if position_minor:
        return x.T.reshape(N_HEADS, HEAD_DIM, x.shape[0])
    heads = jnp.stack([x[:, h * HEAD_DIM:(h + 1) * HEAD_DIM] for h in range(N_HEADS)], axis=0)
    return pltpu.einshape("htd->thd", heads)


def _inproj_kernel(x_ref, shift_ref, scale_ref, g_ref, w_ref, hsum_ref, qg_ref, kg_ref, *rest,
                   position_minor):
    u_ref, zs_ref, q_ref, k_ref, v_ref, kb_ref, vb_ref, za_ref = rest[-8:]
    x = x_ref[...]
    ms = jnp.mean(x * x, axis=-1, keepdims=True)
    h = x * lax.rsqrt(ms + EPS) * g_ref[...]
    h = h * (1.0 + scale_ref[0]) + shift_ref[0]
    hb = h.astype(BF16)

    def proj(c):
        return jnp.dot(hb, w_ref[:, c * SSM_WIDTH:(c + 1) * SSM_WIDTH], preferred_element_type=F32)

    def head_norm(p, g):
        ss = jnp.dot((p * p).astype(BF16), hsum_ref[...], preferred_element_type=F32)
        return p * lax.rsqrt(ss * (1.0 / HEAD_DIM) + EPS) * g

    u_ref[...] = proj(0)
    zs_ref[...] = _silu(proj(1)).astype(BF16)
    q = head_norm(proj(2), qg_ref[...])
    q_ref[...] = (q * (HEAD_DIM ** -0.5)).astype(BF16)
    k = head_norm(proj(3), kg_ref[...])
    k_ref[...] = _heads_major(k, position_minor).reshape(k_ref.shape)
    kb_ref[...] = k.astype(BF16)
    v = proj(4)
    v_ref[...] = _heads_major(v, position_minor).reshape(v_ref.shape)
    vb_ref[...] = v.astype(BF16)
    za_ref[...] = _silu(proj(5)).astype(BF16)


def _in_projection(x2d, shift, scale, norm_g, w_in_b, hsum, q_g, k_g, tm, rows_per_seq,
                   layer, depth, kv_all):
    n = x2d.shape[0]
    nt = n // tm
    if rows_per_seq is None:
        mod_spec = pl.BlockSpec((1, tm, D_MODEL), lambda i: (0, i, 0))
    else:
        tps = rows_per_seq // tm
        mod_spec = pl.BlockSpec((1, 1, D_MODEL), lambda i: (i // tps, 0, 0))
    row_spec = lambda w: pl.BlockSpec((tm, w), lambda i: (i, 0))
    full = lambda a: pl.BlockSpec(a.shape, lambda i: (0,) * a.ndim)
    bf = jax.ShapeDtypeStruct((n, SSM_WIDTH), BF16)
    position_minor = rows_per_seq is not None
    if position_minor:
        f3 = jax.ShapeDtypeStruct((depth, n // rows_per_seq, N_HEADS, HEAD_DIM, rows_per_seq), F32)
        kv_spec = pl.BlockSpec((1, 1, N_HEADS, HEAD_DIM, tm), lambda i: (layer, i // tps, 0, 0, i % tps))
    else:
        f3 = jax.ShapeDtypeStruct((depth, n, N_HEADS, HEAD_DIM), F32)
        kv_spec = pl.BlockSpec((1, tm, N_HEADS, HEAD_DIM), lambda i: (layer, i, 0, 0))
    assert kv_all[0].shape == f3.shape and kv_all[1].shape == f3.shape
    in_specs = [row_spec(D_MODEL), mod_spec, mod_spec, full(norm_g), full(w_in_b), full(hsum),
                full(q_g), full(k_g)] + [pl.BlockSpec(memory_space=pl.ANY)] * 2
    args = [x2d, shift, scale, norm_g, w_in_b, hsum, q_g, k_g, *kv_all]
    aliases = {len(args) - 2: 3, len(args) - 1: 4}
    return pl.pallas_call(
        functools.partial(_inproj_kernel, position_minor=position_minor),
        grid=(nt,),
        in_specs=in_specs,
        out_specs=[row_spec(SSM_WIDTH)] * 3 + [kv_spec] * 2 + [row_spec(SSM_WIDTH)] * 3,
        out_shape=(jax.ShapeDtypeStruct((n, SSM_WIDTH), F32), bf, bf, f3, f3, bf, bf, bf),
        input_output_aliases=aliases,
        compiler_params=_cparams(1),
        name="in_proj",
    )(*args)


def _s5_kernel(u_ref, h0r_ref, h0i_ref, bm_ref, kt_ref, ct_ref, mr_ref, mi_ref, pr_ref, pi_ref,
               y_ref, hfr_ref, hfi_ref,
               uflat, wre, wim, hpr, hpi, *, n_seq, lane_chunk):
    rows = u_ref.shape[0]
    n_chunks = rows // S5_CHUNK
    seq_chunks = n_chunks // n_seq
    half = GROUPS_PER_LANE_BLOCK * SSM_STATE

    def token_rows(c, t):
        if n_seq == 1:
            return pl.ds(t, n_chunks, stride=S5_CHUNK)
        return pl.ds(c * S5_CHUNK + t, n_seq, stride=seq_chunks * S5_CHUNK)

    fold_blocks = [(0, slice(None))] if n_seq == 1 else [(c, slice(c * n_seq, (c + 1) * n_seq))
                                                        for c in range(seq_chunks)]
    for c, frows in fold_blocks:
        for t in range(S5_CHUNK):
            uflat[frows, t * LANES:(t + 1) * LANES] = u_ref[token_rows(c, t), :].astype(BF16)
    w = jnp.dot(uflat[...], bm_ref[...], preferred_element_type=F32)
    wre[...] = w[:, :half]
    wim[...] = w[:, half:]

    if n_seq == 1:
        hpr[0:SUBLANES, :] = jnp.broadcast_to(h0r_ref[0], (SUBLANES, half))
        hpi[0:SUBLANES, :] = jnp.broadcast_to(h0i_ref[0], (SUBLANES, half))
        for ch in range(half // lane_chunk):
            sl = slice(ch * lane_chunk, (ch + 1) * lane_chunk)
            prc = pr_ref[:, sl]
            pic = pi_ref[:, sl]

            def group_body(r, carry):
                cr, ci = carry
                row = pl.multiple_of(r * SUBLANES, SUBLANES)
                vr = wre[pl.ds(row, SUBLANES), sl]
                vi = wim[pl.ds(row, SUBLANES), sl]
                for idx, k in enumerate((1, 2, 4)):
                    mr = mr_ref[idx, :, sl]
                    mi = mi_ref[idx, :, sl]
                    sr = pltpu.roll(vr, k, 0)
                    si = pltpu.roll(vi, k, 0)
                    vr, vi = vr + (mr * sr - mi * si), vi + (mr * si + mi * sr)
                hr = vr + (prc * cr - pic * ci)
                hi = vi + (prc * ci + pic * cr)
                hpr[pl.ds(row + SUBLANES, SUBLANES), sl] = hr
                hpi[pl.ds(row + SUBLANES, SUBLANES), sl] = hi
                last = SUBLANES - 1
                return (jnp.broadcast_to(hr[last:last + 1], hr.shape),
                        jnp.broadcast_to(hi[last:last + 1], hi.shape))

            cr, ci = lax.fori_loop(0, n_chunks // SUBLANES, group_body,
                                   (hpr[0:SUBLANES, sl], hpi[0:SUBLANES, sl]), unroll=2)
            hfr_ref[0, :, sl] = cr[0:1]
            hfi_ref[0, :, sl] = ci[0:1]
        h_prev_r = hpr[SUBLANES - 1:SUBLANES - 1 + n_chunks, :]
        h_prev_i = hpi[SUBLANES - 1:SUBLANES - 1 + n_chunks, :]
    else:
        a8r = pr_ref[0:1, :]
        a8i = pi_ref[0:1, :]
        hr = h0r_ref[...].reshape(n_seq, half)
        hi = h0i_ref[...].reshape(n_seq, half)
        for c, frows in fold_blocks:
            hpr[frows, :] = hr
            hpi[frows, :] = hi
            wr = wre[frows, :]
            wi = wim[frows, :]
            hr, hi = a8r * hr - a8i * hi + wr, a8r * hi + a8i * hr + wi
        hfr_ref[...] = hr.reshape(hfr_ref.shape)
        hfi_ref[...] = hi.reshape(hfi_ref.shape)
        h_prev_r = hpr[0:n_chunks, :]
        h_prev_i = hpi[0:n_chunks, :]

    h_prev = jnp.concatenate([h_prev_r.astype(BF16), h_prev_i.astype(BF16)], axis=1)
    yf = jnp.dot(uflat[...], kt_ref[...], preferred_element_type=F32)
    yf = yf + lax.dot_general(h_prev, ct_ref[...], (((1,), (1,)), ((), ())), preferred_element_type=F32)
    for c, frows in fold_blocks:
        for t in range(S5_CHUNK):
            y_ref[token_rows(c, t), :] = yf[frows, t * LANES:(t + 1) * LANES]


def _s5_branch(u, h0_re, h0_im, bmat, ktoep, ctm, mr, mi, pr, pi_, layer, n_seq, seq_rows):
    n = u.shape[0]
    tile = n_seq * seq_rows
    n_total = h0_re.shape[0]
    n_chunks = tile // S5_CHUNK
    half = GROUPS_PER_LANE_BLOCK * SSM_STATE
    row_spec = pl.BlockSpec((tile, LANES), lambda lb, b: (b, lb))
    st_spec = pl.BlockSpec((n_seq, 1, half), lambda lb, b: (b, 0, lb))
    hp_rows = n_chunks + SUBLANES if n_seq == 1 else n_chunks
    mat_spec = pl.BlockSpec((None,) + bmat.shape[1:], lambda lb, b: (layer * N_LANE_BLOCKS + lb, 0, 0))
    kern = functools.partial(_s5_kernel, n_seq=n_seq, lane_chunk=256)
    return pl.pallas_call(
        kern,
        grid=(N_LANE_BLOCKS, n // tile),
        in_specs=[row_spec, st_spec, st_spec,
                  mat_spec, mat_spec, mat_spec,
                  pl.BlockSpec((None, 3, SUBLANES, half), lambda lb, b: (layer, 0, 0, lb)),
                  pl.BlockSpec((None, 3, SUBLANES, half), lambda lb, b: (layer, 0, 0, lb)),
                  pl.BlockSpec((None, SUBLANES, half), lambda lb, b: (layer, 0, lb)),
                  pl.BlockSpec((None, SUBLANES, half), lambda lb, b: (layer, 0, lb))],
        out_specs=[row_spec, st_spec, st_spec],
        out_shape=(jax.ShapeDtypeStruct((n, SSM_WIDTH), F32),
                   jax.ShapeDtypeStruct((n_total, 1, STATE_LANES), F32),
                   jax.ShapeDtypeStruct((n_total, 1, STATE_LANES), F32)),
        scratch_shapes=[pltpu.VMEM((n_chunks, S5_CHUNK * LANES), BF16),
                        pltpu.VMEM((n_chunks, half), F32), pltpu.VMEM((n_chunks, half), F32),
                        pltpu.VMEM((hp_rows, half), F32), pltpu.VMEM((hp_rows, half), F32)],
        compiler_params=_cparams(2),
        name="s5_branch",
    )(u, h0_re, h0_im, bmat, ktoep, ctm, mr, mi, pr, pi_)


def _sb_mask_queries(q_rows, qm_ref, tq):
    lane = lax.broadcasted_iota(jnp.int32, (tq, LANES), 1)
    for j in range(N_PAIRS):
        qpair = q_rows(j).astype(F32)
        for hh in range(HEADS_PER_LANE_BLOCK):
            h = j * HEADS_PER_LANE_BLOCK + hh
            in_head = (lane >= hh * HEAD_DIM) & (lane < (hh + 1) * HEAD_DIM)
            qm_ref[h * tq:(h + 1) * tq, :] = jnp.where(in_head, qpair, 0.0).astype(BF16)


class _SbTile(NamedTuple):
    tq: int
    k_tile: Any
    v_tile: Any
    keys_minor: bool
    valid: Any
    first: bool
    qm: Any
    lb: Any
    hl: Any
    a: Any
    car: Any
    acc: Any


def _sb_scores(t):
    pair = HEADS_PER_LANE_BLOCK * t.tq
    valid2 = None if t.valid is None else jnp.concatenate([t.valid] * HEADS_PER_LANE_BLOCK, axis=0)
    for j in range(N_PAIRS):
        rows = slice(j * pair, (j + 1) * pair)
        contract_k = 0 if t.keys_minor else 1
        z = lax.dot_general(t.qm[rows, :], t.k_tile(j), (((1,), (contract_k,)), ((), ())),
                            preferred_element_type=F32)
        sp = jnp.maximum(z, 0.0) + jnp.log(1.0 + jnp.exp(-jnp.abs(z)))
        t.lb[rows, :] = z - sp
        if valid2 is not None:
            sp = jnp.where(valid2, sp, 0.0)
        hi = sp.astype(BF16)
        t.hl[rows, :KEY_TILE] = hi
        t.hl[rows, KEY_TILE:] = (sp - hi.astype(F32)).astype(BF16)


def _sb_weights(t, tri_ref):
    cs = jnp.dot(t.hl[...], tri_ref[...], preferred_element_type=F32)
    cmax = None
    for h in range(N_HEADS):
        rows = slice(h * t.tq, (h + 1) * t.tq)
        after = cs[rows, :KEY_TILE]
        total = cs[rows, KEY_TILE:]
        if not t.first:
            carry = t.car[rows, :]
            after = after + carry
            total = total + carry
        a = jnp.exp(t.lb[rows, :] + after)
        if t.valid is not None:
            a = jnp.where(t.valid, a, 0.0)
        t.a[rows, :] = a.astype(BF16)
        t.car[rows, :] = total
        cmax = total if cmax is None else jnp.maximum(cmax, total)
    return jnp.max(cmax)


def _sb_output(t):
    pair = HEADS_PER_LANE_BLOCK * t.tq
    lane = lax.broadcasted_iota(jnp.int32, (t.tq, LANES), 1)
    for j in range(N_PAIRS):
        contract_v = 1 if t.keys_minor else 0
        pv = lax.dot_general(t.a[j * pair:(j + 1) * pair, :], t.v_tile(j), (((1,), (contract_v,)), ((), ())),
                             preferred_element_type=F32)
        out = jnp.where(lane < HEAD_DIM, pv[:t.tq], pv[t.tq:])
        if t.first:
            t.acc[:, j * LANES:(j + 1) * LANES] = out
        else:
            t.acc[:, j * LANES:(j + 1) * LANES] += out


def _sb_tiles(tiles, tri_ref):
    for t in tiles:
        _sb_scores(t)
    ms = [_sb_weights(t, tri_ref) for t in tiles]
    for t in tiles:
        _sb_output(t)
    return ms


def _sb_scratch(n_blocks, tq):
    rows = N_HEADS * tq
    return [pltpu.VMEM((n_blocks, rows, LANES), BF16),
            pltpu.VMEM((n_blocks, rows, LANES), F32),
            pltpu.VMEM((n_blocks, rows, 2 * KEY_TILE), BF16),
            pltpu.VMEM((n_blocks, rows, LANES), BF16),
            pltpu.VMEM((n_blocks, rows, LANES), F32),
            pltpu.VMEM((n_blocks, tq, ATT_WIDTH), F32),
            pltpu.SMEM((n_blocks,), F32)]


def _sb_prompt_kernel(q_ref, k_ref, v_ref, za_ref, tri_ref, o_ref,
                      qm_ref, lb_ref, hl_ref, a_ref, car_ref, acc_ref, m_ref):
    tq = SB_Q_ROWS
    n_blocks = q_ref.shape[0] // tq
    step_row0 = pl.program_id(1) * (n_blocks * tq)
    row = lax.broadcasted_iota(jnp.int32, (tq, KEY_TILE), 0)
    col = lax.broadcasted_iota(jnp.int32, (tq, KEY_TILE), 1)

    def tiles(ref, start):
        return lambda j: ref[pl.ds(start, KEY_TILE), j * LANES:(j + 1) * LANES]

    def band_start(s):
        return pl.multiple_of(jnp.maximum(step_row0 + (s + 1) * tq - KEY_TILE, 0), tq)

    def tile(s, lo, valid, first):
        return _SbTile(tq, tiles(k_ref, lo), tiles(v_ref, lo), False, valid, first, qm_ref.at[s], lb_ref.at[s],
                       hl_ref.at[s], a_ref.at[s], car_ref.at[s], acc_ref.at[s])

    band = []
    for s in range(n_blocks):
        lo = band_start(s)
        _sb_mask_queries(lambda j: q_ref[s * tq:(s + 1) * tq, j * LANES:(j + 1) * LANES], qm_ref.at[s], tq)
        valid = (col - row) < (step_row0 + s * tq - lo)
        band.append(tile(s, lo, valid, True))
    for s, m in enumerate(_sb_tiles(band, tri_ref)):
        m_ref[s] = m

    def more_tiles(s, _):
        def cond(state):
            hi, m = state
            return (hi > 0) & (m > SB_LOG_CUTOFF)

        def body(state):
            hi, _ = state
            lo = pl.multiple_of(jnp.maximum(hi - KEY_TILE, 0), tq)
            valid = col < (hi - lo)
            (m,) = _sb_tiles([tile(s, lo, valid, False)], tri_ref)
            return lo, m

        lax.while_loop(cond, body, (band_start(s), m_ref[s]))
        return 0

    lax.fori_loop(0, n_blocks, more_tiles, 0)
    for s in range(n_blocks):
        rows = slice(s * tq, (s + 1) * tq)
        o_ref[rows, :] = (acc_ref[s] * za_ref[rows, :].astype(F32)).astype(BF16)


def _sb_prompt(q, kb, vb, za, tri, n_batch, seq_len):
    step_rows = SB_STEP_BLOCKS * SB_Q_ROWS
    nq = seq_len // step_rows
    row_spec = pl.BlockSpec((step_rows, ATT_WIDTH), lambda b, i: (b * nq + i, 0))
    seq_spec = pl.BlockSpec((seq_len, ATT_WIDTH), lambda b, i: (b, 0))
    return pl.pallas_call(
        _sb_prompt_kernel,
        grid=(n_batch, nq),
        in_specs=[row_spec, seq_spec, seq_spec, row_spec, pl.BlockSpec(tri.shape, lambda b, i: (0, 0))],
        out_specs=row_spec,
        out_shape=jax.ShapeDtypeStruct(q.shape, BF16),
        scratch_shapes=_sb_scratch(SB_STEP_BLOCKS, SB_Q_ROWS),
        compiler_params=_cparams(2),
        name="sb_prompt",
    )(q, kb, vb, za, tri)


def _sb_sample_kernel(q_ref, kn_ref, vn_ref, kc_ref, vc_ref, za_ref, tri_ref, o_ref,
                      qm_ref, lb_ref, hl_ref, a_ref, car_ref, acc_ref, m_ref):
    tq = q_ref.shape[0]
    n_past = kc_ref.shape[-1] // KEY_TILE
    _sb_mask_queries(lambda j: q_ref[:, j * LANES:(j + 1) * LANES], qm_ref.at[0], tq)
    row = lax.broadcasted_iota(jnp.int32, (tq, KEY_TILE), 0)
    col = lax.broadcasted_iota(jnp.int32, (tq, KEY_TILE), 1)
    scratch = (qm_ref.at[0], lb_ref.at[0], hl_ref.at[0], a_ref.at[0], car_ref.at[0], acc_ref.at[0])

    new_tile = lambda ref: (lambda j: ref[:, j * LANES:(j + 1) * LANES])
    (m0,) = _sb_tiles([_SbTile(tq, new_tile(kn_ref), new_tile(vn_ref), False, col < row, True, *scratch)],
                      tri_ref)

    def past_tile(ref, t):
        start = pl.multiple_of(t * KEY_TILE, KEY_TILE)
        return lambda j: ref[0, 0, j * HEADS_PER_LANE_BLOCK:(j + 1) * HEADS_PER_LANE_BLOCK, :,
                             pl.ds(start, KEY_TILE)].reshape(LANES, KEY_TILE).astype(BF16)

    def cond(state):
        t, m = state
        return (t >= 0) & (m > SB_LOG_CUTOFF)

    def body(state):
        t, _ = state
        (m,) = _sb_tiles([_SbTile(tq, past_tile(kc_ref, t), past_tile(vc_ref, t), True, None, False,
                                  *scratch)], tri_ref)
        return t - 1, m

    lax.while_loop(cond, body, (n_past - 1, m0))
    o_ref[...] = (acc_ref[0] * za_ref[...].astype(F32)).astype(BF16)


def _sb_sample(q, k_new, v_new, k_past, v_past, layer, za, tri, n_batch, t_new):
    past = k_past.shape[-1]
    row_spec = pl.BlockSpec((t_new, ATT_WIDTH), lambda b: (b, 0))
    new_spec = pl.BlockSpec((KEY_TILE, ATT_WIDTH), lambda b: (b, 0))
    past_spec = pl.BlockSpec((1, 1, N_HEADS, HEAD_DIM, past), lambda b: (layer, b, 0, 0, 0))
    return pl.pallas_call(
        _sb_sample_kernel,
        grid=(n_batch,),
        in_specs=[row_spec, new_spec, new_spec, past_spec, past_spec, row_spec,
                  pl.BlockSpec(tri.shape, lambda b: (0, 0))],
        out_specs=row_spec,
        out_shape=jax.ShapeDtypeStruct(q.shape, BF16),
        scratch_shapes=_sb_scratch(1, t_new),
        compiler_params=_cparams(1),
        name="sb_sample",
    )(q, k_new, v_new, k_past, v_past, za, tri)


def _outproj_kernel(x_ref, ysc_ref, u_ref, zs_ref, ya_ref, gate_ref, d_ref, wglu_ref, bglu_ref, w_ref, o_ref):
    y = ysc_ref[...] + d_ref[...] * u_ref[...]
    g = _gelu_tanh(y)
    glu = _sigmoid(jnp.dot(g.astype(BF16), wglu_ref[...], preferred_element_type=F32) + bglu_ref[...])
    ys = (g * glu * zs_ref[...].astype(F32)).astype(BF16)
    mix = jnp.dot(ys, w_ref[:SSM_WIDTH, :], preferred_element_type=F32)
    mix = mix + jnp.dot(ya_ref[...], w_ref[SSM_WIDTH:, :], preferred_element_type=F32)
    o_ref[...] = x_ref[...] + gate_ref[0] * mix


def _out_projection(x2d, ysc, u, zs, ya, gate, d_skip, w_glu_b, b_glu, w_out_b, tm, rows_per_seq):
    n = x2d.shape[0]
    if rows_per_seq is None:
        mod_spec = pl.BlockSpec((1, tm, D_MODEL), lambda i: (0, i, 0))
    else:
        tps = rows_per_seq // tm
        mod_spec = pl.BlockSpec((1, 1, D_MODEL), lambda i: (i // tps, 0, 0))
    row_spec = lambda w: pl.BlockSpec((tm, w), lambda i: (i, 0))
    full = lambda a: pl.BlockSpec(a.shape, lambda i: (0,) * a.ndim)
    return pl.pallas_call(
        _outproj_kernel,
        grid=(n // tm,),
        in_specs=[row_spec(D_MODEL), row_spec(SSM_WIDTH), row_spec(SSM_WIDTH), row_spec(SSM_WIDTH),
                  row_spec(ATT_WIDTH), mod_spec, full(d_skip), full(w_glu_b), full(b_glu), full(w_out_b)],
        out_specs=row_spec(D_MODEL),
        out_shape=jax.ShapeDtypeStruct(x2d.shape, F32),
        compiler_params=_cparams(1),
        name="out_proj",
    )(x2d, ysc, u, zs, ya, gate, d_skip, w_glu_b, b_glu, w_out_b)


def _sb_sum_matrix():
    j = jnp.arange(2 * KEY_TILE)[:, None] % KEY_TILE
    s = jnp.arange(2 * KEY_TILE)[None, :]
    return -((s >= KEY_TILE) | (j > s)).astype(BF16)


def _head_sum_matrix():
    a = jnp.arange(ATT_WIDTH)
    return (a[:, None] // HEAD_DIM == a[None, :] // HEAD_DIM).astype(BF16)


def kernel(x_prompt, x_sample, c_prompt, c_sample, cache_k, cache_v, state_ssm_re, state_ssm_im, norm_g, w_mod, b_mod, w_in, ssm_a_re, ssm_a_im, ssm_log_dt, ssm_b_re, ssm_b_im, ssm_c_re, ssm_c_im, ssm_d, w_glu, b_glu, q_norm_g, k_norm_g, w_out):
    depth = w_in.shape[0]
    nb, seq_len, _ = x_prompt.shape
    ns, t_new, _ = x_sample.shape
    tm = 512

    mod = _modulation(jnp.concatenate([c_prompt, c_sample], axis=0), w_mod, b_mod)
    s5_ops = _s5_operators(ssm_a_re, ssm_a_im, ssm_log_dt, ssm_b_re, ssm_b_im, ssm_c_re, ssm_c_im)

    w_in_b = w_in.astype(BF16)
    w_out_b = w_out.astype(BF16)
    w_glu_b = w_glu.astype(BF16)
    tri = _sb_sum_matrix()
    hsum = _head_sum_matrix()

    xp = x_prompt.reshape(nb * seq_len, D_MODEL)
    xs = x_sample.reshape(ns * t_new, D_MODEL)
    zero_state = jnp.zeros((nb, 1, STATE_LANES), F32)
    kc = jnp.transpose(cache_k, (0, 1, 3, 4, 2))
    vc = jnp.transpose(cache_v, (0, 1, 3, 4, 2))
    outs = {name: [] for name in ("pr", "pi", "sr", "si")}
    pkv = tuple(jnp.zeros((depth, nb, N_HEADS, HEAD_DIM, seq_len), F32) for _ in range(2))
    skv = tuple(jnp.zeros((depth, ns * t_new, N_HEADS, HEAD_DIM), F32) for _ in range(2))
    for l in range(depth):
        g2 = norm_g[l].reshape(1, D_MODEL)
        qg = jnp.tile(q_norm_g[l], N_HEADS).reshape(1, ATT_WIDTH)
        kg = jnp.tile(k_norm_g[l], N_HEADS).reshape(1, ATT_WIDTH)
        d2 = ssm_d[l].reshape(1, SSM_WIDTH)
        bg2 = b_glu[l].reshape(1, SSM_WIDTH)
        glu_w = (d2, w_glu_b[l], bg2)

        mp = mod[l, :nb].reshape(nb, 1, 3 * D_MODEL)
        shift, scale, gate = (mp[:, :, i * D_MODEL:(i + 1) * D_MODEL] for i in range(3))
        u, zs, q, k_all, v_all, kb, vb, za = _in_projection(xp, shift, scale, g2, w_in_b[l], hsum, qg, kg,
                                                            tm, seq_len, l, depth, pkv)
        pkv = (k_all, v_all)
        ysc, hfr, hfi = _s5_branch(u, zero_state, zero_state, *s5_ops, l, 1, seq_len)
        ya = _sb_prompt(q, kb, vb, za, tri, nb, seq_len)
        xp = _out_projection(xp, ysc, u, zs, ya, gate, *glu_w, w_out_b[l], tm, seq_len)
        outs["pr"].append(hfr.reshape(nb, SSM_GROUPS, SSM_STATE))
        outs["pi"].append(hfi.reshape(nb, SSM_GROUPS, SSM_STATE))

        ms = jnp.repeat(mod[l, nb:], t_new, axis=0).reshape(1, ns * t_new, 3 * D_MODEL)
        shift, scale, gate = (ms[:, :, i * D_MODEL:(i + 1) * D_MODEL] for i in range(3))
        ts = ns * t_new
        u, zs, q, k_all, v_all, kb, vb, za = _in_projection(xs, shift, scale, g2, w_in_b[l], hsum, qg, kg,
                                                            ts, None, l, depth, skv)
        skv = (k_all, v_all)
        h0r = state_ssm_re[l].reshape(ns, 1, STATE_LANES)
        h0i = state_ssm_im[l].reshape(ns, 1, STATE_LANES)
        ysc, hfr, hfi = _s5_branch(u, h0r, h0i, *s5_ops, l, ns, t_new)
        pad = lambda a: jnp.pad(a.reshape(ns, t_new, ATT_WIDTH),
                                ((0, 0), (0, KEY_TILE - t_new), (0, 0))).reshape(ns * KEY_TILE, ATT_WIDTH)
        ya = _sb_sample(q, pad(kb), pad(vb), kc, vc, l, za, tri, ns, t_new)
        xs = _out_projection(xs, ysc, u, zs, ya, gate, *glu_w, w_out_b[l], ts, None)
        outs["sr"].append(hfr.reshape(ns, SSM_GROUPS, SSM_STATE))
        outs["si"].append(hfi.reshape(ns, SSM_GROUPS, SSM_STATE))

    st = lambda name: jnp.stack(outs[name])
    heads_p = lambda a: jnp.transpose(a, (0, 1, 4, 2, 3))
    heads_s = lambda a: a.reshape(depth, ns, t_new, N_HEADS, HEAD_DIM)
    return (xp.reshape(nb, seq_len, D_MODEL), xs.reshape(ns, t_new, D_MODEL),
            heads_p(pkv[0]), heads_p(pkv[1]), st("pr"), st("pi"),
            heads_s(skv[0]), heads_s(skv[1]), st("sr"), st("si"))
```

```python
import functools
import math
from typing import Any, NamedTuple

import jax
import jax.numpy as jnp
from jax import lax
from jax.experimental import pallas as pl
from jax.experimental.pallas import tpu as pltpu

F32 = jnp.float32
BF16 = jnp.bfloat16

D_MODEL = 1024
SSM_WIDTH = 512
SSM_GROUP = 16
SSM_GROUPS = 32
SSM_STATE = 64
STATE_LANES = SSM_GROUPS * SSM_STATE
ATT_WIDTH = 512
HEAD_DIM = 64
N_HEADS = 8
IN_WIDTH = 2 * SSM_WIDTH + 4 * ATT_WIDTH
EPS = 1e-6

LANES = 128
SUBLANES = 8
HEADS_PER_LANE_BLOCK = LANES // HEAD_DIM
GROUPS_PER_LANE_BLOCK = LANES // SSM_GROUP
N_LANE_BLOCKS = SSM_WIDTH // LANES
N_PAIRS = N_HEADS // HEADS_PER_LANE_BLOCK
S5_CHUNK = SUBLANES
KEY_TILE = 128
SB_Q_ROWS = 32
SB_STEP_BLOCKS = 16
LOG2_E = math.log2(math.e)
SB_LOG2_CUTOFF = -40.0 * LOG2_E
VMEM_LIMIT = 56 * 1024 * 1024


def _cparams(n_axes):
    return pltpu.CompilerParams(dimension_semantics=("arbitrary",) * n_axes,
                                vmem_limit_bytes=VMEM_LIMIT)


def _silu(x):
    return x * (1.0 / (1.0 + jnp.exp(-x)))


def _sigmoid(x):
    return 1.0 / (1.0 + jnp.exp(-x))


def _gelu_tanh(x):
    c = math.sqrt(2.0 / math.pi)
    return 0.5 * x * (1.0 + jnp.tanh(c * (x + 0.044715 * (x * x * x))))


def _mod_kernel(c_ref, w_ref, b_ref, o_ref):
    c = c_ref[...]
    a = _silu(c)
    o_ref[0] = jnp.dot(a, w_ref[0], preferred_element_type=F32,
                       precision=lax.Precision.HIGHEST) + b_ref[0]


def _modulation(c_all, w_mod, b_mod):
    depth = w_mod.shape[0]
    n = c_all.shape[0]
    nt = 3
    return pl.pallas_call(
        _mod_kernel,
        grid=(depth, nt),
        in_specs=[pl.BlockSpec((n, D_MODEL), lambda l, j: (0, 0)),
                  pl.BlockSpec((1, D_MODEL, D_MODEL), lambda l, j: (l, 0, j)),
                  pl.BlockSpec((1, 1, D_MODEL), lambda l, j: (l, 0, j))],
        out_specs=pl.BlockSpec((1, n, D_MODEL), lambda l, j: (l, 0, j)),
        out_shape=jax.ShapeDtypeStruct((depth, n, 3 * D_MODEL), F32),
        compiler_params=_cparams(2),
        name="modulation",
    )(c_all, w_mod, b_mod.reshape(depth, 1, 3 * D_MODEL))


def _disc_kernel(lr_ref, li_ref, ldt_ref, bre_ref, bim_ref, cre_ref, cim_ref,
                 pw8r_ref, pw8i_ref, lbr_ref, lbi_ref, clr_ref, cli_ref):
    lr = lr_ref[...]
    li = li_ref[...]
    dt = jnp.exp(ldt_ref[...])
    mag = jnp.exp(lr * dt)
    ang = li * dt
    ab_re = mag * jnp.cos(ang)
    ab_im = mag * jnp.sin(ang)
    den = lr * lr + li * li
    nr = ab_re - 1.0
    f_re = (nr * lr + ab_im * li) / den
    f_im = (ab_im * lr - nr * li) / den
    bre = bre_ref[...]
    bim = bim_ref[...]
    cre = cre_ref[...]
    cim = cim_ref[...]
    xr = f_re[:, None, :] * bre - f_im[:, None, :] * bim
    xi = f_re[:, None, :] * bim + f_im[:, None, :] * bre
    pr, pi_ = ab_re, ab_im
    for t in range(S5_CHUNK):
        lbr_ref[t] = xr
        lbi_ref[t] = xi
        clr_ref[t] = cre * pr[:, None, :] - cim * pi_[:, None, :]
        cli_ref[t] = cre * pi_[:, None, :] + cim * pr[:, None, :]
        if t + 1 < S5_CHUNK:
            xr, xi = (xr * ab_re[:, None, :] - xi * ab_im[:, None, :],
                      xr * ab_im[:, None, :] + xi * ab_re[:, None, :])
            pr, pi_ = pr * ab_re - pi_ * ab_im, pr * ab_im + pi_ * ab_re
    a8r, a8i = pr, pi_
    qr, qi = a8r, a8i
    for m in range(SUBLANES):
        pw8r_ref[m] = qr
        pw8i_ref[m] = qi
        qr, qi = qr * a8r - qi * a8i, qr * a8i + qi * a8r


def _place_kernel(lbr_ref, lbi_ref, clr_ref, cli_ref, cre_ref, cim_ref, bm_ref, ct_ref, kt_ref, kcat):
    gb = GROUPS_PER_LANE_BLOCK
    half = gb * SSM_STATE
    rep_p = (lax.broadcasted_iota(jnp.int32, (SSM_STATE, half), 0)
             == lax.broadcasted_iota(jnp.int32, (SSM_STATE, half), 1) % SSM_STATE).astype(BF16)
    rep_c = (lax.broadcasted_iota(jnp.int32, (SSM_GROUP, LANES), 0)
             == lax.broadcasted_iota(jnp.int32, (SSM_GROUP, LANES), 1) % SSM_GROUP).astype(BF16)
    same_p = (lax.broadcasted_iota(jnp.int32, (LANES, half), 0) // SSM_GROUP
              == lax.broadcasted_iota(jnp.int32, (LANES, half), 1) // SSM_STATE)
    same_c = (lax.broadcasted_iota(jnp.int32, (LANES, LANES), 0) // SSM_GROUP
              == lax.broadcasted_iota(jnp.int32, (LANES, LANES), 1) // SSM_GROUP)

    def place_states(x):
        tiled = jnp.dot(x.reshape(LANES, SSM_STATE).astype(BF16), rep_p, preferred_element_type=F32)
        return jnp.where(same_p, tiled, 0.0).astype(BF16)

    for s in range(S5_CHUNK):
        rows = slice(s * LANES, (s + 1) * LANES)
        bm_ref[0, rows, :half] = place_states(lbr_ref[S5_CHUNK - 1 - s])
        bm_ref[0, rows, half:] = place_states(lbi_ref[S5_CHUNK - 1 - s])
        ct_ref[0, rows, :half] = place_states(clr_ref[s])
        ct_ref[0, rows, half:] = place_states(-cli_ref[s])

    for g in range(gb):
        b_cat = jnp.concatenate([lbr_ref[:, g].reshape(LANES, SSM_STATE),
                                 lbi_ref[:, g].reshape(LANES, SSM_STATE)], axis=1)
        c_cat = jnp.concatenate([cre_ref[g], -cim_ref[g]], axis=1)
        kg = lax.dot_general(b_cat, c_cat, (((1,), (1,)), ((), ())), preferred_element_type=F32,
                             precision=lax.Precision.HIGHEST)
        for j in range(S5_CHUNK):
            kcat[j, g * SSM_GROUP:(g + 1) * SSM_GROUP, :] = kg[j * SSM_GROUP:(j + 1) * SSM_GROUP, :]
    zero = jnp.zeros((LANES, LANES), BF16)
    for j in range(S5_CHUNK):
        tiled = jnp.dot(kcat[j].astype(BF16), rep_c, preferred_element_type=F32)
        blk = jnp.where(same_c, tiled, 0.0).astype(BF16)
        for s in range(S5_CHUNK - j):
            kt_ref[0, s * LANES:(s + 1) * LANES, (s + j) * LANES:(s + j + 1) * LANES] = blk
        if j > 0:
            for t in range(S5_CHUNK - j):
                kt_ref[0, (t + j) * LANES:(t + j + 1) * LANES, t * LANES:(t + 1) * LANES] = zero


def _s5_operators(a_re, a_im, log_dt, b_re, b_im, c_re, c_im):
    depth = a_re.shape[0]
    rows = depth * SSM_GROUPS
    lr = a_re.reshape(rows, SSM_STATE)
    li = a_im.reshape(rows, SSM_STATE)
    ldt = jnp.broadcast_to(log_dt.reshape(rows, 1), (rows, SSM_STATE))
    bre = b_re.transpose(0, 1, 3, 2).reshape(rows, SSM_GROUP, SSM_STATE)
    bim = b_im.transpose(0, 1, 3, 2).reshape(rows, SSM_GROUP, SSM_STATE)
    cre = c_re.reshape(rows, SSM_GROUP, SSM_STATE)
    cim = c_im.reshape(rows, SSM_GROUP, SSM_STATE)
    pw_shape = jax.ShapeDtypeStruct((SUBLANES, rows, SSM_STATE), F32)
    op_shape = jax.ShapeDtypeStruct((S5_CHUNK, rows, SSM_GROUP, SSM_STATE), F32)
    pw8r, pw8i, lbr, lbi, clr, cli = pl.pallas_call(
        _disc_kernel,
        out_shape=(pw_shape, pw_shape, op_shape, op_shape, op_shape, op_shape),
        compiler_params=_cparams(0),
        name="s5_discretise",
    )(lr, li, ldt, bre, bim, cre, cim)
    gb = GROUPS_PER_LANE_BLOCK
    n_blk = rows // gb
    width = S5_CHUNK * LANES
    op_spec = pl.BlockSpec((S5_CHUNK, gb, SSM_GROUP, SSM_STATE), lambda i: (0, i, 0, 0))
    c_spec = pl.BlockSpec((gb, SSM_GROUP, SSM_STATE), lambda i: (i, 0, 0))
    mat_spec = pl.BlockSpec((1, width, width), lambda i: (i, 0, 0))
    mat_shape = jax.ShapeDtypeStruct((n_blk, width, width), BF16)
    bmat, ctm, ktoep = pl.pallas_call(
        _place_kernel,
        grid=(n_blk,),
        in_specs=[op_spec] * 4 + [c_spec] * 2,
        out_specs=[mat_spec] * 3,
        out_shape=(mat_shape,) * 3,
        scratch_shapes=[pltpu.VMEM((S5_CHUNK, LANES, SSM_GROUP), F32)],
        compiler_params=_cparams(1),
        name="s5_place_operators",
    )(lbr, lbi, clr, cli, cre, cim)

    pw8r = pw8r.reshape(SUBLANES, depth, STATE_LANES).transpose(1, 0, 2)
    pw8i = pw8i.reshape(SUBLANES, depth, STATE_LANES).transpose(1, 0, 2)
    row = jnp.arange(SUBLANES)[None, :, None]

    def step_mult(pw):
        return jnp.stack([jnp.where(row >= k, pw[:, k - 1][:, None, :], 0.0) for k in (1, 2, 4)], axis=1)

    return bmat, ktoep, ctm, step_mult(pw8r), step_mult(pw8i), pw8r, pw8i


def _heads_major(x, position_minor):
    if position_minor:
        return x.T.reshape(N_HEADS, HEAD_DIM, x.shape[0])
    heads = jnp.stack([x[:, h * HEAD_DIM:(h + 1) * HEAD_DIM] for h in range(N_HEADS)], axis=0)
    return pltpu.einshape("htd->thd", heads)


def _inproj_kernel(x_ref, shift_ref, scale_ref, g_ref, w_ref, hsum_ref, qg_ref, kg_ref, *rest,
                   position_minor):
    u_ref, zs_ref, q_ref, k_ref, v_ref, kb_ref, vb_ref, za_ref = rest[-8:]
    x = x_ref[...]
    ms = jnp.mean(x * x, axis=-1, keepdims=True)
    h = x * lax.rsqrt(ms + EPS) * g_ref[...]
    h = h * (1.0 + scale_ref[0]) + shift_ref[0]
    hb = h.astype(BF16)

    def proj(c):
        return jnp.dot(hb, w_ref[:, c * SSM_WIDTH:(c + 1) * SSM_WIDTH], preferred_element_type=F32)

    def head_norm(p, g):
        ss = jnp.dot((p * p).astype(BF16), hsum_ref[...], preferred_element_type=F32)
        return p * lax.rsqrt(ss * (1.0 / HEAD_DIM) + EPS) * g

    u_ref[...] = proj(0)
    zs_ref[...] = _silu(proj(1)).astype(BF16)
    q = head_norm(proj(2), qg_ref[...])
    q_ref[...] = (q * (HEAD_DIM ** -0.5 * LOG2_E)).astype(BF16)
    k = head_norm(proj(3), kg_ref[...])
    k_ref[...] = _heads_major(k, position_minor).reshape(k_ref.shape)
    kb_ref[...] = k.astype(BF16)
    v = proj(4)
    v_ref[...] = _heads_major(v, position_minor).reshape(v_ref.shape)
    vb_ref[...] = v.astype(BF16)
    za_ref[...] = _silu(proj(5)).astype(BF16)


def _in_projection(x2d, shift, scale, norm_g, w_in_b, hsum, q_g, k_g, tm, rows_per_seq,
                   layer, depth, kv_all):
    n = x2d.shape[0]
    nt = n // tm
    if rows_per_seq is None:
        mod_spec = pl.BlockSpec((1, tm, D_MODEL), lambda i: (0, i, 0))
    else:
        tps = rows_per_seq // tm
        mod_spec = pl.BlockSpec((1, 1, D_MODEL), lambda i: (i // tps, 0, 0))
    row_spec = lambda w: pl.BlockSpec((tm, w), lambda i: (i, 0))
    full = lambda a: pl.BlockSpec(a.shape, lambda i: (0,) * a.ndim)
    bf = jax.ShapeDtypeStruct((n, SSM_WIDTH), BF16)
    position_minor = rows_per_seq is not None
    if position_minor:
        f3 = jax.ShapeDtypeStruct((depth, n // rows_per_seq, N_HEADS, HEAD_DIM, rows_per_seq), F32)
        kv_spec = pl.BlockSpec((1, 1, N_HEADS, HEAD_DIM, tm), lambda i: (layer, i // tps, 0, 0, i % tps))
    else:
        f3 = jax.ShapeDtypeStruct((depth, n, N_HEADS, HEAD_DIM), F32)
        kv_spec = pl.BlockSpec((1, tm, N_HEADS, HEAD_DIM), lambda i: (layer, i, 0, 0))
    assert kv_all[0].shape == f3.shape and kv_all[1].shape == f3.shape
    in_specs = [row_spec(D_MODEL), mod_spec, mod_spec, full(norm_g), full(w_in_b), full(hsum),
                full(q_g), full(k_g)] + [pl.BlockSpec(memory_space=pl.ANY)] * 2
    args = [x2d, shift, scale, norm_g, w_in_b, hsum, q_g, k_g, *kv_all]
    aliases = {len(args) - 2: 3, len(args) - 1: 4}
    return pl.pallas_call(
        functools.partial(_inproj_kernel, position_minor=position_minor),
        grid=(nt,),
        in_specs=in_specs,
        out_specs=[row_spec(SSM_WIDTH)] * 3 + [kv_spec] * 2 + [row_spec(SSM_WIDTH)] * 3,
        out_shape=(jax.ShapeDtypeStruct((n, SSM_WIDTH), F32), bf, bf, f3, f3, bf, bf, bf),
        input_output_aliases=aliases,
        compiler_params=_cparams(1),
        name="in_proj",
    )(*args)


def _s5_kernel(u_ref, h0r_ref, h0i_ref, bm_ref, kt_ref, ct_ref, mr_ref, mi_ref, pr_ref, pi_ref,
               y_ref, hfr_ref, hfi_ref,
               uflat, wre, wim, hpr, hpi, *, n_seq, lane_chunk):
    rows = u_ref.shape[0]
    n_chunks = rows // S5_CHUNK
    seq_chunks = n_chunks // n_seq
    half = GROUPS_PER_LANE_BLOCK * SSM_STATE

    def token_rows(c, t):
        if n_seq == 1:
            return pl.ds(t, n_chunks, stride=S5_CHUNK)
        return pl.ds(c * S5_CHUNK + t, n_seq, stride=seq_chunks * S5_CHUNK)

    fold_blocks = [(0, slice(None))] if n_seq == 1 else [(c, slice(c * n_seq, (c + 1) * n_seq))
                                                        for c in range(seq_chunks)]
    for c, frows in fold_blocks:
        for t in range(S5_CHUNK):
            uflat[frows, t * LANES:(t + 1) * LANES] = u_ref[token_rows(c, t), :].astype(BF16)
    w = jnp.dot(uflat[...], bm_ref[...], preferred_element_type=F32)
    wre[...] = w[:, :half]
    wim[...] = w[:, half:]

    if n_seq == 1:
        hpr[0:SUBLANES, :] = jnp.broadcast_to(h0r_ref[0], (SUBLANES, half))
        hpi[0:SUBLANES, :] = jnp.broadcast_to(h0i_ref[0], (SUBLANES, half))
        for ch in range(half // lane_chunk):
            sl = slice(ch * lane_chunk, (ch + 1) * lane_chunk)
            prc = pr_ref[:, sl]
            pic = pi_ref[:, sl]

            def group_body(r, carry):
                cr, ci = carry
                row = pl.multiple_of(r * SUBLANES, SUBLANES)
                vr = wre[pl.ds(row, SUBLANES), sl]
                vi = wim[pl.ds(row, SUBLANES), sl]
                for idx, k in enumerate((1, 2, 4)):
                    mr = mr_ref[idx, :, sl]
                    mi = mi_ref[idx, :, sl]
                    sr = pltpu.roll(vr, k, 0)
                    si = pltpu.roll(vi, k, 0)
                    vr, vi = vr + (mr * sr - mi * si), vi + (mr * si + mi * sr)
                hr = vr + (prc * cr - pic * ci)
                hi = vi + (prc * ci + pic * cr)
                hpr[pl.ds(row + SUBLANES, SUBLANES), sl] = hr
                hpi[pl.ds(row + SUBLANES, SUBLANES), sl] = hi
                last = SUBLANES - 1
                return (jnp.broadcast_to(hr[last:last + 1], hr.shape),
                        jnp.broadcast_to(hi[last:last + 1], hi.shape))

            cr, ci = lax.fori_loop(0, n_chunks // SUBLANES, group_body,
                                   (hpr[0:SUBLANES, sl], hpi[0:SUBLANES, sl]), unroll=2)
            hfr_ref[0, :, sl] = cr[0:1]
            hfi_ref[0, :, sl] = ci[0:1]
        h_prev_r = hpr[SUBLANES - 1:SUBLANES - 1 + n_chunks, :]
        h_prev_i = hpi[SUBLANES - 1:SUBLANES - 1 + n_chunks, :]
    else:
        a8r = pr_ref[0:1, :]
        a8i = pi_ref[0:1, :]
        hr = h0r_ref[...].reshape(n_seq, half)
        hi = h0i_ref[...].reshape(n_seq, half)
        for c, frows in fold_blocks:
            hpr[frows, :] = hr
            hpi[frows, :] = hi
            wr = wre[frows, :]
            wi = wim[frows, :]
            hr, hi = a8r * hr - a8i * hi + wr, a8r * hi + a8i * hr + wi
        hfr_ref[...] = hr.reshape(hfr_ref.shape)
        hfi_ref[...] = hi.reshape(hfi_ref.shape)
        h_prev_r = hpr[0:n_chunks, :]
        h_prev_i = hpi[0:n_chunks, :]

    h_prev = jnp.concatenate([h_prev_r.astype(BF16), h_prev_i.astype(BF16)], axis=1)
    yf = jnp.dot(uflat[...], kt_ref[...], preferred_element_type=F32)
    yf = yf + lax.dot_general(h_prev, ct_ref[...], (((1,), (1,)), ((), ())), preferred_element_type=F32)
    for c, frows in fold_blocks:
        for t in range(S5_CHUNK):
            y_ref[token_rows(c, t), :] = yf[frows, t * LANES:(t + 1) * LANES]


def _s5_branch(u, h0_re, h0_im, bmat, ktoep, ctm, mr, mi, pr, pi_, layer, n_seq, seq_rows):
    n = u.shape[0]
    tile = n_seq * seq_rows
    n_total = h0_re.shape[0]
    n_chunks = tile // S5_CHUNK
    half = GROUPS_PER_LANE_BLOCK * SSM_STATE
    row_spec = pl.BlockSpec((tile, LANES), lambda lb, b: (b, lb))
    st_spec = pl.BlockSpec((n_seq, 1, half), lambda lb, b: (b, 0, lb))
    hp_rows = n_chunks + SUBLANES if n_seq == 1 else n_chunks
    mat_spec = pl.BlockSpec((None,) + bmat.shape[1:], lambda lb, b: (layer * N_LANE_BLOCKS + lb, 0, 0))
    kern = functools.partial(_s5_kernel, n_seq=n_seq, lane_chunk=256)
    return pl.pallas_call(
        kern,
        grid=(N_LANE_BLOCKS, n // tile),
        in_specs=[row_spec, st_spec, st_spec,
                  mat_spec, mat_spec, mat_spec,
                  pl.BlockSpec((None, 3, SUBLANES, half), lambda lb, b: (layer, 0, 0, lb)),
                  pl.BlockSpec((None, 3, SUBLANES, half), lambda lb, b: (layer, 0, 0, lb)),
                  pl.BlockSpec((None, SUBLANES, half), lambda lb, b: (layer, 0, lb)),
                  pl.BlockSpec((None, SUBLANES, half), lambda lb, b: (layer, 0, lb))],
        out_specs=[row_spec, st_spec, st_spec],
        out_shape=(jax.ShapeDtypeStruct((n, SSM_WIDTH), F32),
                   jax.ShapeDtypeStruct((n_total, 1, STATE_LANES), F32),
                   jax.ShapeDtypeStruct((n_total, 1, STATE_LANES), F32)),
        scratch_shapes=[pltpu.VMEM((n_chunks, S5_CHUNK * LANES), BF16),
                        pltpu.VMEM((n_chunks, half), F32), pltpu.VMEM((n_chunks, half), F32),
                        pltpu.VMEM((hp_rows, half), F32), pltpu.VMEM((hp_rows, half), F32)],
        compiler_params=_cparams(2),
        name="s5_branch",
    )(u, h0_re, h0_im, bmat, ktoep, ctm, mr, mi, pr, pi_)


def _sb_mask_queries(q_rows, qm_ref, tq):
    lane = lax.broadcasted_iota(jnp.int32, (tq, LANES), 1)
    for j in range(N_PAIRS):
        qpair = q_rows(j).astype(F32)
        for hh in range(HEADS_PER_LANE_BLOCK):
            h = j * HEADS_PER_LANE_BLOCK + hh
            in_head = (lane >= hh * HEAD_DIM) & (lane < (hh + 1) * HEAD_DIM)
            qm_ref[h * tq:(h + 1) * tq, :] = jnp.where(in_head, qpair, 0.0).astype(BF16)


class _SbTile(NamedTuple):
    tq: int
    k_tile: Any
    v_tile: Any
    keys_minor: bool
    valid: Any
    first: bool
    qm: Any
    lb: Any
    hl: Any
    a: Any
    car: Any
    acc: Any


def _sb_scores(t):
    pair = HEADS_PER_LANE_BLOCK * t.tq
    valid2 = None if t.valid is None else jnp.concatenate([t.valid] * HEADS_PER_LANE_BLOCK, axis=0)
    for j in range(N_PAIRS):
        rows = slice(j * pair, (j + 1) * pair)
        contract_k = 0 if t.keys_minor else 1
        z = lax.dot_general(t.qm[rows, :], t.k_tile(j), (((1,), (contract_k,)), ((), ())),
                            preferred_element_type=F32)
        sp = jnp.maximum(z, 0.0) + jnp.log2(1.0 + jnp.exp2(-jnp.abs(z)))
        t.lb[rows, :] = z - sp
        if valid2 is not None:
            sp = jnp.where(valid2, sp, 0.0)
        hi = sp.astype(BF16)
        t.hl[rows, :KEY_TILE] = hi
        t.hl[rows, KEY_TILE:] = (sp - hi.astype(F32)).astype(BF16)


def _sb_weights(t, tri_ref):
    cs = jnp.dot(t.hl[...], tri_ref[...], preferred_element_type=F32)
    cmax = None
    for h in range(N_HEADS):
        rows = slice(h * t.tq, (h + 1) * t.tq)
        after = cs[rows, :KEY_TILE]
        total = cs[rows, KEY_TILE:]
        if not t.first:
            carry = t.car[rows, :]
            after = after + carry
            total = total + carry
        a = jnp.exp2(t.lb[rows, :] + after)
        if t.valid is not None:
            a = jnp.where(t.valid, a, 0.0)
        t.a[rows, :] = a.astype(BF16)
        t.car[rows, :] = total
        cmax = total if cmax is None else jnp.maximum(cmax, total)
    return jnp.max(cmax)


def _sb_output(t):
    pair = HEADS_PER_LANE_BLOCK * t.tq
    lane = lax.broadcasted_iota(jnp.int32, (t.tq, LANES), 1)
    for j in range(N_PAIRS):
        contract_v = 1 if t.keys_minor else 0
        pv = lax.dot_general(t.a[j * pair:(j + 1) * pair, :], t.v_tile(j), (((1,), (contract_v,)), ((), ())),
                             preferred_element_type=F32)
        out = jnp.where(lane < HEAD_DIM, pv[:t.tq], pv[t.tq:])
        if t.first:
            t.acc[:, j * LANES:(j + 1) * LANES] = out
        else:
            t.acc[:, j * LANES:(j + 1) * LANES] += out


def _sb_tiles(tiles, tri_ref):
    for t in tiles:
        _sb_scores(t)
    ms = [_sb_weights(t, tri_ref) for t in tiles]
    for t in tiles:
        _sb_output(t)
    return ms


def _sb_scratch(n_blocks, tq):
    rows = N_HEADS * tq
    return [pltpu.VMEM((n_blocks, rows, LANES), BF16),
            pltpu.VMEM((n_blocks, rows, LANES), F32),
            pltpu.VMEM((n_blocks, rows, 2 * KEY_TILE), BF16),
            pltpu.VMEM((n_blocks, rows, LANES), BF16),
            pltpu.VMEM((n_blocks, rows, LANES), F32),
            pltpu.VMEM((n_blocks, tq, ATT_WIDTH), F32),
            pltpu.SMEM((n_blocks,), F32)]


def _sb_prompt_kernel(q_ref, k_ref, v_ref, za_ref, tri_ref, o_ref,
                      qm_ref, lb_ref, hl_ref, a_ref, car_ref, acc_ref, m_ref):
    tq = SB_Q_ROWS
    n_blocks = q_ref.shape[0] // tq
    step_row0 = pl.program_id(1) * (n_blocks * tq)
    row = lax.broadcasted_iota(jnp.int32, (tq, KEY_TILE), 0)
    col = lax.broadcasted_iota(jnp.int32, (tq, KEY_TILE), 1)

    def tiles(ref, start):
        return lambda j: ref[pl.ds(start, KEY_TILE), j * LANES:(j + 1) * LANES]

    def band_start(s):
        return pl.multiple_of(jnp.maximum(step_row0 + (s + 1) * tq - KEY_TILE, 0), tq)

    def tile(s, lo, valid, first):
        return _SbTile(tq, tiles(k_ref, lo), tiles(v_ref, lo), False, valid, first, qm_ref.at[s], lb_ref.at[s],
                       hl_ref.at[s], a_ref.at[s], car_ref.at[s], acc_ref.at[s])

    band = []
    for s in range(n_blocks):
        lo = band_start(s)
        _sb_mask_queries(lambda j: q_ref[s * tq:(s + 1) * tq, j * LANES:(j + 1) * LANES], qm_ref.at[s], tq)
        valid = (col - row) < (step_row0 + s * tq - lo)
        band.append(tile(s, lo, valid, True))
    for s, m in enumerate(_sb_tiles(band, tri_ref)):
        m_ref[s] = m

    def more_tiles(s, _):
        def cond(state):
            hi, m = state
            return (hi > 0) & (m > SB_LOG2_CUTOFF)

        def body(state):
            hi, _ = state
            lo = pl.multiple_of(jnp.maximum(hi - KEY_TILE, 0), tq)
            valid = col < (hi - lo)
            (m,) = _sb_tiles([tile(s, lo, valid, False)], tri_ref)
            return lo, m

        lax.while_loop(cond, body, (band_start(s), m_ref[s]))
        return 0

    lax.fori_loop(0, n_blocks, more_tiles, 0)
    for s in range(n_blocks):
        rows = slice(s * tq, (s + 1) * tq)
        o_ref[rows, :] = (acc_ref[s] * za_ref[rows, :].astype(F32)).astype(BF16)


def _sb_prompt(q, kb, vb, za, tri, n_batch, seq_len):
    step_rows = SB_STEP_BLOCKS * SB_Q_ROWS
    nq = seq_len // step_rows
    row_spec = pl.BlockSpec((step_rows, ATT_WIDTH), lambda b, i: (b * nq + i, 0))
    seq_spec = pl.BlockSpec((seq_len, ATT_WIDTH), lambda b, i: (b, 0))
    return pl.pallas_call(
        _sb_prompt_kernel,
        grid=(n_batch, nq),
        in_specs=[row_spec, seq_spec, seq_spec, row_spec, pl.BlockSpec(tri.shape, lambda b, i: (0, 0))],
        out_specs=row_spec,
        out_shape=jax.ShapeDtypeStruct(q.shape, BF16),
        scratch_shapes=_sb_scratch(SB_STEP_BLOCKS, SB_Q_ROWS),
        compiler_params=_cparams(2),
        name="sb_prompt",
    )(q, kb, vb, za, tri)


def _sb_sample_kernel(q_ref, kn_ref, vn_ref, kc_ref, vc_ref, za_ref, tri_ref, o_ref,
                      qm_ref, lb_ref, hl_ref, a_ref, car_ref, acc_ref, m_ref):
    tq = q_ref.shape[0]
    n_past = kc_ref.shape[-1] // KEY_TILE
    _sb_mask_queries(lambda j: q_ref[:, j * LANES:(j + 1) * LANES], qm_ref.at[0], tq)
    row = lax.broadcasted_iota(jnp.int32, (tq, KEY_TILE), 0)
    col = lax.broadcasted_iota(jnp.int32, (tq, KEY_TILE), 1)
    scratch = (qm_ref.at[0], lb_ref.at[0], hl_ref.at[0], a_ref.at[0], car_ref.at[0], acc_ref.at[0])

    new_tile = lambda ref: (lambda j: ref[:, j * LANES:(j + 1) * LANES])
    (m0,) = _sb_tiles([_SbTile(tq, new_tile(kn_ref), new_tile(vn_ref), False, col < row, True, *scratch)],
                      tri_ref)

    def past_tile(ref, t):
        start = pl.multiple_of(t * KEY_TILE, KEY_TILE)
        return lambda j: ref[0, 0, j * HEADS_PER_LANE_BLOCK:(j + 1) * HEADS_PER_LANE_BLOCK, :,
                             pl.ds(start, KEY_TILE)].reshape(LANES, KEY_TILE).astype(BF16)

    def cond(state):
        t, m = state
        return (t >= 0) & (m > SB_LOG2_CUTOFF)

    def body(state):
        t, _ = state
        (m,) = _sb_tiles([_SbTile(tq, past_tile(kc_ref, t), past_tile(vc_ref, t), True, None, False,
                                  *scratch)], tri_ref)
        return t - 1, m

    lax.while_loop(cond, body, (n_past - 1, m0))
    o_ref[...] = (acc_ref[0] * za_ref[...].astype(F32)).astype(BF16)


def _sb_sample(q, k_new, v_new, k_past, v_past, layer, za, tri, n_batch, t_new):
    past = k_past.shape[-1]
    row_spec = pl.BlockSpec((t_new, ATT_WIDTH), lambda b: (b, 0))
    new_spec = pl.BlockSpec((KEY_TILE, ATT_WIDTH), lambda b: (b, 0))
    past_spec = pl.BlockSpec((1, 1, N_HEADS, HEAD_DIM, past), lambda b: (layer, b, 0, 0, 0))
    return pl.pallas_call(
        _sb_sample_kernel,
        grid=(n_batch,),
        in_specs=[row_spec, new_spec, new_spec, past_spec, past_spec, row_spec,
                  pl.BlockSpec(tri.shape, lambda b: (0, 0))],
        out_specs=row_spec,
        out_shape=jax.ShapeDtypeStruct(q.shape, BF16),
        scratch_shapes=_sb_scratch(1, t_new),
        compiler_params=_cparams(1),
        name="sb_sample",
    )(q, k_new, v_new, k_past, v_past, za, tri)


def _outproj_kernel(x_ref, ysc_ref, u_ref, zs_ref, ya_ref, gate_ref, d_ref, wglu_ref, bglu_ref, w_ref, o_ref):
    y = ysc_ref[...] + d_ref[...] * u_ref[...]
    g = _gelu_tanh(y)
    glu = _sigmoid(jnp.dot(g.astype(BF16), wglu_ref[...], preferred_element_type=F32) + bglu_ref[...])
    ys = (g * glu * zs_ref[...].astype(F32)).astype(BF16)
    mix = jnp.dot(ys, w_ref[:SSM_WIDTH, :], preferred_element_type=F32)
    mix = mix + jnp.dot(ya_ref[...], w_ref[SSM_WIDTH:, :], preferred_element_type=F32)
    o_ref[...] = x_ref[...] + gate_ref[0] * mix


def _out_projection(x2d, ysc, u, zs, ya, gate, d_skip, w_glu_b, b_glu, w_out_b, tm, rows_per_seq):
    n = x2d.shape[0]
    if rows_per_seq is None:
        mod_spec = pl.BlockSpec((1, tm, D_MODEL), lambda i: (0, i, 0))
    else:
        tps = rows_per_seq // tm
        mod_spec = pl.BlockSpec((1, 1, D_MODEL), lambda i: (i // tps, 0, 0))
    row_spec = lambda w: pl.BlockSpec((tm, w), lambda i: (i, 0))
    full = lambda a: pl.BlockSpec(a.shape, lambda i: (0,) * a.ndim)
    return pl.pallas_call(
        _outproj_kernel,
        grid=(n // tm,),
        in_specs=[row_spec(D_MODEL), row_spec(SSM_WIDTH), row_spec(SSM_WIDTH), row_spec(SSM_WIDTH),
                  row_spec(ATT_WIDTH), mod_spec, full(d_skip), full(w_glu_b), full(b_glu), full(w_out_b)],
        out_specs=row_spec(D_MODEL),
        out_shape=jax.ShapeDtypeStruct(x2d.shape, F32),
        compiler_params=_cparams(1),
        name="out_proj",
    )(x2d, ysc, u, zs, ya, gate, d_skip, w_glu_b, b_glu, w_out_b)


def _sb_sum_matrix():
    j = jnp.arange(2 * KEY_TILE)[:, None] % KEY_TILE
    s = jnp.arange(2 * KEY_TILE)[None, :]
    return -((s >= KEY_TILE) | (j > s)).astype(BF16)


def _head_sum_matrix():
    a = jnp.arange(ATT_WIDTH)
    return (a[:, None] // HEAD_DIM == a[None, :] // HEAD_DIM).astype(BF16)


def kernel(x_prompt, x_sample, c_prompt, c_sample, cache_k, cache_v, state_ssm_re, state_ssm_im, norm_g, w_mod, b_mod, w_in, ssm_a_re, ssm_a_im, ssm_log_dt, ssm_b_re, ssm_b_im, ssm_c_re, ssm_c_im, ssm_d, w_glu, b_glu, q_norm_g, k_norm_g, w_out):
    depth = w_in.shape[0]
    nb, seq_len, _ = x_prompt.shape
    ns, t_new, _ = x_sample.shape
    tm = 512

    mod = _modulation(jnp.concatenate([c_prompt, c_sample], axis=0), w_mod, b_mod)
    s5_ops = _s5_operators(ssm_a_re, ssm_a_im, ssm_log_dt, ssm_b_re, ssm_b_im, ssm_c_re, ssm_c_im)

    w_in_b = w_in.astype(BF16)
    w_out_b = w_out.astype(BF16)
    w_glu_b = w_glu.astype(BF16)
    tri = _sb_sum_matrix()
    hsum = _head_sum_matrix()

    xp = x_prompt.reshape(nb * seq_len, D_MODEL)
    xs = x_sample.reshape(ns * t_new, D_MODEL)
    zero_state = jnp.zeros((nb, 1, STATE_LANES), F32)
    kc = jnp.transpose(cache_k, (0, 1, 3, 4, 2))
    vc = jnp.transpose(cache_v, (0, 1, 3, 4, 2))
    outs = {name: [] for name in ("pr", "pi", "sr", "si")}
    pkv = tuple(jnp.zeros((depth, nb, N_HEADS, HEAD_DIM, seq_len), F32) for _ in range(2))
    skv = tuple(jnp.zeros((depth, ns * t_new, N_HEADS, HEAD_DIM), F32) for _ in range(2))
    for l in range(depth):
        g2 = norm_g[l].reshape(1, D_MODEL)
        qg = jnp.tile(q_norm_g[l], N_HEADS).reshape(1, ATT_WIDTH)
        kg = jnp.tile(k_norm_g[l], N_HEADS).reshape(1, ATT_WIDTH)
        d2 = ssm_d[l].reshape(1, SSM_WIDTH)
        bg2 = b_glu[l].reshape(1, SSM_WIDTH)
        glu_w = (d2, w_glu_b[l], bg2)

        mp = mod[l, :nb].reshape(nb, 1, 3 * D_MODEL)
        shift, scale, gate = (mp[:, :, i * D_MODEL:(i + 1) * D_MODEL] for i in range(3))
        u, zs, q, k_all, v_all, kb, vb, za = _in_projection(xp, shift, scale, g2, w_in_b[l], hsum, qg, kg,
                                                            tm, seq_len, l, depth, pkv)
        pkv = (k_all, v_all)
        ysc, hfr, hfi = _s5_branch(u, zero_state, zero_state, *s5_ops, l, 1, seq_len)
        ya = _sb_prompt(q, kb, vb, za, tri, nb, seq_len)
        xp = _out_projection(xp, ysc, u, zs, ya, gate, *glu_w, w_out_b[l], tm, seq_len)
        outs["pr"].append(hfr.reshape(nb, SSM_GROUPS, SSM_STATE))
        outs["pi"].append(hfi.reshape(nb, SSM_GROUPS, SSM_STATE))

        ms = jnp.repeat(mod[l, nb:], t_new, axis=0).reshape(1, ns * t_new, 3 * D_MODEL)
        shift, scale, gate = (ms[:, :, i * D_MODEL:(i + 1) * D_MODEL] for i in range(3))
        ts = ns * t_new
        u, zs, q, k_all, v_all, kb, vb, za = _in_projection(xs, shift, scale, g2, w_in_b[l], hsum, qg, kg,
                                                            ts, None, l, depth, skv)
        skv = (k_all, v_all)
        h0r = state_ssm_re[l].reshape(ns, 1, STATE_LANES)
        h0i = state_ssm_im[l].reshape(ns, 1, STATE_LANES)
        ysc, hfr, hfi = _s5_branch(u, h0r, h0i, *s5_ops, l, ns, t_new)
        pad = lambda a: jnp.pad(a.reshape(ns, t_new, ATT_WIDTH),
                                ((0, 0), (0, KEY_TILE - t_new), (0, 0))).reshape(ns * KEY_TILE, ATT_WIDTH)
        ya = _sb_sample(q, pad(kb), pad(vb), kc, vc, l, za, tri, ns, t_new)
        xs = _out_projection(xs, ysc, u, zs, ya, gate, *glu_w, w_out_b[l], ts, None)
        outs["sr"].append(hfr.reshape(ns, SSM_GROUPS, SSM_STATE))
        outs["si"].append(hfi.reshape(ns, SSM_GROUPS, SSM_STATE))

    st = lambda name: jnp.stack(outs[name])
    heads_p = lambda a: jnp.transpose(a, (0, 1, 4, 2, 3))
    heads_s = lambda a: a.reshape(depth, ns, t_new, N_HEADS, HEAD_DIM)
    return (xp.reshape(nb, seq_len, D_MODEL), xs.reshape(ns, t_new, D_MODEL),
            heads_p(pkv[0]), heads_p(pkv[1]), st("pr"), st("pi"),
            heads_s(skv[0]), heads_s(skv[1]), st("sr"), st("si"))
```

```python
import functools
import math
from typing import Any, NamedTuple

import jax
import jax.numpy as jnp
from jax import lax
from jax.experimental import pallas as pl
from jax.experimental.pallas import tpu as pltpu

F32 = jnp.float32
BF16 = jnp.bfloat16

D_MODEL = 1024
SSM_WIDTH = 512
SSM_GROUP = 16
SSM_GROUPS = 32
SSM_STATE = 64
STATE_LANES = SSM_GROUPS * SSM_STATE
ATT_WIDTH = 512
HEAD_DIM = 64
N_HEADS = 8
IN_WIDTH = 2 * SSM_WIDTH + 4 * ATT_WIDTH
EPS = 1e-6

LANES = 128
SUBLANES = 8
HEADS_PER_LANE_BLOCK = LANES // HEAD_DIM
GROUPS_PER_LANE_BLOCK = LANES // SSM_GROUP
N_LANE_BLOCKS = SSM_WIDTH // LANES
N_PAIRS = N_HEADS // HEADS_PER_LANE_BLOCK
S5_CHUNK = SUBLANES
KEY_TILE = 128
SB_Q_ROWS = 32
SB_STEP_BLOCKS = 16
LOG2_E = math.log2(math.e)
SB_LOG2_CUTOFF = -40.0 * LOG2_E
VMEM_LIMIT = 56 * 1024 * 1024


def _cparams(n_axes):
    return pltpu.CompilerParams(dimension_semantics=("arbitrary",) * n_axes,
                                vmem_limit_bytes=VMEM_LIMIT)


def _silu(x):
    return x * (1.0 / (1.0 + jnp.exp(-x)))


def _sigmoid(x):
    return 1.0 / (1.0 + jnp.exp(-x))


def _gelu_tanh(x):
    c = math.sqrt(2.0 / math.pi)
    return 0.5 * x * (1.0 + jnp.tanh(c * (x + 0.044715 * (x * x * x))))


def _mod_kernel(c_ref, w_ref, b_ref, o_ref):
    c = c_ref[...]
    a = _silu(c)
    o_ref[0] = jnp.dot(a, w_ref[0], preferred_element_type=F32,
                       precision=lax.Precision.HIGHEST) + b_ref[0]


def _modulation(c_all, w_mod, b_mod):
    depth = w_mod.shape[0]
    n = c_all.shape[0]
    nt = 3
    return pl.pallas_call(
        _mod_kernel,
        grid=(depth, nt),
        in_specs=[pl.BlockSpec((n, D_MODEL), lambda l, j: (0, 0)),
                  pl.BlockSpec((1, D_MODEL, D_MODEL), lambda l, j: (l, 0, j)),
                  pl.BlockSpec((1, 1, D_MODEL), lambda l, j: (l, 0, j))],
        out_specs=pl.BlockSpec((1, n, D_MODEL), lambda l, j: (l, 0, j)),
        out_shape=jax.ShapeDtypeStruct((depth, n, 3 * D_MODEL), F32),
        compiler_params=_cparams(2),
        name="modulation",
    )(c_all, w_mod, b_mod.reshape(depth, 1, 3 * D_MODEL))


def _disc_kernel(lr_ref, li_ref, ldt_ref, bre_ref, bim_ref, cre_ref, cim_ref,
                 pw8r_ref, pw8i_ref, lbr_ref, lbi_ref, clr_ref, cli_ref):
    lr = lr_ref[...]
    li = li_ref[...]
    dt = jnp.exp(ldt_ref[...])
    mag = jnp.exp(lr * dt)
    ang = li * dt
    ab_re = mag * jnp.cos(ang)
    ab_im = mag * jnp.sin(ang)
    den = lr * lr + li * li
    nr = ab_re - 1.0
    f_re = (nr * lr + ab_im * li) / den
    f_im = (ab_im * lr - nr * li) / den
    bre = bre_ref[...]
    bim = bim_ref[...]
    cre = cre_ref[...]
    cim = cim_ref[...]
    xr = f_re[:, None, :] * bre - f_im[:, None, :] * bim
    xi = f_re[:, None, :] * bim + f_im[:, None, :] * bre
    pr, pi_ = ab_re, ab_im
    for t in range(S5_CHUNK):
        lbr_ref[t] = xr
        lbi_ref[t] = xi
        clr_ref[t] = cre * pr[:, None, :] - cim * pi_[:, None, :]
        cli_ref[t] = cre * pi_[:, None, :] + cim * pr[:, None, :]
        if t + 1 < S5_CHUNK:
            xr, xi = (xr * ab_re[:, None, :] - xi * ab_im[:, None, :],
                      xr * ab_im[:, None, :] + xi * ab_re[:, None, :])
            pr, pi_ = pr * ab_re - pi_ * ab_im, pr * ab_im + pi_ * ab_re
    a8r, a8i = pr, pi_
    qr, qi = a8r, a8i
    for m in range(SUBLANES):
        pw8r_ref[m] = qr
        pw8i_ref[m] = qi
        qr, qi = qr * a8r - qi * a8i, qr * a8i + qi * a8r


def _place_kernel(lbr_ref, lbi_ref, clr_ref, cli_ref, cre_ref, cim_ref, bm_ref, ct_ref, kt_ref, kcat):
    gb = GROUPS_PER_LANE_BLOCK
    half = gb * SSM_STATE
    rep_p = (lax.broadcasted_iota(jnp.int32, (SSM_STATE, half), 0)
             == lax.broadcasted_iota(jnp.int32, (SSM_STATE, half), 1) % SSM_STATE).astype(BF16)
    rep_c = (lax.broadcasted_iota(jnp.int32, (SSM_GROUP, LANES), 0)
             == lax.broadcasted_iota(jnp.int32, (SSM_GROUP, LANES), 1) % SSM_GROUP).astype(BF16)
    same_p = (lax.broadcasted_iota(jnp.int32, (LANES, half), 0) // SSM_GROUP
              == lax.broadcasted_iota(jnp.int32, (LANES, half), 1) // SSM_STATE)
    same_c = (lax.broadcasted_iota(jnp.int32, (LANES, LANES), 0) // SSM_GROUP
              == lax.broadcasted_iota(jnp.int32, (LANES, LANES), 1) // SSM_GROUP)

    def place_states(x):
        tiled = jnp.dot(x.reshape(LANES, SSM_STATE).astype(BF16), rep_p, preferred_element_type=F32)
        return jnp.where(same_p, tiled, 0.0).astype(BF16)

    for s in range(S5_CHUNK):
        rows = slice(s * LANES, (s + 1) * LANES)
        bm_ref[0, rows, :half] = place_states(lbr_ref[S5_CHUNK - 1 - s])
        bm_ref[0, rows, half:] = place_states(lbi_ref[S5_CHUNK - 1 - s])
        ct_ref[0, rows, :half] = place_states(clr_ref[s])
        ct_ref[0, rows, half:] = place_states(-cli_ref[s])

    for g in range(gb):
        b_cat = jnp.concatenate([lbr_ref[:, g].reshape(LANES, SSM_STATE),
                                 lbi_ref[:, g].reshape(LANES, SSM_STATE)], axis=1)
        c_cat = jnp.concatenate([cre_ref[g], -cim_ref[g]], axis=1)
        kg = lax.dot_general(b_cat, c_cat, (((1,), (1,)), ((), ())), preferred_element_type=F32,
                             precision=lax.Precision.HIGHEST)
        for j in range(S5_CHUNK):
            kcat[j, g * SSM_GROUP:(g + 1) * SSM_GROUP, :] = kg[j * SSM_GROUP:(j + 1) * SSM_GROUP, :]
    zero = jnp.zeros((LANES, LANES), BF16)
    for j in range(S5_CHUNK):
        tiled = jnp.dot(kcat[j].astype(BF16), rep_c, preferred_element_type=F32)
        blk = jnp.where(same_c, tiled, 0.0).astype(BF16)
        for s in range(S5_CHUNK - j):
            kt_ref[0, s * LANES:(s + 1) * LANES, (s + j) * LANES:(s + j + 1) * LANES] = blk
        if j > 0:
            for t in range(S5_CHUNK - j):
                kt_ref[0, (t + j) * LANES:(t + j + 1) * LANES, t * LANES:(t + 1) * LANES] = zero


def _s5_operators(a_re, a_im, log_dt, b_re, b_im, c_re, c_im):
    depth = a_re.shape[0]
    rows = depth * SSM_GROUPS
    lr = a_re.reshape(rows, SSM_STATE)
    li = a_im.reshape(rows, SSM_STATE)
    ldt = jnp.broadcast_to(log_dt.reshape(rows, 1), (rows, SSM_STATE))
    bre = b_re.transpose(0, 1, 3, 2).reshape(rows, SSM_GROUP, SSM_STATE)
    bim = b_im.transpose(0, 1, 3, 2).reshape(rows, SSM_GROUP, SSM_STATE)
    cre = c_re.reshape(rows, SSM_GROUP, SSM_STATE)
    cim = c_im.reshape(rows, SSM_GROUP, SSM_STATE)
    pw_shape = jax.ShapeDtypeStruct((SUBLANES, rows, SSM_STATE), F32)
    op_shape = jax.ShapeDtypeStruct((S5_CHUNK, rows, SSM_GROUP, SSM_STATE), F32)
    pw8r, pw8i, lbr, lbi, clr, cli = pl.pallas_call(
        _disc_kernel,
        out_shape=(pw_shape, pw_shape, op_shape, op_shape, op_shape, op_shape),
        compiler_params=_cparams(0),
        name="s5_discretise",
    )(lr, li, ldt, bre, bim, cre, cim)
    gb = GROUPS_PER_LANE_BLOCK
    n_blk = rows // gb
    width = S5_CHUNK * LANES
    op_spec = pl.BlockSpec((S5_CHUNK, gb, SSM_GROUP, SSM_STATE), lambda i: (0, i, 0, 0))
    c_spec = pl.BlockSpec((gb, SSM_GROUP, SSM_STATE), lambda i: (i, 0, 0))
    mat_spec = pl.BlockSpec((1, width, width), lambda i: (i, 0, 0))
    mat_shape = jax.ShapeDtypeStruct((n_blk, width, width), BF16)
    bmat, ctm, ktoep = pl.pallas_call(
        _place_kernel,
        grid=(n_blk,),
        in_specs=[op_spec] * 4 + [c_spec] * 2,
        out_specs=[mat_spec] * 3,
        out_shape=(mat_shape,) * 3,
        scratch_shapes=[pltpu.VMEM((S5_CHUNK, LANES, SSM_GROUP), F32)],
        compiler_params=_cparams(1),
        name="s5_place_operators",
    )(lbr, lbi, clr, cli, cre, cim)

    pw8r = pw8r.reshape(SUBLANES, depth, STATE_LANES).transpose(1, 0, 2)
    pw8i = pw8i.reshape(SUBLANES, depth, STATE_LANES).transpose(1, 0, 2)
    row = jnp.arange(SUBLANES)[None, :, None]

    def step_mult(pw):
        return jnp.stack([jnp.where(row >= k, pw[:, k - 1][:, None, :], 0.0) for k in (1, 2, 4)], axis=1)

    return bmat, ktoep, ctm, step_mult(pw8r), step_mult(pw8i), pw8r, pw8i


def _heads_major(x, position_minor):
    if position_minor:
        return x.T.reshape(N_HEADS, HEAD_DIM, x.shape[0])
    heads = jnp.stack([x[:, h * HEAD_DIM:(h + 1) * HEAD_DIM] for h in range(N_HEADS)], axis=0)
    return pltpu.einshape("htd->thd", heads)


def _inproj_body(x, shift_ref, scale_ref, g_ref, w_ref, hsum_ref, qg_ref, kg_ref, outs, position_minor):
    u_ref, zs_ref, q_ref, k_ref, v_ref, kb_ref, vb_ref, za_ref = outs
    ms = jnp.mean(x * x, axis=-1, keepdims=True)
    h = x * lax.rsqrt(ms + EPS) * g_ref[...]
    h = h * (1.0 + scale_ref[0]) + shift_ref[0]
    hb = h.astype(BF16)

    def proj(c):
        return jnp.dot(hb, w_ref[:, c * SSM_WIDTH:(c + 1) * SSM_WIDTH], preferred_element_type=F32)

    def head_norm(p, g):
        ss = jnp.dot((p * p).astype(BF16), hsum_ref[...], preferred_element_type=F32)
        return p * lax.rsqrt(ss * (1.0 / HEAD_DIM) + EPS) * g

    u_ref[...] = proj(0)
    zs_ref[...] = _silu(proj(1)).astype(BF16)
    q = head_norm(proj(2), qg_ref[...])
    q_ref[...] = (q * (HEAD_DIM ** -0.5 * LOG2_E)).astype(BF16)
    k = head_norm(proj(3), kg_ref[...])
    k_ref[...] = _heads_major(k, position_minor).reshape(k_ref.shape)
    kb_ref[...] = k.astype(BF16)
    v = proj(4)
    v_ref[...] = _heads_major(v, position_minor).reshape(v_ref.shape)
    vb_ref[...] = v.astype(BF16)
    za_ref[...] = _silu(proj(5)).astype(BF16)


def _s5_kernel(u_ref, h0r_ref, h0i_ref, bm_ref, kt_ref, ct_ref, mr_ref, mi_ref, pr_ref, pi_ref,
               y_ref, hfr_ref, hfi_ref,
               uflat, wre, wim, hpr, hpi, *, n_seq, lane_chunk):
    rows = u_ref.shape[0]
    n_chunks = rows // S5_CHUNK
    seq_chunks = n_chunks // n_seq
    half = GROUPS_PER_LANE_BLOCK * SSM_STATE

    def token_rows(c, t):
        if n_seq == 1:
            return pl.ds(t, n_chunks, stride=S5_CHUNK)
        return pl.ds(c * S5_CHUNK + t, n_seq, stride=seq_chunks * S5_CHUNK)

    fold_blocks = [(0, slice(None))] if n_seq == 1 else [(c, slice(c * n_seq, (c + 1) * n_seq))
                                                        for c in range(seq_chunks)]
    for c, frows in fold_blocks:
        for t in range(S5_CHUNK):
            uflat[frows, t * LANES:(t + 1) * LANES] = u_ref[token_rows(c, t), :].astype(BF16)
    w = jnp.dot(uflat[...], bm_ref[...], preferred_element_type=F32)
    wre[...] = w[:, :half]
    wim[...] = w[:, half:]

    if n_seq == 1:
        hpr[0:SUBLANES, :] = jnp.broadcast_to(h0r_ref[0], (SUBLANES, half))
        hpi[0:SUBLANES, :] = jnp.broadcast_to(h0i_ref[0], (SUBLANES, half))
        for ch in range(half // lane_chunk):
            sl = slice(ch * lane_chunk, (ch + 1) * lane_chunk)
            prc = pr_ref[:, sl]
            pic = pi_ref[:, sl]

            def group_body(r, carry):
                cr, ci = carry
                row = pl.multiple_of(r * SUBLANES, SUBLANES)
                vr = wre[pl.ds(row, SUBLANES), sl]
                vi = wim[pl.ds(row, SUBLANES), sl]
                for idx, k in enumerate((1, 2, 4)):
                    mr = mr_ref[idx, :, sl]
                    mi = mi_ref[idx, :, sl]
                    sr = pltpu.roll(vr, k, 0)
                    si = pltpu.roll(vi, k, 0)
                    vr, vi = vr + (mr * sr - mi * si), vi + (mr * si + mi * sr)
                hr = vr + (prc * cr - pic * ci)
                hi = vi + (prc * ci + pic * cr)
                hpr[pl.ds(row + SUBLANES, SUBLANES), sl] = hr
                hpi[pl.ds(row + SUBLANES, SUBLANES), sl] = hi
                last = SUBLANES - 1
                return (jnp.broadcast_to(hr[last:last + 1], hr.shape),
                        jnp.broadcast_to(hi[last:last + 1], hi.shape))

            cr, ci = lax.fori_loop(0, n_chunks // SUBLANES, group_body,
                                   (hpr[0:SUBLANES, sl], hpi[0:SUBLANES, sl]), unroll=2)
            hfr_ref[0, :, sl] = cr[0:1]
            hfi_ref[0, :, sl] = ci[0:1]
        h_prev_r = hpr[SUBLANES - 1:SUBLANES - 1 + n_chunks, :]
        h_prev_i = hpi[SUBLANES - 1:SUBLANES - 1 + n_chunks, :]
    else:
        a8r = pr_ref[0:1, :]
        a8i = pi_ref[0:1, :]
        hr = h0r_ref[...].reshape(n_seq, half)
        hi = h0i_ref[...].reshape(n_seq, half)
        for c, frows in fold_blocks:
            hpr[frows, :] = hr
            hpi[frows, :] = hi
            wr = wre[frows, :]
            wi = wim[frows, :]
            hr, hi = a8r * hr - a8i * hi + wr, a8r * hi + a8i * hr + wi
        hfr_ref[...] = hr.reshape(hfr_ref.shape)
        hfi_ref[...] = hi.reshape(hfi_ref.shape)
        h_prev_r = hpr[0:n_chunks, :]
        h_prev_i = hpi[0:n_chunks, :]

    h_prev = jnp.concatenate([h_prev_r.astype(BF16), h_prev_i.astype(BF16)], axis=1)
    yf = jnp.dot(uflat[...], kt_ref[...], preferred_element_type=F32)
    yf = yf + lax.dot_general(h_prev, ct_ref[...], (((1,), (1,)), ((), ())), preferred_element_type=F32)
    for c, frows in fold_blocks:
        for t in range(S5_CHUNK):
            y_ref[token_rows(c, t), :] = yf[frows, t * LANES:(t + 1) * LANES]


def _s5_branch(u, h0_re, h0_im, bmat, ktoep, ctm, mr, mi, pr, pi_, layer, n_seq, seq_rows):
    n = u.shape[0]
    tile = n_seq * seq_rows
    n_total = h0_re.shape[0]
    n_chunks = tile // S5_CHUNK
    half = GROUPS_PER_LANE_BLOCK * SSM_STATE
    row_spec = pl.BlockSpec((tile, LANES), lambda lb, b: (b, lb))
    st_spec = pl.BlockSpec((n_seq, 1, half), lambda lb, b: (b, 0, lb))
    hp_rows = n_chunks + SUBLANES if n_seq == 1 else n_chunks
    mat_spec = pl.BlockSpec((None,) + bmat.shape[1:], lambda lb, b: (layer * N_LANE_BLOCKS + lb, 0, 0))
    kern = functools.partial(_s5_kernel, n_seq=n_seq, lane_chunk=256)
    return pl.pallas_call(
        kern,
        grid=(N_LANE_BLOCKS, n // tile),
        in_specs=[row_spec, st_spec, st_spec,
                  mat_spec, mat_spec, mat_spec,
                  pl.BlockSpec((None, 3, SUBLANES, half), lambda lb, b: (layer, 0, 0, lb)),
                  pl.BlockSpec((None, 3, SUBLANES, half), lambda lb, b: (layer, 0, 0, lb)),
                  pl.BlockSpec((None, SUBLANES, half), lambda lb, b: (layer, 0, lb)),
                  pl.BlockSpec((None, SUBLANES, half), lambda lb, b: (layer, 0, lb))],
        out_specs=[row_spec, st_spec, st_spec],
        out_shape=(jax.ShapeDtypeStruct((n, SSM_WIDTH), F32),
                   jax.ShapeDtypeStruct((n_total, 1, STATE_LANES), F32),
                   jax.ShapeDtypeStruct((n_total, 1, STATE_LANES), F32)),
        scratch_shapes=[pltpu.VMEM((n_chunks, S5_CHUNK * LANES), BF16),
                        pltpu.VMEM((n_chunks, half), F32), pltpu.VMEM((n_chunks, half), F32),
                        pltpu.VMEM((hp_rows, half), F32), pltpu.VMEM((hp_rows, half), F32)],
        compiler_params=_cparams(2),
        name="s5_branch",
    )(u, h0_re, h0_im, bmat, ktoep, ctm, mr, mi, pr, pi_)


def _sb_mask_queries(q_rows, qm_ref, tq):
    lane = lax.broadcasted_iota(jnp.int32, (tq, LANES), 1)
    for j in range(N_PAIRS):
        qpair = q_rows(j).astype(F32)
        for hh in range(HEADS_PER_LANE_BLOCK):
            h = j * HEADS_PER_LANE_BLOCK + hh
            in_head = (lane >= hh * HEAD_DIM) & (lane < (hh + 1) * HEAD_DIM)
            qm_ref[h * tq:(h + 1) * tq, :] = jnp.where(in_head, qpair, 0.0).astype(BF16)


class _SbTile(NamedTuple):
    tq: int
    k_tile: Any
    v_tile: Any
    keys_minor: bool
    valid: Any
    first: bool
    qm: Any
    lb: Any
    hl: Any
    a: Any
    car: Any
    acc: Any


def _sb_scores(t):
    pair = HEADS_PER_LANE_BLOCK * t.tq
    valid2 = None if t.valid is None else jnp.concatenate([t.valid] * HEADS_PER_LANE_BLOCK, axis=0)
    for j in range(N_PAIRS):
        rows = slice(j * pair, (j + 1) * pair)
        contract_k = 0 if t.keys_minor else 1
        z = lax.dot_general(t.qm[rows, :], t.k_tile(j), (((1,), (contract_k,)), ((), ())),
                            preferred_element_type=F32)
        sp = jnp.maximum(z, 0.0) + jnp.log2(1.0 + jnp.exp2(-jnp.abs(z)))
        t.lb[rows, :] = z - sp
        if valid2 is not None:
            sp = jnp.where(valid2, sp, 0.0)
        hi = sp.astype(BF16)
        t.hl[rows, :KEY_TILE] = hi
        t.hl[rows, KEY_TILE:] = (sp - hi.astype(F32)).astype(BF16)


def _sb_weights(t, tri_ref):
    cs = jnp.dot(t.hl[...], tri_ref[...], preferred_element_type=F32)
    cmax = None
    for h in range(N_HEADS):
        rows = slice(h * t.tq, (h + 1) * t.tq)
        after = cs[rows, :KEY_TILE]
        total = cs[rows, KEY_TILE:]
        if not t.first:
            carry = t.car[rows, :]
            after = after + carry
            total = total + carry
        a = jnp.exp2(t.lb[rows, :] + after)
        if t.valid is not None:
            a = jnp.where(t.valid, a, 0.0)
        t.a[rows, :] = a.astype(BF16)
        t.car[rows, :] = total
        cmax = total if cmax is None else jnp.maximum(cmax, total)
    return jnp.max(cmax)


def _sb_output(t):
    pair = HEADS_PER_LANE_BLOCK * t.tq
    lane = lax.broadcasted_iota(jnp.int32, (t.tq, LANES), 1)
    for j in range(N_PAIRS):
        contract_v = 1 if t.keys_minor else 0
        pv = lax.dot_general(t.a[j * pair:(j + 1) * pair, :], t.v_tile(j), (((1,), (contract_v,)), ((), ())),
                             preferred_element_type=F32)
        out = jnp.where(lane < HEAD_DIM, pv[:t.tq], pv[t.tq:])
        if t.first:
            t.acc[:, j * LANES:(j + 1) * LANES] = out
        else:
            t.acc[:, j * LANES:(j + 1) * LANES] += out


def _sb_tiles(tiles, tri_ref):
    for t in tiles:
        _sb_scores(t)
    ms = [_sb_weights(t, tri_ref) for t in tiles]
    for t in tiles:
        _sb_output(t)
    return ms


def _sb_scratch(n_blocks, tq):
    rows = N_HEADS * tq
    return [pltpu.VMEM((n_blocks, rows, LANES), BF16),
            pltpu.VMEM((n_blocks, rows, LANES), F32),
            pltpu.VMEM((n_blocks, rows, 2 * KEY_TILE), BF16),
            pltpu.VMEM((n_blocks, rows, LANES), BF16),
            pltpu.VMEM((n_blocks, rows, LANES), F32),
            pltpu.VMEM((n_blocks, tq, ATT_WIDTH), F32),
            pltpu.SMEM((n_blocks,), F32)]


def _sb_prompt_kernel(q_ref, k_ref, v_ref, za_ref, tri_ref, o_ref,
                      qm_ref, lb_ref, hl_ref, a_ref, car_ref, acc_ref, m_ref):
    tq = SB_Q_ROWS
    n_blocks = q_ref.shape[0] // tq
    step_row0 = pl.program_id(1) * (n_blocks * tq)
    row = lax.broadcasted_iota(jnp.int32, (tq, KEY_TILE), 0)
    col = lax.broadcasted_iota(jnp.int32, (tq, KEY_TILE), 1)

    def tiles(ref, start):
        return lambda j: ref[pl.ds(start, KEY_TILE), j * LANES:(j + 1) * LANES]

    def band_start(s):
        return pl.multiple_of(jnp.maximum(step_row0 + (s + 1) * tq - KEY_TILE, 0), tq)

    def tile(s, lo, valid, first):
        return _SbTile(tq, tiles(k_ref, lo), tiles(v_ref, lo), False, valid, first, qm_ref.at[s], lb_ref.at[s],
                       hl_ref.at[s], a_ref.at[s], car_ref.at[s], acc_ref.at[s])

    band = []
    for s in range(n_blocks):
        lo = band_start(s)
        _sb_mask_queries(lambda j: q_ref[s * tq:(s + 1) * tq, j * LANES:(j + 1) * LANES], qm_ref.at[s], tq)
        valid = (col - row) < (step_row0 + s * tq - lo)
        band.append(tile(s, lo, valid, True))
    for s, m in enumerate(_sb_tiles(band, tri_ref)):
        m_ref[s] = m

    def more_tiles(s, _):
        def cond(state):
            hi, m = state
            return (hi > 0) & (m > SB_LOG2_CUTOFF)

        def body(state):
            hi, _ = state
            lo = pl.multiple_of(jnp.maximum(hi - KEY_TILE, 0), tq)
            valid = col < (hi - lo)
            (m,) = _sb_tiles([tile(s, lo, valid, False)], tri_ref)
            return lo, m

        lax.while_loop(cond, body, (band_start(s), m_ref[s]))
        return 0

    lax.fori_loop(0, n_blocks, more_tiles, 0)
    for s in range(n_blocks):
        rows = slice(s * tq, (s + 1) * tq)
        o_ref[rows, :] = (acc_ref[s] * za_ref[rows, :].astype(F32)).astype(BF16)


def _sb_prompt(q, kb, vb, za, tri, n_batch, seq_len):
    step_rows = SB_STEP_BLOCKS * SB_Q_ROWS
    nq = seq_len // step_rows
    row_spec = pl.BlockSpec((step_rows, ATT_WIDTH), lambda b, i: (b * nq + i, 0))
    seq_spec = pl.BlockSpec((seq_len, ATT_WIDTH), lambda b, i: (b, 0))
    return pl.pallas_call(
        _sb_prompt_kernel,
        grid=(n_batch, nq),
        in_specs=[row_spec, seq_spec, seq_spec, row_spec, pl.BlockSpec(tri.shape, lambda b, i: (0, 0))],
        out_specs=row_spec,
        out_shape=jax.ShapeDtypeStruct(q.shape, BF16),
        scratch_shapes=_sb_scratch(SB_STEP_BLOCKS, SB_Q_ROWS),
        compiler_params=_cparams(2),
        name="sb_prompt",
    )(q, kb, vb, za, tri)


def _sb_sample_kernel(q_ref, kn_ref, vn_ref, kc_ref, vc_ref, za_ref, tri_ref, o_ref,
                      qm_ref, lb_ref, hl_ref, a_ref, car_ref, acc_ref, m_ref):
    tq = q_ref.shape[0]
    n_past = kc_ref.shape[-1] // KEY_TILE
    _sb_mask_queries(lambda j: q_ref[:, j * LANES:(j + 1) * LANES], qm_ref.at[0], tq)
    row = lax.broadcasted_iota(jnp.int32, (tq, KEY_TILE), 0)
    col = lax.broadcasted_iota(jnp.int32, (tq, KEY_TILE), 1)
    scratch = (qm_ref.at[0], lb_ref.at[0], hl_ref.at[0], a_ref.at[0], car_ref.at[0], acc_ref.at[0])

    new_tile = lambda ref: (lambda j: ref[:, j * LANES:(j + 1) * LANES])
    (m0,) = _sb_tiles([_SbTile(tq, new_tile(kn_ref), new_tile(vn_ref), False, col < row, True, *scratch)],
                      tri_ref)

    def past_tile(ref, t):
        start = pl.multiple_of(t * KEY_TILE, KEY_TILE)
        return lambda j: ref[0, 0, j * HEADS_PER_LANE_BLOCK:(j + 1) * HEADS_PER_LANE_BLOCK, :,
                             pl.ds(start, KEY_TILE)].reshape(LANES, KEY_TILE).astype(BF16)

    def cond(state):
        t, m = state
        return (t >= 0) & (m > SB_LOG2_CUTOFF)

    def body(state):
        t, _ = state
        (m,) = _sb_tiles([_SbTile(tq, past_tile(kc_ref, t), past_tile(vc_ref, t), True, None, False,
                                  *scratch)], tri_ref)
        return t - 1, m

    lax.while_loop(cond, body, (n_past - 1, m0))
    o_ref[...] = (acc_ref[0] * za_ref[...].astype(F32)).astype(BF16)


def _sb_sample(q, k_new, v_new, k_past, v_past, layer, za, tri, n_batch, t_new):
    past = k_past.shape[-1]
    row_spec = pl.BlockSpec((t_new, ATT_WIDTH), lambda b: (b, 0))
    new_spec = pl.BlockSpec((KEY_TILE, ATT_WIDTH), lambda b: (b, 0))
    past_spec = pl.BlockSpec((1, 1, N_HEADS, HEAD_DIM, past), lambda b: (layer, b, 0, 0, 0))
    return pl.pallas_call(
        _sb_sample_kernel,
        grid=(n_batch,),
        in_specs=[row_spec, new_spec, new_spec, past_spec, past_spec, row_spec,
                  pl.BlockSpec(tri.shape, lambda b: (0, 0))],
        out_specs=row_spec,
        out_shape=jax.ShapeDtypeStruct(q.shape, BF16),
        scratch_shapes=_sb_scratch(1, t_new),
        compiler_params=_cparams(1),
        name="sb_sample",
    )(q, k_new, v_new, k_past, v_past, za, tri)


def _outproj_body(x, ysc_ref, u_ref, zs_ref, ya_ref, gate_ref, d_ref, wglu_ref, bglu_ref, w_ref):
    y = ysc_ref[...] + d_ref[...] * u_ref[...]
    g = _gelu_tanh(y)
    glu = _sigmoid(jnp.dot(g.astype(BF16), wglu_ref[...], preferred_element_type=F32) + bglu_ref[...])
    ys = (g * glu * zs_ref[...].astype(F32)).astype(BF16)
    mix = jnp.dot(ys, w_ref[:SSM_WIDTH, :], preferred_element_type=F32)
    mix = mix + jnp.dot(ya_ref[...], w_ref[SSM_WIDTH:, :], preferred_element_type=F32)
    return x + gate_ref[0] * mix


class _OutProj(NamedTuple):
    ysc: Any
    u: Any
    zs: Any
    ya: Any
    gate: Any
    d_skip: Any
    w_glu_b: Any
    b_glu: Any
    w_out_b: Any


class _InProj(NamedTuple):
    shift: Any
    scale: Any
    norm_g: Any
    w_in_b: Any
    hsum: Any
    q_g: Any
    k_g: Any
    layer: int
    kv_all: Any


N_OUTPROJ_REFS = len(_OutProj._fields)
N_INPROJ_REFS = len(_InProj._fields) - 2


def _layer_kernel(x_ref, *refs, closes, opens, position_minor):
    refs = list(refs)
    x = x_ref[...]
    if closes:
        out_in, refs = refs[:N_OUTPROJ_REFS], refs[N_OUTPROJ_REFS:]
    if opens:
        in_in, refs = refs[:N_INPROJ_REFS], refs[N_INPROJ_REFS + 2:]
    if closes:
        x = _outproj_body(x, *out_in)
        refs.pop(0)[...] = x
    if opens:
        _inproj_body(x, *in_in, refs, position_minor)


def _layer_call(x2d, closing, opening, tm, rows_per_seq, depth):
    n = x2d.shape[0]
    if rows_per_seq is None:
        mod_spec = pl.BlockSpec((1, tm, D_MODEL), lambda i: (0, i, 0))
    else:
        tps = rows_per_seq // tm
        mod_spec = pl.BlockSpec((1, 1, D_MODEL), lambda i: (i // tps, 0, 0))
    row_spec = lambda w: pl.BlockSpec((tm, w), lambda i: (i, 0))
    full = lambda a: pl.BlockSpec(a.shape, lambda i: (0,) * a.ndim)
    position_minor = rows_per_seq is not None
    args, in_specs, out_specs, out_shape, aliases = [x2d], [row_spec(D_MODEL)], [], [], {}
    if closing is not None:
        args += list(closing)
        in_specs += [row_spec(SSM_WIDTH)] * 3 + [row_spec(ATT_WIDTH), mod_spec] + [full(a) for a in closing[5:]]
        out_specs.append(row_spec(D_MODEL))
        out_shape.append(jax.ShapeDtypeStruct(x2d.shape, F32))
    if opening is not None:
        layer = opening.layer
        bf = jax.ShapeDtypeStruct((n, SSM_WIDTH), BF16)
        if position_minor:
            f3 = jax.ShapeDtypeStruct((depth, n // rows_per_seq, N_HEADS, HEAD_DIM, rows_per_seq), F32)
            kv_spec = pl.BlockSpec((1, 1, N_HEADS, HEAD_DIM, tm), lambda i: (layer, i // tps, 0, 0, i % tps))
        else:
            f3 = jax.ShapeDtypeStruct((depth, n, N_HEADS, HEAD_DIM), F32)
            kv_spec = pl.BlockSpec((1, tm, N_HEADS, HEAD_DIM), lambda i: (layer, i, 0, 0))
        assert opening.kv_all[0].shape == f3.shape and opening.kv_all[1].shape == f3.shape
        args += list(opening[:N_INPROJ_REFS]) + list(opening.kv_all)
        in_specs += [mod_spec, mod_spec] + [full(a) for a in opening[2:N_INPROJ_REFS]]
        in_specs += [pl.BlockSpec(memory_space=pl.ANY)] * 2
        aliases = {len(args) - 2: len(out_shape) + 3, len(args) - 1: len(out_shape) + 4}
        out_specs += [row_spec(SSM_WIDTH)] * 3 + [kv_spec] * 2 + [row_spec(SSM_WIDTH)] * 3
        out_shape += [jax.ShapeDtypeStruct((n, SSM_WIDTH), F32), bf, bf, f3, f3, bf, bf, bf]
    outs = pl.pallas_call(
        functools.partial(_layer_kernel, closes=closing is not None, opens=opening is not None,
                          position_minor=position_minor),
        grid=(n // tm,),
        in_specs=in_specs,
        out_specs=out_specs,
        out_shape=out_shape,
        input_output_aliases=aliases,
        compiler_params=_cparams(1),
        name="layer_" + "_".join(["close"] * (closing is not None) + ["open"] * (opening is not None)),
    )(*args)
    x_new = outs[0] if closing is not None else None
    opened = tuple(outs[len(outs) - 8:]) if opening is not None else None
    return x_new, opened


def _sb_sum_matrix():
    j = jnp.arange(2 * KEY_TILE)[:, None] % KEY_TILE
    s = jnp.arange(2 * KEY_TILE)[None, :]
    return -((s >= KEY_TILE) | (j > s)).astype(BF16)


def _head_sum_matrix():
    a = jnp.arange(ATT_WIDTH)
    return (a[:, None] // HEAD_DIM == a[None, :] // HEAD_DIM).astype(BF16)


def kernel(x_prompt, x_sample, c_prompt, c_sample, cache_k, cache_v, state_ssm_re, state_ssm_im, norm_g, w_mod, b_mod, w_in, ssm_a_re, ssm_a_im, ssm_log_dt, ssm_b_re, ssm_b_im, ssm_c_re, ssm_c_im, ssm_d, w_glu, b_glu, q_norm_g, k_norm_g, w_out):
    depth = w_in.shape[0]
    nb, seq_len, _ = x_prompt.shape
    ns, t_new, _ = x_sample.shape
    tm = 512

    mod = _modulation(jnp.concatenate([c_prompt, c_sample], axis=0), w_mod, b_mod)
    s5_ops = _s5_operators(ssm_a_re, ssm_a_im, ssm_log_dt, ssm_b_re, ssm_b_im, ssm_c_re, ssm_c_im)

    w_in_b = w_in.astype(BF16)
    w_out_b = w_out.astype(BF16)
    w_glu_b = w_glu.astype(BF16)
    tri = _sb_sum_matrix()
    hsum = _head_sum_matrix()

    xp = x_prompt.reshape(nb * seq_len, D_MODEL)
    xs = x_sample.reshape(ns * t_new, D_MODEL)
    zero_state = jnp.zeros((nb, 1, STATE_LANES), F32)
    kc = jnp.transpose(cache_k, (0, 1, 3, 4, 2))
    vc = jnp.transpose(cache_v, (0, 1, 3, 4, 2))
    outs = {name: [] for name in ("pr", "pi", "sr", "si")}
    pkv = tuple(jnp.zeros((depth, nb, N_HEADS, HEAD_DIM, seq_len), F32) for _ in range(2))
    skv = tuple(jnp.zeros((depth, ns * t_new, N_HEADS, HEAD_DIM), F32) for _ in range(2))
    def in_proj_args(l, shift, scale, kv_all):
        return _InProj(shift, scale, norm_g[l].reshape(1, D_MODEL), w_in_b[l], hsum,
                       jnp.tile(q_norm_g[l], N_HEADS).reshape(1, ATT_WIDTH),
                       jnp.tile(k_norm_g[l], N_HEADS).reshape(1, ATT_WIDTH), l, kv_all)

    def prompt_mod(l):
        mp = mod[l, :nb].reshape(nb, 1, 3 * D_MODEL)
        return tuple(mp[:, :, i * D_MODEL:(i + 1) * D_MODEL] for i in range(3))

    shift, scale, gate = prompt_mod(0)
    _, opened = _layer_call(xp, None, in_proj_args(0, shift, scale, pkv), tm, seq_len, depth)
    for l in range(depth):
        glu_w = (ssm_d[l].reshape(1, SSM_WIDTH), w_glu_b[l], b_glu[l].reshape(1, SSM_WIDTH))

        u, zs, q, k_all, v_all, kb, vb, za = opened
        pkv = (k_all, v_all)
        ysc, hfr, hfi = _s5_branch(u, zero_state, zero_state, *s5_ops, l, 1, seq_len)
        ya = _sb_prompt(q, kb, vb, za, tri, nb, seq_len)
        closing = _OutProj(ysc, u, zs, ya, gate, *glu_w, w_out_b[l])
        opening = None
        if l + 1 < depth:
            shift, scale, gate = prompt_mod(l + 1)
            opening = in_proj_args(l + 1, shift, scale, pkv)
        xp, opened = _layer_call(xp, closing, opening, tm, seq_len, depth)
        outs["pr"].append(hfr.reshape(nb, SSM_GROUPS, SSM_STATE))
        outs["pi"].append(hfi.reshape(nb, SSM_GROUPS, SSM_STATE))

        ms = jnp.repeat(mod[l, nb:], t_new, axis=0).reshape(1, ns * t_new, 3 * D_MODEL)
        s_shift, s_scale, s_gate = (ms[:, :, i * D_MODEL:(i + 1) * D_MODEL] for i in range(3))
        ts = ns * t_new
        _, (u, zs, q, k_all, v_all, kb, vb, za) = _layer_call(
            xs, None, in_proj_args(l, s_shift, s_scale, skv), ts, None, depth)
        skv = (k_all, v_all)
        h0r = state_ssm_re[l].reshape(ns, 1, STATE_LANES)
        h0i = state_ssm_im[l].reshape(ns, 1, STATE_LANES)
        ysc, hfr, hfi = _s5_branch(u, h0r, h0i, *s5_ops, l, ns, t_new)
        pad = lambda a: jnp.pad(a.reshape(ns, t_new, ATT_WIDTH),
                                ((0, 0), (0, KEY_TILE - t_new), (0, 0))).reshape(ns * KEY_TILE, ATT_WIDTH)
        ya = _sb_sample(q, pad(kb), pad(vb), kc, vc, l, za, tri, ns, t_new)
        xs, _ = _layer_call(xs, _OutProj(ysc, u, zs, ya, s_gate, *glu_w, w_out_b[l]), None, ts, None, depth)
        outs["sr"].append(hfr.reshape(ns, SSM_GROUPS, SSM_STATE))
        outs["si"].append(hfi.reshape(ns, SSM_GROUPS, SSM_STATE))

    st = lambda name: jnp.stack(outs[name])
    heads_p = lambda a: jnp.transpose(a, (0, 1, 4, 2, 3))
    heads_s = lambda a: a.reshape(depth, ns, t_new, N_HEADS, HEAD_DIM)
    return (xp.reshape(nb, seq_len, D_MODEL), xs.reshape(ns, t_new, D_MODEL),
            heads_p(pkv[0]), heads_p(pkv[1]), st("pr"), st("pi"),
            heads_s(skv[0]), heads_s(skv[1]), st("sr"), st("si"))
```

```python
import functools
import math
from typing import Any, NamedTuple

import jax
import jax.numpy as jnp
from jax import lax
from jax.experimental import pallas as pl
from jax.experimental.pallas import tpu as pltpu

F32 = jnp.float32
BF16 = jnp.bfloat16

D_MODEL = 1024
SSM_WIDTH = 512
SSM_GROUP = 16
SSM_GROUPS = 32
SSM_STATE = 64
STATE_LANES = SSM_GROUPS * SSM_STATE
ATT_WIDTH = 512
HEAD_DIM = 64
N_HEADS = 8
IN_WIDTH = 2 * SSM_WIDTH + 4 * ATT_WIDTH
EPS = 1e-6

LANES = 128
SUBLANES = 8
MXU_DIM = 256
HEADS_PER_LANE_BLOCK = LANES // HEAD_DIM
GROUPS_PER_LANE_BLOCK = LANES // SSM_GROUP
N_LANE_BLOCKS = SSM_WIDTH // LANES
N_PAIRS = N_HEADS // HEADS_PER_LANE_BLOCK
S5_CHUNK = SUBLANES
KEY_TILE = 128
SB_Q_ROWS = 32
SB_STEP_BLOCKS = 16
LOG2_E = math.log2(math.e)
SB_LOG2_CUTOFF = -40.0 * LOG2_E
VMEM_LIMIT = 56 * 1024 * 1024


def _cparams(n_axes):
    return pltpu.CompilerParams(dimension_semantics=("arbitrary",) * n_axes,
                                vmem_limit_bytes=VMEM_LIMIT)


def _silu(x):
    return x * (1.0 / (1.0 + jnp.exp(-x)))


def _sigmoid(x):
    return 1.0 / (1.0 + jnp.exp(-x))


def _gelu_tanh(x):
    c = math.sqrt(2.0 / math.pi)
    return 0.5 * x * (1.0 + jnp.tanh(c * (x + 0.044715 * (x * x * x))))


def _mod_kernel(c_ref, w_ref, b_ref, o_ref):
    c = c_ref[...]
    a = _silu(c)
    o_ref[0] = jnp.dot(a, w_ref[0], preferred_element_type=F32,
                       precision=lax.Precision.HIGHEST) + b_ref[0]


def _modulation(c_all, w_mod, b_mod):
    depth = w_mod.shape[0]
    n = c_all.shape[0]
    nt = 3
    return pl.pallas_call(
        _mod_kernel,
        grid=(depth, nt),
        in_specs=[pl.BlockSpec((n, D_MODEL), lambda l, j: (0, 0)),
                  pl.BlockSpec((1, D_MODEL, D_MODEL), lambda l, j: (l, 0, j)),
                  pl.BlockSpec((1, 1, D_MODEL), lambda l, j: (l, 0, j))],
        out_specs=pl.BlockSpec((1, n, D_MODEL), lambda l, j: (l, 0, j)),
        out_shape=jax.ShapeDtypeStruct((depth, n, 3 * D_MODEL), F32),
        compiler_params=_cparams(2),
        name="modulation",
    )(c_all, w_mod, b_mod.reshape(depth, 1, 3 * D_MODEL))


def _disc_kernel(lr_ref, li_ref, ldt_ref, bre_ref, bim_ref, cre_ref, cim_ref,
                 pw8r_ref, pw8i_ref, lbr_ref, lbi_ref, clr_ref, cli_ref):
    lr = lr_ref[...]
    li = li_ref[...]
    dt = jnp.exp(ldt_ref[...])
    mag = jnp.exp(lr * dt)
    ang = li * dt
    ab_re = mag * jnp.cos(ang)
    ab_im = mag * jnp.sin(ang)
    den = lr * lr + li * li
    nr = ab_re - 1.0
    f_re = (nr * lr + ab_im * li) / den
    f_im = (ab_im * lr - nr * li) / den
    bre = bre_ref[...]
    bim = bim_ref[...]
    cre = cre_ref[...]
    cim = cim_ref[...]
    xr = f_re[:, None, :] * bre - f_im[:, None, :] * bim
    xi = f_re[:, None, :] * bim + f_im[:, None, :] * bre
    pr, pi_ = ab_re, ab_im
    for t in range(S5_CHUNK):
        lbr_ref[t] = xr
        lbi_ref[t] = xi
        clr_ref[t] = cre * pr[:, None, :] - cim * pi_[:, None, :]
        cli_ref[t] = cre * pi_[:, None, :] + cim * pr[:, None, :]
        if t + 1 < S5_CHUNK:
            xr, xi = (xr * ab_re[:, None, :] - xi * ab_im[:, None, :],
                      xr * ab_im[:, None, :] + xi * ab_re[:, None, :])
            pr, pi_ = pr * ab_re - pi_ * ab_im, pr * ab_im + pi_ * ab_re
    a8r, a8i = pr, pi_
    qr, qi = a8r, a8i
    for m in range(SUBLANES):
        pw8r_ref[m] = qr
        pw8i_ref[m] = qi
        qr, qi = qr * a8r - qi * a8i, qr * a8i + qi * a8r


def _place_kernel(lbr_ref, lbi_ref, clr_ref, cli_ref, cre_ref, cim_ref, bm_ref, ct_ref, kt_ref, kcat):
    gb = GROUPS_PER_LANE_BLOCK
    half = gb * SSM_STATE
    rep_p = (lax.broadcasted_iota(jnp.int32, (SSM_STATE, half), 0)
             == lax.broadcasted_iota(jnp.int32, (SSM_STATE, half), 1) % SSM_STATE).astype(BF16)
    rep_c = (lax.broadcasted_iota(jnp.int32, (SSM_GROUP, LANES), 0)
             == lax.broadcasted_iota(jnp.int32, (SSM_GROUP, LANES), 1) % SSM_GROUP).astype(BF16)
    same_p = (lax.broadcasted_iota(jnp.int32, (LANES, half), 0) // SSM_GROUP
              == lax.broadcasted_iota(jnp.int32, (LANES, half), 1) // SSM_STATE)
    same_c = (lax.broadcasted_iota(jnp.int32, (LANES, LANES), 0) // SSM_GROUP
              == lax.broadcasted_iota(jnp.int32, (LANES, LANES), 1) // SSM_GROUP)

    def place_states(x):
        tiled = jnp.dot(x.reshape(LANES, SSM_STATE).astype(BF16), rep_p, preferred_element_type=F32)
        return jnp.where(same_p, tiled, 0.0).astype(BF16)

    for s in range(S5_CHUNK):
        rows = slice(s * LANES, (s + 1) * LANES)
        bm_ref[0, rows, :half] = place_states(lbr_ref[S5_CHUNK - 1 - s])
        bm_ref[0, rows, half:] = place_states(lbi_ref[S5_CHUNK - 1 - s])
        ct_ref[0, rows, :half] = place_states(clr_ref[s])
        ct_ref[0, rows, half:] = place_states(-cli_ref[s])

    for g in range(gb):
        b_cat = jnp.concatenate([lbr_ref[:, g].reshape(LANES, SSM_STATE),
                                 lbi_ref[:, g].reshape(LANES, SSM_STATE)], axis=1)
        c_cat = jnp.concatenate([cre_ref[g], -cim_ref[g]], axis=1)
        kg = lax.dot_general(b_cat, c_cat, (((1,), (1,)), ((), ())), preferred_element_type=F32,
                             precision=lax.Precision.HIGHEST)
        for j in range(S5_CHUNK):
            kcat[j, g * SSM_GROUP:(g + 1) * SSM_GROUP, :] = kg[j * SSM_GROUP:(j + 1) * SSM_GROUP, :]
    zero = jnp.zeros((LANES, LANES), BF16)
    for j in range(S5_CHUNK):
        tiled = jnp.dot(kcat[j].astype(BF16), rep_c, preferred_element_type=F32)
        blk = jnp.where(same_c, tiled, 0.0).astype(BF16)
        for s in range(S5_CHUNK - j):
            kt_ref[0, s * LANES:(s + 1) * LANES, (s + j) * LANES:(s + j + 1) * LANES] = blk
        if j > 0:
            for t in range(S5_CHUNK - j):
                kt_ref[0, (t + j) * LANES:(t + j + 1) * LANES, t * LANES:(t + 1) * LANES] = zero


def _s5_operators(a_re, a_im, log_dt, b_re, b_im, c_re, c_im):
    depth = a_re.shape[0]
    rows = depth * SSM_GROUPS
    lr = a_re.reshape(rows, SSM_STATE)
    li = a_im.reshape(rows, SSM_STATE)
    ldt = jnp.broadcast_to(log_dt.reshape(rows, 1), (rows, SSM_STATE))
    bre = b_re.transpose(0, 1, 3, 2).reshape(rows, SSM_GROUP, SSM_STATE)
    bim = b_im.transpose(0, 1, 3, 2).reshape(rows, SSM_GROUP, SSM_STATE)
    cre = c_re.reshape(rows, SSM_GROUP, SSM_STATE)
    cim = c_im.reshape(rows, SSM_GROUP, SSM_STATE)
    pw_shape = jax.ShapeDtypeStruct((SUBLANES, rows, SSM_STATE), F32)
    op_shape = jax.ShapeDtypeStruct((S5_CHUNK, rows, SSM_GROUP, SSM_STATE), F32)
    pw8r, pw8i, lbr, lbi, clr, cli = pl.pallas_call(
        _disc_kernel,
        out_shape=(pw_shape, pw_shape, op_shape, op_shape, op_shape, op_shape),
        compiler_params=_cparams(0),
        name="s5_discretise",
    )(lr, li, ldt, bre, bim, cre, cim)
    gb = GROUPS_PER_LANE_BLOCK
    n_blk = rows // gb
    width = S5_CHUNK * LANES
    op_spec = pl.BlockSpec((S5_CHUNK, gb, SSM_GROUP, SSM_STATE), lambda i: (0, i, 0, 0))
    c_spec = pl.BlockSpec((gb, SSM_GROUP, SSM_STATE), lambda i: (i, 0, 0))
    mat_spec = pl.BlockSpec((1, width, width), lambda i: (i, 0, 0))
    mat_shape = jax.ShapeDtypeStruct((n_blk, width, width), BF16)
    bmat, ctm, ktoep = pl.pallas_call(
        _place_kernel,
        grid=(n_blk,),
        in_specs=[op_spec] * 4 + [c_spec] * 2,
        out_specs=[mat_spec] * 3,
        out_shape=(mat_shape,) * 3,
        scratch_shapes=[pltpu.VMEM((S5_CHUNK, LANES, SSM_GROUP), F32)],
        compiler_params=_cparams(1),
        name="s5_place_operators",
    )(lbr, lbi, clr, cli, cre, cim)

    pw8r = pw8r.reshape(SUBLANES, depth, STATE_LANES).transpose(1, 0, 2)
    pw8i = pw8i.reshape(SUBLANES, depth, STATE_LANES).transpose(1, 0, 2)
    row = jnp.arange(SUBLANES)[None, :, None]

    def step_mult(pw):
        return jnp.stack([jnp.where(row >= k, pw[:, k - 1][:, None, :], 0.0) for k in (1, 2, 4)], axis=1)

    return bmat, ktoep, ctm, step_mult(pw8r), step_mult(pw8i), pw8r, pw8i


def _heads_major(x, position_minor):
    if position_minor:
        return x.T.reshape(N_HEADS, HEAD_DIM, x.shape[0])
    heads = jnp.stack([x[:, h * HEAD_DIM:(h + 1) * HEAD_DIM] for h in range(N_HEADS)], axis=0)
    return pltpu.einshape("htd->thd", heads)


def _inproj_body(x, shift_ref, scale_ref, g_ref, w_ref, hsum_ref, qg_ref, kg_ref, outs, position_minor):
    u_ref, zs_ref, q_ref, k_ref, v_ref, kb_ref, vb_ref, za_ref = outs
    ms = jnp.mean(x * x, axis=-1, keepdims=True)
    h = x * lax.rsqrt(ms + EPS) * g_ref[...]
    h = h * (1.0 + scale_ref[0]) + shift_ref[0]
    hb = h.astype(BF16)

    def proj(c):
        return jnp.dot(hb, w_ref[:, c * SSM_WIDTH:(c + 1) * SSM_WIDTH], preferred_element_type=F32)

    def head_norm(p, g):
        sq = (p * p).astype(BF16)
        width = hsum_ref.shape[0]
        ss = jnp.concatenate([jnp.dot(sq[:, c:c + width], hsum_ref[...], preferred_element_type=F32)
                              for c in range(0, ATT_WIDTH, width)], axis=1)
        return p * lax.rsqrt(ss * (1.0 / HEAD_DIM) + EPS) * g

    u_ref[...] = proj(0)
    zs_ref[...] = _silu(proj(1)).astype(BF16)
    q = head_norm(proj(2), qg_ref[...])
    q_ref[...] = (q * (HEAD_DIM ** -0.5 * LOG2_E)).astype(BF16)
    k = head_norm(proj(3), kg_ref[...])
    k_ref[...] = _heads_major(k, position_minor).reshape(k_ref.shape)
    kb_ref[...] = k.astype(BF16)
    v = proj(4)
    v_ref[...] = _heads_major(v, position_minor).reshape(v_ref.shape)
    vb_ref[...] = v.astype(BF16)
    za_ref[...] = _silu(proj(5)).astype(BF16)


def _s5_kernel(u_ref, h0r_ref, h0i_ref, bm_ref, kt_ref, ct_ref, mr_ref, mi_ref, pr_ref, pi_ref,
               y_ref, hfr_ref, hfi_ref,
               uflat, wre, wim, hpr, hpi, *, n_seq, lane_chunk):
    rows = u_ref.shape[0]
    n_chunks = rows // S5_CHUNK
    seq_chunks = n_chunks // n_seq
    half = GROUPS_PER_LANE_BLOCK * SSM_STATE

    def token_rows(c, t):
        if n_seq == 1:
            return pl.ds(t, n_chunks, stride=S5_CHUNK)
        return pl.ds(c * S5_CHUNK + t, n_seq, stride=seq_chunks * S5_CHUNK)

    fold_blocks = [(0, slice(None))] if n_seq == 1 else [(c, slice(c * n_seq, (c + 1) * n_seq))
                                                        for c in range(seq_chunks)]
    for c, frows in fold_blocks:
        for t in range(S5_CHUNK):
            uflat[frows, t * LANES:(t + 1) * LANES] = u_ref[token_rows(c, t), :].astype(BF16)
    w = jnp.dot(uflat[...], bm_ref[...], preferred_element_type=F32)
    wre[...] = w[:, :half]
    wim[...] = w[:, half:]

    if n_seq == 1:
        hpr[0:SUBLANES, :] = jnp.broadcast_to(h0r_ref[0], (SUBLANES, half))
        hpi[0:SUBLANES, :] = jnp.broadcast_to(h0i_ref[0], (SUBLANES, half))
        for ch in range(half // lane_chunk):
            sl = slice(ch * lane_chunk, (ch + 1) * lane_chunk)
            prc = pr_ref[:, sl]
            pic = pi_ref[:, sl]

            def group_body(r, carry):
                cr, ci = carry
                row = pl.multiple_of(r * SUBLANES, SUBLANES)
                vr = wre[pl.ds(row, SUBLANES), sl]
                vi = wim[pl.ds(row, SUBLANES), sl]
                for idx, k in enumerate((1, 2, 4)):
                    mr = mr_ref[idx, :, sl]
                    mi = mi_ref[idx, :, sl]
                    sr = pltpu.roll(vr, k, 0)
                    si = pltpu.roll(vi, k, 0)
                    vr, vi = vr + (mr * sr - mi * si), vi + (mr * si + mi * sr)
                hr = vr + (prc * cr - pic * ci)
                hi = vi + (prc * ci + pic * cr)
                hpr[pl.ds(row + SUBLANES, SUBLANES), sl] = hr
                hpi[pl.ds(row + SUBLANES, SUBLANES), sl] = hi
                last = SUBLANES - 1
                return (jnp.broadcast_to(hr[last:last + 1], hr.shape),
                        jnp.broadcast_to(hi[last:last + 1], hi.shape))

            cr, ci = lax.fori_loop(0, n_chunks // SUBLANES, group_body,
                                   (hpr[0:SUBLANES, sl], hpi[0:SUBLANES, sl]), unroll=2)
            hfr_ref[0, :, sl] = cr[0:1]
            hfi_ref[0, :, sl] = ci[0:1]
        h_prev_r = hpr[SUBLANES - 1:SUBLANES - 1 + n_chunks, :]
        h_prev_i = hpi[SUBLANES - 1:SUBLANES - 1 + n_chunks, :]
    else:
        a8r = pr_ref[0:1, :]
        a8i = pi_ref[0:1, :]
        hr = h0r_ref[...].reshape(n_seq, half)
        hi = h0i_ref[...].reshape(n_seq, half)
        for c, frows in fold_blocks:
            hpr[frows, :] = hr
            hpi[frows, :] = hi
            wr = wre[frows, :]
            wi = wim[frows, :]
            hr, hi = a8r * hr - a8i * hi + wr, a8r * hi + a8i * hr + wi
        hfr_ref[...] = hr.reshape(hfr_ref.shape)
        hfi_ref[...] = hi.reshape(hfi_ref.shape)
        h_prev_r = hpr[0:n_chunks, :]
        h_prev_i = hpi[0:n_chunks, :]

    h_prev = jnp.concatenate([h_prev_r.astype(BF16), h_prev_i.astype(BF16)], axis=1)
    yf = jnp.concatenate([jnp.dot(uflat[:, :c + MXU_DIM], kt_ref[:c + MXU_DIM, c:c + MXU_DIM],
                                  preferred_element_type=F32)
                          for c in range(0, S5_CHUNK * LANES, MXU_DIM)], axis=1)
    yf = yf + lax.dot_general(h_prev, ct_ref[...], (((1,), (1,)), ((), ())), preferred_element_type=F32)
    for c, frows in fold_blocks:
        for t in range(S5_CHUNK):
            y_ref[token_rows(c, t), :] = yf[frows, t * LANES:(t + 1) * LANES]


def _s5_branch(u, h0_re, h0_im, bmat, ktoep, ctm, mr, mi, pr, pi_, layer, n_seq, seq_rows):
    n = u.shape[0]
    tile = n_seq * seq_rows
    n_total = h0_re.shape[0]
    n_chunks = tile // S5_CHUNK
    half = GROUPS_PER_LANE_BLOCK * SSM_STATE
    row_spec = pl.BlockSpec((tile, LANES), lambda lb, b: (b, lb))
    st_spec = pl.BlockSpec((n_seq, 1, half), lambda lb, b: (b, 0, lb))
    hp_rows = n_chunks + SUBLANES if n_seq == 1 else n_chunks
    mat_spec = pl.BlockSpec((None,) + bmat.shape[1:], lambda lb, b: (layer * N_LANE_BLOCKS + lb, 0, 0))
    kern = functools.partial(_s5_kernel, n_seq=n_seq, lane_chunk=256)
    return pl.pallas_call(
        kern,
        grid=(N_LANE_BLOCKS, n // tile),
        in_specs=[row_spec, st_spec, st_spec,
                  mat_spec, mat_spec, mat_spec,
                  pl.BlockSpec((None, 3, SUBLANES, half), lambda lb, b: (layer, 0, 0, lb)),
                  pl.BlockSpec((None, 3, SUBLANES, half), lambda lb, b: (layer, 0, 0, lb)),
                  pl.BlockSpec((None, SUBLANES, half), lambda lb, b: (layer, 0, lb)),
                  pl.BlockSpec((None, SUBLANES, half), lambda lb, b: (layer, 0, lb))],
        out_specs=[row_spec, st_spec, st_spec],
        out_shape=(jax.ShapeDtypeStruct((n, SSM_WIDTH), F32),
                   jax.ShapeDtypeStruct((n_total, 1, STATE_LANES), F32),
                   jax.ShapeDtypeStruct((n_total, 1, STATE_LANES), F32)),
        scratch_shapes=[pltpu.VMEM((n_chunks, S5_CHUNK * LANES), BF16),
                        pltpu.VMEM((n_chunks, half), F32), pltpu.VMEM((n_chunks, half), F32),
                        pltpu.VMEM((hp_rows, half), F32), pltpu.VMEM((hp_rows, half), F32)],
        compiler_params=_cparams(2),
        name="s5_branch",
    )(u, h0_re, h0_im, bmat, ktoep, ctm, mr, mi, pr, pi_)


def _sb_mask_queries(q_rows, qm_ref, tq):
    lane = lax.broadcasted_iota(jnp.int32, (tq, LANES), 1)
    for j in range(N_PAIRS):
        qpair = q_rows(j).astype(F32)
        for hh in range(HEADS_PER_LANE_BLOCK):
            h = j * HEADS_PER_LANE_BLOCK + hh
            in_head = (lane >= hh * HEAD_DIM) & (lane < (hh + 1) * HEAD_DIM)
            qm_ref[h * tq:(h + 1) * tq, :] = jnp.where(in_head, qpair, 0.0).astype(BF16)


class _SbTile(NamedTuple):
    tq: int
    k_tile: Any
    v_tile: Any
    keys_minor: bool
    valid: Any
    first: bool
    qm: Any
    lb: Any
    hl: Any
    a: Any
    car: Any
    acc: Any


def _sb_scores(t):
    pair = HEADS_PER_LANE_BLOCK * t.tq
    valid2 = None if t.valid is None else jnp.concatenate([t.valid] * HEADS_PER_LANE_BLOCK, axis=0)
    for j in range(N_PAIRS):
        rows = slice(j * pair, (j + 1) * pair)
        contract_k = 0 if t.keys_minor else 1
        z = lax.dot_general(t.qm[rows, :], t.k_tile(j), (((1,), (contract_k,)), ((), ())),
                            preferred_element_type=F32)
        sp = jnp.maximum(z, 0.0) + jnp.log2(1.0 + jnp.exp2(-jnp.abs(z)))
        t.lb[rows, :] = z - sp
        if valid2 is not None:
            sp = jnp.where(valid2, sp, 0.0)
        hi = sp.astype(BF16)
        t.hl[rows, :KEY_TILE] = hi
        t.hl[rows, KEY_TILE:] = (sp - hi.astype(F32)).astype(BF16)


def _sb_weights(t, tri_ref):
    cs = jnp.dot(t.hl[...], tri_ref[...], preferred_element_type=F32)
    cmax = None
    for h in range(N_HEADS):
        rows = slice(h * t.tq, (h + 1) * t.tq)
        after = cs[rows, :KEY_TILE]
        total = cs[rows, KEY_TILE:]
        if not t.first:
            carry = t.car[rows, :]
            after = after + carry
            total = total + carry
        a = jnp.exp2(t.lb[rows, :] + after)
        if t.valid is not None:
            a = jnp.where(t.valid, a, 0.0)
        t.a[rows, :] = a.astype(BF16)
        t.car[rows, :] = total
        cmax = total if cmax is None else jnp.maximum(cmax, total)
    return jnp.max(cmax)


def _sb_output(t):
    pair = HEADS_PER_LANE_BLOCK * t.tq
    lane = lax.broadcasted_iota(jnp.int32, (t.tq, LANES), 1)
    for j in range(N_PAIRS):
        contract_v = 1 if t.keys_minor else 0
        pv = lax.dot_general(t.a[j * pair:(j + 1) * pair, :], t.v_tile(j), (((1,), (contract_v,)), ((), ())),
                             preferred_element_type=F32)
        out = jnp.where(lane < HEAD_DIM, pv[:t.tq], pv[t.tq:])
        if t.first:
            t.acc[:, j * LANES:(j + 1) * LANES] = out
        else:
            t.acc[:, j * LANES:(j + 1) * LANES] += out


def _sb_tiles(tiles, tri_ref):
    for t in tiles:
        _sb_scores(t)
    ms = [_sb_weights(t, tri_ref) for t in tiles]
    for t in tiles:
        _sb_output(t)
    return ms


def _sb_scratch(n_blocks, tq):
    rows = N_HEADS * tq
    return [pltpu.VMEM((n_blocks, rows, LANES), BF16),
            pltpu.VMEM((n_blocks, rows, LANES), F32),
            pltpu.VMEM((n_blocks, rows, 2 * KEY_TILE), BF16),
            pltpu.VMEM((n_blocks, rows, LANES), BF16),
            pltpu.VMEM((n_blocks, rows, LANES), F32),
            pltpu.VMEM((n_blocks, tq, ATT_WIDTH), F32),
            pltpu.SMEM((n_blocks,), F32)]


def _sb_prompt_kernel(q_ref, k_ref, v_ref, za_ref, tri_ref, o_ref,
                      qm_ref, lb_ref, hl_ref, a_ref, car_ref, acc_ref, m_ref):
    tq = SB_Q_ROWS
    n_blocks = q_ref.shape[0] // tq
    step_row0 = pl.program_id(1) * (n_blocks * tq)
    row = lax.broadcasted_iota(jnp.int32, (tq, KEY_TILE), 0)
    col = lax.broadcasted_iota(jnp.int32, (tq, KEY_TILE), 1)

    def tiles(ref, start):
        return lambda j: ref[pl.ds(start, KEY_TILE), j * LANES:(j + 1) * LANES]

    def band_start(s):
        return pl.multiple_of(jnp.maximum(step_row0 + (s + 1) * tq - KEY_TILE, 0), tq)

    def tile(s, lo, valid, first):
        return _SbTile(tq, tiles(k_ref, lo), tiles(v_ref, lo), False, valid, first, qm_ref.at[s], lb_ref.at[s],
                       hl_ref.at[s], a_ref.at[s], car_ref.at[s], acc_ref.at[s])

    band = []
    for s in range(n_blocks):
        lo = band_start(s)
        _sb_mask_queries(lambda j: q_ref[s * tq:(s + 1) * tq, j * LANES:(j + 1) * LANES], qm_ref.at[s], tq)
        valid = (col - row) < (step_row0 + s * tq - lo)
        band.append(tile(s, lo, valid, True))
    for s, m in enumerate(_sb_tiles(band, tri_ref)):
        m_ref[s] = m

    def more_tiles(s, _):
        def cond(state):
            hi, m = state
            return (hi > 0) & (m > SB_LOG2_CUTOFF)

        def body(state):
            hi, _ = state
            lo = pl.multiple_of(jnp.maximum(hi - KEY_TILE, 0), tq)
            valid = col < (hi - lo)
            (m,) = _sb_tiles([tile(s, lo, valid, False)], tri_ref)
            return lo, m

        lax.while_loop(cond, body, (band_start(s), m_ref[s]))
        return 0

    lax.fori_loop(0, n_blocks, more_tiles, 0)
    for s in range(n_blocks):
        rows = slice(s * tq, (s + 1) * tq)
        o_ref[rows, :] = (acc_ref[s] * za_ref[rows, :].astype(F32)).astype(BF16)


def _sb_prompt(q, kb, vb, za, tri, n_batch, seq_len):
    step_rows = SB_STEP_BLOCKS * SB_Q_ROWS
    nq = seq_len // step_rows
    row_spec = pl.BlockSpec((step_rows, ATT_WIDTH), lambda b, i: (b * nq + i, 0))
    seq_spec = pl.BlockSpec((seq_len, ATT_WIDTH), lambda b, i: (b, 0))
    return pl.pallas_call(
        _sb_prompt_kernel,
        grid=(n_batch, nq),
        in_specs=[row_spec, seq_spec, seq_spec, row_spec, pl.BlockSpec(tri.shape, lambda b, i: (0, 0))],
        out_specs=row_spec,
        out_shape=jax.ShapeDtypeStruct(q.shape, BF16),
        scratch_shapes=_sb_scratch(SB_STEP_BLOCKS, SB_Q_ROWS),
        compiler_params=_cparams(2),
        name="sb_prompt",
    )(q, kb, vb, za, tri)


def _sb_sample_kernel(q_ref, kn_ref, vn_ref, kc_ref, vc_ref, za_ref, tri_ref, o_ref,
                      qm_ref, lb_ref, hl_ref, a_ref, car_ref, acc_ref, m_ref):
    tq = q_ref.shape[0]
    n_past = kc_ref.shape[-1] // KEY_TILE
    _sb_mask_queries(lambda j: q_ref[:, j * LANES:(j + 1) * LANES], qm_ref.at[0], tq)
    row = lax.broadcasted_iota(jnp.int32, (tq, KEY_TILE), 0)
    col = lax.broadcasted_iota(jnp.int32, (tq, KEY_TILE), 1)
    scratch = (qm_ref.at[0], lb_ref.at[0], hl_ref.at[0], a_ref.at[0], car_ref.at[0], acc_ref.at[0])

    new_tile = lambda ref: (lambda j: ref[:, j * LANES:(j + 1) * LANES])
    (m0,) = _sb_tiles([_SbTile(tq, new_tile(kn_ref), new_tile(vn_ref), False, col < row, True, *scratch)],
                      tri_ref)

    def past_tile(ref, t):
        start = pl.multiple_of(t * KEY_TILE, KEY_TILE)
        return lambda j: ref[0, 0, j * HEADS_PER_LANE_BLOCK:(j + 1) * HEADS_PER_LANE_BLOCK, :,
                             pl.ds(start, KEY_TILE)].reshape(LANES, KEY_TILE).astype(BF16)

    def cond(state):
        t, m = state
        return (t >= 0) & (m > SB_LOG2_CUTOFF)

    def body(state):
        t, _ = state
        (m,) = _sb_tiles([_SbTile(tq, past_tile(kc_ref, t), past_tile(vc_ref, t), True, None, False,
                                  *scratch)], tri_ref)
        return t - 1, m

    lax.while_loop(cond, body, (n_past - 1, m0))
    o_ref[...] = (acc_ref[0] * za_ref[...].astype(F32)).astype(BF16)


def _sb_sample(q, k_new, v_new, k_past, v_past, layer, za, tri, n_batch, t_new):
    past = k_past.shape[-1]
    row_spec = pl.BlockSpec((t_new, ATT_WIDTH), lambda b: (b, 0))
    new_spec = pl.BlockSpec((KEY_TILE, ATT_WIDTH), lambda b: (b, 0))
    past_spec = pl.BlockSpec((1, 1, N_HEADS, HEAD_DIM, past), lambda b: (layer, b, 0, 0, 0))
    return pl.pallas_call(
        _sb_sample_kernel,
        grid=(n_batch,),
        in_specs=[row_spec, new_spec, new_spec, past_spec, past_spec, row_spec,
                  pl.BlockSpec(tri.shape, lambda b: (0, 0))],
        out_specs=row_spec,
        out_shape=jax.ShapeDtypeStruct(q.shape, BF16),
        scratch_shapes=_sb_scratch(1, t_new),
        compiler_params=_cparams(1),
        name="sb_sample",
    )(q, k_new, v_new, k_past, v_past, za, tri)


def _outproj_body(x, ysc_ref, u_ref, zs_ref, ya_ref, gate_ref, d_ref, wglu_ref, bglu_ref, w_ref):
    y = ysc_ref[...] + d_ref[...] * u_ref[...]
    g = _gelu_tanh(y)
    glu = _sigmoid(jnp.dot(g.astype(BF16), wglu_ref[...], preferred_element_type=F32) + bglu_ref[...])
    ys = (g * glu * zs_ref[...].astype(F32)).astype(BF16)
    mix = jnp.dot(ys, w_ref[:SSM_WIDTH, :], preferred_element_type=F32)
    mix = mix + jnp.dot(ya_ref[...], w_ref[SSM_WIDTH:, :], preferred_element_type=F32)
    return x + gate_ref[0] * mix


class _OutProj(NamedTuple):
    ysc: Any
    u: Any
    zs: Any
    ya: Any
    gate: Any
    d_skip: Any
    w_glu_b: Any
    b_glu: Any
    w_out_b: Any


class _InProj(NamedTuple):
    shift: Any
    scale: Any
    norm_g: Any
    w_in_b: Any
    hsum: Any
    q_g: Any
    k_g: Any
    layer: int
    kv_all: Any


N_OUTPROJ_REFS = len(_OutProj._fields)
N_INPROJ_REFS = len(_InProj._fields) - 2


def _layer_kernel(x_ref, *refs, closes, opens, position_minor):
    refs = list(refs)
    x = x_ref[...]
    if closes:
        out_in, refs = refs[:N_OUTPROJ_REFS], refs[N_OUTPROJ_REFS:]
    if opens:
        in_in, refs = refs[:N_INPROJ_REFS], refs[N_INPROJ_REFS + 2:]
    if closes:
        x = _outproj_body(x, *out_in)
        refs.pop(0)[...] = x
    if opens:
        _inproj_body(x, *in_in, refs, position_minor)


def _layer_call(x2d, closing, opening, tm, rows_per_seq, depth):
    n = x2d.shape[0]
    if rows_per_seq is None:
        mod_spec = pl.BlockSpec((1, tm, D_MODEL), lambda i: (0, i, 0))
    else:
        tps = rows_per_seq // tm
        mod_spec = pl.BlockSpec((1, 1, D_MODEL), lambda i: (i // tps, 0, 0))
    row_spec = lambda w: pl.BlockSpec((tm, w), lambda i: (i, 0))
    full = lambda a: pl.BlockSpec(a.shape, lambda i: (0,) * a.ndim)
    position_minor = rows_per_seq is not None
    args, in_specs, out_specs, out_shape, aliases = [x2d], [row_spec(D_MODEL)], [], [], {}
    if closing is not None:
        args += list(closing)
        in_specs += [row_spec(SSM_WIDTH)] * 3 + [row_spec(ATT_WIDTH), mod_spec] + [full(a) for a in closing[5:]]
        out_specs.append(row_spec(D_MODEL))
        out_shape.append(jax.ShapeDtypeStruct(x2d.shape, F32))
    if opening is not None:
        layer = opening.layer
        bf = jax.ShapeDtypeStruct((n, SSM_WIDTH), BF16)
        if position_minor:
            f3 = jax.ShapeDtypeStruct((depth, n // rows_per_seq, N_HEADS, HEAD_DIM, rows_per_seq), F32)
            kv_spec = pl.BlockSpec((1, 1, N_HEADS, HEAD_DIM, tm), lambda i: (layer, i // tps, 0, 0, i % tps))
        else:
            f3 = jax.ShapeDtypeStruct((depth, n, N_HEADS, HEAD_DIM), F32)
            kv_spec = pl.BlockSpec((1, tm, N_HEADS, HEAD_DIM), lambda i: (layer, i, 0, 0))
        assert opening.kv_all[0].shape == f3.shape and opening.kv_all[1].shape == f3.shape
        args += list(opening[:N_INPROJ_REFS]) + list(opening.kv_all)
        in_specs += [mod_spec, mod_spec] + [full(a) for a in opening[2:N_INPROJ_REFS]]
        in_specs += [pl.BlockSpec(memory_space=pl.ANY)] * 2
        aliases = {len(args) - 2: len(out_shape) + 3, len(args) - 1: len(out_shape) + 4}
        out_specs += [row_spec(SSM_WIDTH)] * 3 + [kv_spec] * 2 + [row_spec(SSM_WIDTH)] * 3
        out_shape += [jax.ShapeDtypeStruct((n, SSM_WIDTH), F32), bf, bf, f3, f3, bf, bf, bf]
    outs = pl.pallas_call(
        functools.partial(_layer_kernel, closes=closing is not None, opens=opening is not None,
                          position_minor=position_minor),
        grid=(n // tm,),
        in_specs=in_specs,
        out_specs=out_specs,
        out_shape=out_shape,
        input_output_aliases=aliases,
        compiler_params=_cparams(1),
        name="layer_" + "_".join(["close"] * (closing is not None) + ["open"] * (opening is not None)),
    )(*args)
    x_new = outs[0] if closing is not None else None
    opened = tuple(outs[len(outs) - 8:]) if opening is not None else None
    return x_new, opened


def _sb_sum_matrix():
    j = jnp.arange(2 * KEY_TILE)[:, None] % KEY_TILE
    s = jnp.arange(2 * KEY_TILE)[None, :]
    return -((s >= KEY_TILE) | (j > s)).astype(BF16)


def _head_sum_matrix():
    a = jnp.arange(MXU_DIM)
    return (a[:, None] // HEAD_DIM == a[None, :] // HEAD_DIM).astype(BF16)


def kernel(x_prompt, x_sample, c_prompt, c_sample, cache_k, cache_v, state_ssm_re, state_ssm_im, norm_g, w_mod, b_mod, w_in, ssm_a_re, ssm_a_im, ssm_log_dt, ssm_b_re, ssm_b_im, ssm_c_re, ssm_c_im, ssm_d, w_glu, b_glu, q_norm_g, k_norm_g, w_out):
    depth = w_in.shape[0]
    nb, seq_len, _ = x_prompt.shape
    ns, t_new, _ = x_sample.shape
    tm = 512

    mod = _modulation(jnp.concatenate([c_prompt, c_sample], axis=0), w_mod, b_mod)
    s5_ops = _s5_operators(ssm_a_re, ssm_a_im, ssm_log_dt, ssm_b_re, ssm_b_im, ssm_c_re, ssm_c_im)

    w_in_b = w_in.astype(BF16)
    w_out_b = w_out.astype(BF16)
    w_glu_b = w_glu.astype(BF16)
    tri = _sb_sum_matrix()
    hsum = _head_sum_matrix()

    xp = x_prompt.reshape(nb * seq_len, D_MODEL)
    xs = x_sample.reshape(ns * t_new, D_MODEL)
    zero_state = jnp.zeros((nb, 1, STATE_LANES), F32)
    kc = jnp.transpose(cache_k, (0, 1, 3, 4, 2))
    vc = jnp.transpose(cache_v, (0, 1, 3, 4, 2))
    outs = {name: [] for name in ("pr", "pi", "sr", "si")}
    pkv = tuple(jnp.zeros((depth, nb, N_HEADS, HEAD_DIM, seq_len), F32) for _ in range(2))
    skv = tuple(jnp.zeros((depth, ns * t_new, N_HEADS, HEAD_DIM), F32) for _ in range(2))
    def in_proj_args(l, shift, scale, kv_all):
        return _InProj(shift, scale, norm_g[l].reshape(1, D_MODEL), w_in_b[l], hsum,
                       jnp.tile(q_norm_g[l], N_HEADS).reshape(1, ATT_WIDTH),
                       jnp.tile(k_norm_g[l], N_HEADS).reshape(1, ATT_WIDTH), l, kv_all)

    def prompt_mod(l):
        mp = mod[l, :nb].reshape(nb, 1, 3 * D_MODEL)
        return tuple(mp[:, :, i * D_MODEL:(i + 1) * D_MODEL] for i in range(3))

    shift, scale, gate = prompt_mod(0)
    _, opened = _layer_call(xp, None, in_proj_args(0, shift, scale, pkv), tm, seq_len, depth)
    for l in range(depth):
        glu_w = (ssm_d[l].reshape(1, SSM_WIDTH), w_glu_b[l], b_glu[l].reshape(1, SSM_WIDTH))

        u, zs, q, k_all, v_all, kb, vb, za = opened
        pkv = (k_all, v_all)
        ysc, hfr, hfi = _s5_branch(u, zero_state, zero_state, *s5_ops, l, 1, seq_len)
        ya = _sb_prompt(q, kb, vb, za, tri, nb, seq_len)
        closing = _OutProj(ysc, u, zs, ya, gate, *glu_w, w_out_b[l])
        opening = None
        if l + 1 < depth:
            shift, scale, gate = prompt_mod(l + 1)
            opening = in_proj_args(l + 1, shift, scale, pkv)
        xp, opened = _layer_call(xp, closing, opening, tm, seq_len, depth)
        outs["pr"].append(hfr.reshape(nb, SSM_GROUPS, SSM_STATE))
        outs["pi"].append(hfi.reshape(nb, SSM_GROUPS, SSM_STATE))

        ms = jnp.repeat(mod[l, nb:], t_new, axis=0).reshape(1, ns * t_new, 3 * D_MODEL)
        s_shift, s_scale, s_gate = (ms[:, :, i * D_MODEL:(i + 1) * D_MODEL] for i in range(3))
        ts = ns * t_new
        _, (u, zs, q, k_all, v_all, kb, vb, za) = _layer_call(
            xs, None, in_proj_args(l, s_shift, s_scale, skv), ts, None, depth)
        skv = (k_all, v_all)
        h0r = state_ssm_re[l].reshape(ns, 1, STATE_LANES)
        h0i = state_ssm_im[l].reshape(ns, 1, STATE_LANES)
        ysc, hfr, hfi = _s5_branch(u, h0r, h0i, *s5_ops, l, ns, t_new)
        pad = lambda a: jnp.pad(a.reshape(ns, t_new, ATT_WIDTH),
                                ((0, 0), (0, KEY_TILE - t_new), (0, 0))).reshape(ns * KEY_TILE, ATT_WIDTH)
        ya = _sb_sample(q, pad(kb), pad(vb), kc, vc, l, za, tri, ns, t_new)
        xs, _ = _layer_call(xs, _OutProj(ysc, u, zs, ya, s_gate, *glu_w, w_out_b[l]), None, ts, None, depth)
        outs["sr"].append(hfr.reshape(ns, SSM_GROUPS, SSM_STATE))
        outs["si"].append(hfi.reshape(ns, SSM_GROUPS, SSM_STATE))

    st = lambda name: jnp.stack(outs[name])
    heads_p = lambda a: jnp.transpose(a, (0, 1, 4, 2, 3))
    heads_s = lambda a: a.reshape(depth, ns, t_new, N_HEADS, HEAD_DIM)
    return (xp.reshape(nb, seq_len, D_MODEL), xs.reshape(ns, t_new, D_MODEL),
            heads_p(pkv[0]), heads_p(pkv[1]), st("pr"), st("pi"),
            heads_s(skv[0]), heads_s(skv[1]), st("sr"), st("si"))
```

```python
import functools
import math
from typing import Any, NamedTuple

import jax
import jax.numpy as jnp
from jax import lax
from jax.experimental import pallas as pl
from jax.experimental.pallas import tpu as pltpu

F32 = jnp.float32
BF16 = jnp.bfloat16

D_MODEL = 1024
SSM_WIDTH = 512
SSM_GROUP = 16
SSM_GROUPS = 32
SSM_STATE = 64
STATE_LANES = SSM_GROUPS * SSM_STATE
ATT_WIDTH = 512
HEAD_DIM = 64
N_HEADS = 8
IN_WIDTH = 2 * SSM_WIDTH + 4 * ATT_WIDTH
EPS = 1e-6

LANES = 128
SUBLANES = 8
MXU_DIM = 256
HEADS_PER_LANE_BLOCK = LANES // HEAD_DIM
GROUPS_PER_LANE_BLOCK = LANES // SSM_GROUP
N_LANE_BLOCKS = SSM_WIDTH // LANES
N_PAIRS = N_HEADS // HEADS_PER_LANE_BLOCK
S5_CHUNK = SUBLANES
KEY_TILE = 128
SB_Q_ROWS = 32
SB_STEP_BLOCKS = 16
LOG2_E = math.log2(math.e)
SB_LOG2_CUTOFF = -40.0 * LOG2_E
VMEM_LIMIT = 56 * 1024 * 1024


def _cparams(n_axes):
    return pltpu.CompilerParams(dimension_semantics=("arbitrary",) * n_axes,
                                vmem_limit_bytes=VMEM_LIMIT)


def _silu(x):
    return x * (1.0 / (1.0 + jnp.exp(-x)))


def _sigmoid(x):
    return 1.0 / (1.0 + jnp.exp(-x))


def _gelu_tanh(x):
    c = math.sqrt(2.0 / math.pi)
    return 0.5 * x * (1.0 + jnp.tanh(c * (x + 0.044715 * (x * x * x))))


def _mod_kernel(c_ref, w_ref, b_ref, o_ref):
    c = c_ref[...]
    a = _silu(c)
    o_ref[0] = jnp.dot(a, w_ref[0], preferred_element_type=F32,
                       precision=lax.Precision.HIGHEST) + b_ref[0]


def _modulation(c_all, w_mod, b_mod):
    depth = w_mod.shape[0]
    n = c_all.shape[0]
    nt = 3
    return pl.pallas_call(
        _mod_kernel,
        grid=(depth, nt),
        in_specs=[pl.BlockSpec((n, D_MODEL), lambda l, j: (0, 0)),
                  pl.BlockSpec((1, D_MODEL, D_MODEL), lambda l, j: (l, 0, j)),
                  pl.BlockSpec((1, 1, D_MODEL), lambda l, j: (l, 0, j))],
        out_specs=pl.BlockSpec((1, n, D_MODEL), lambda l, j: (l, 0, j)),
        out_shape=jax.ShapeDtypeStruct((depth, n, 3 * D_MODEL), F32),
        compiler_params=_cparams(2),
        name="modulation",
    )(c_all, w_mod, b_mod.reshape(depth, 1, 3 * D_MODEL))


def _disc_kernel(lr_ref, li_ref, ldt_ref, bre_ref, bim_ref, cre_ref, cim_ref,
                 pw8r_ref, pw8i_ref, lbr_ref, lbi_ref, clr_ref, cli_ref):
    lr = lr_ref[...]
    li = li_ref[...]
    dt = jnp.exp(ldt_ref[...])
    mag = jnp.exp(lr * dt)
    ang = li * dt
    ab_re = mag * jnp.cos(ang)
    ab_im = mag * jnp.sin(ang)
    den = lr * lr + li * li
    nr = ab_re - 1.0
    f_re = (nr * lr + ab_im * li) / den
    f_im = (ab_im * lr - nr * li) / den
    bre = bre_ref[...]
    bim = bim_ref[...]
    cre = cre_ref[...]
    cim = cim_ref[...]
    xr = f_re[:, None, :] * bre - f_im[:, None, :] * bim
    xi = f_re[:, None, :] * bim + f_im[:, None, :] * bre
    pr, pi_ = ab_re, ab_im
    for t in range(S5_CHUNK):
        lbr_ref[t] = xr
        lbi_ref[t] = xi
        clr_ref[t] = cre * pr[:, None, :] - cim * pi_[:, None, :]
        cli_ref[t] = cre * pi_[:, None, :] + cim * pr[:, None, :]
        if t + 1 < S5_CHUNK:
            xr, xi = (xr * ab_re[:, None, :] - xi * ab_im[:, None, :],
                      xr * ab_im[:, None, :] + xi * ab_re[:, None, :])
            pr, pi_ = pr * ab_re - pi_ * ab_im, pr * ab_im + pi_ * ab_re
    a8r, a8i = pr, pi_
    qr, qi = a8r, a8i
    for m in range(SUBLANES):
        pw8r_ref[m] = qr
        pw8i_ref[m] = qi
        qr, qi = qr * a8r - qi * a8i, qr * a8i + qi * a8r


def _place_kernel(lbr_ref, lbi_ref, clr_ref, cli_ref, cre_ref, cim_ref, bm_ref, ct_ref, kt_ref, kcat):
    gb = GROUPS_PER_LANE_BLOCK
    half = gb * SSM_STATE
    rep_p = (lax.broadcasted_iota(jnp.int32, (SSM_STATE, half), 0)
             == lax.broadcasted_iota(jnp.int32, (SSM_STATE, half), 1) % SSM_STATE).astype(BF16)
    rep_c = (lax.broadcasted_iota(jnp.int32, (SSM_GROUP, LANES), 0)
             == lax.broadcasted_iota(jnp.int32, (SSM_GROUP, LANES), 1) % SSM_GROUP).astype(BF16)
    same_p = (lax.broadcasted_iota(jnp.int32, (LANES, half), 0) // SSM_GROUP
              == lax.broadcasted_iota(jnp.int32, (LANES, half), 1) // SSM_STATE)
    same_c = (lax.broadcasted_iota(jnp.int32, (LANES, LANES), 0) // SSM_GROUP
              == lax.broadcasted_iota(jnp.int32, (LANES, LANES), 1) // SSM_GROUP)

    def place_states(x):
        tiled = jnp.dot(x.reshape(LANES, SSM_STATE).astype(BF16), rep_p, preferred_element_type=F32)
        return jnp.where(same_p, tiled, 0.0).astype(BF16)

    for s in range(S5_CHUNK):
        rows = slice(s * LANES, (s + 1) * LANES)
        bm_ref[0, rows, :half] = place_states(lbr_ref[S5_CHUNK - 1 - s])
        bm_ref[0, rows, half:] = place_states(lbi_ref[S5_CHUNK - 1 - s])
        ct_ref[0, rows, :half] = place_states(clr_ref[s])
        ct_ref[0, rows, half:] = place_states(-cli_ref[s])

    for g in range(gb):
        b_cat = jnp.concatenate([lbr_ref[:, g].reshape(LANES, SSM_STATE),
                                 lbi_ref[:, g].reshape(LANES, SSM_STATE)], axis=1)
        c_cat = jnp.concatenate([cre_ref[g], -cim_ref[g]], axis=1)
        kg = lax.dot_general(b_cat, c_cat, (((1,), (1,)), ((), ())), preferred_element_type=F32,
                             precision=lax.Precision.HIGHEST)
        for j in range(S5_CHUNK):
            kcat[j, g * SSM_GROUP:(g + 1) * SSM_GROUP, :] = kg[j * SSM_GROUP:(j + 1) * SSM_GROUP, :]
    zero = jnp.zeros((LANES, LANES), BF16)
    for j in range(S5_CHUNK):
        tiled = jnp.dot(kcat[j].astype(BF16), rep_c, preferred_element_type=F32)
        blk = jnp.where(same_c, tiled, 0.0).astype(BF16)
        for s in range(S5_CHUNK - j):
            kt_ref[0, s * LANES:(s + 1) * LANES, (s + j) * LANES:(s + j + 1) * LANES] = blk
        if j > 0:
            for t in range(S5_CHUNK - j):
                kt_ref[0, (t + j) * LANES:(t + j + 1) * LANES, t * LANES:(t + 1) * LANES] = zero


def _s5_operators(a_re, a_im, log_dt, b_re, b_im, c_re, c_im):
    depth = a_re.shape[0]
    rows = depth * SSM_GROUPS
    lr = a_re.reshape(rows, SSM_STATE)
    li = a_im.reshape(rows, SSM_STATE)
    ldt = jnp.broadcast_to(log_dt.reshape(rows, 1), (rows, SSM_STATE))
    bre = b_re.transpose(0, 1, 3, 2).reshape(rows, SSM_GROUP, SSM_STATE)
    bim = b_im.transpose(0, 1, 3, 2).reshape(rows, SSM_GROUP, SSM_STATE)
    cre = c_re.reshape(rows, SSM_GROUP, SSM_STATE)
    cim = c_im.reshape(rows, SSM_GROUP, SSM_STATE)
    pw_shape = jax.ShapeDtypeStruct((SUBLANES, rows, SSM_STATE), F32)
    op_shape = jax.ShapeDtypeStruct((S5_CHUNK, rows, SSM_GROUP, SSM_STATE), F32)
    pw8r, pw8i, lbr, lbi, clr, cli = pl.pallas_call(
        _disc_kernel,
        out_shape=(pw_shape, pw_shape, op_shape, op_shape, op_shape, op_shape),
        compiler_params=_cparams(0),
        name="s5_discretise",
    )(lr, li, ldt, bre, bim, cre, cim)
    gb = GROUPS_PER_LANE_BLOCK
    n_blk = rows // gb
    width = S5_CHUNK * LANES
    op_spec = pl.BlockSpec((S5_CHUNK, gb, SSM_GROUP, SSM_STATE), lambda i: (0, i, 0, 0))
    c_spec = pl.BlockSpec((gb, SSM_GROUP, SSM_STATE), lambda i: (i, 0, 0))
    mat_spec = pl.BlockSpec((1, width, width), lambda i: (i, 0, 0))
    mat_shape = jax.ShapeDtypeStruct((n_blk, width, width), BF16)
    bmat, ctm, ktoep = pl.pallas_call(
        _place_kernel,
        grid=(n_blk,),
        in_specs=[op_spec] * 4 + [c_spec] * 2,
        out_specs=[mat_spec] * 3,
        out_shape=(mat_shape,) * 3,
        scratch_shapes=[pltpu.VMEM((S5_CHUNK, LANES, SSM_GROUP), F32)],
        compiler_params=_cparams(1),
        name="s5_place_operators",
    )(lbr, lbi, clr, cli, cre, cim)

    pw8r = pw8r.reshape(SUBLANES, depth, STATE_LANES).transpose(1, 0, 2)
    pw8i = pw8i.reshape(SUBLANES, depth, STATE_LANES).transpose(1, 0, 2)
    row = jnp.arange(SUBLANES)[None, :, None]

    def step_mult(pw):
        return jnp.stack([jnp.where(row >= k, pw[:, k - 1][:, None, :], 0.0) for k in (1, 2, 4)], axis=1)

    return bmat, ktoep, ctm, step_mult(pw8r), step_mult(pw8i), pw8r, pw8i


def _heads_major(x, position_minor):
    if position_minor:
        return x.T.reshape(N_HEADS, HEAD_DIM, x.shape[0])
    heads = jnp.stack([x[:, h * HEAD_DIM:(h + 1) * HEAD_DIM] for h in range(N_HEADS)], axis=0)
    return pltpu.einshape("htd->thd", heads)


def _inproj_body(x, shift_ref, scale_ref, g_ref, w_ref, hsum_ref, qg_ref, kg_ref, outs, kv_slot,
                 position_minor):
    u_ref, zs_ref, q_ref, k_ref, v_ref, kb_ref, vb_ref, za_ref = outs

    def store_kv(ref, value):
        slot = kv_slot if ref.shape[0] > 1 else 0
        for s in range(ref.shape[0]):
            ref[s] = (_heads_major(value, position_minor).reshape(ref.shape[1:]) if s == slot
                      else jnp.zeros(ref.shape[1:], F32))

    ms = jnp.mean(x * x, axis=-1, keepdims=True)
    h = x * lax.rsqrt(ms + EPS) * g_ref[...]
    h = h * (1.0 + scale_ref[0]) + shift_ref[0]
    hb = h.astype(BF16)

    def proj(c):
        return jnp.dot(hb, w_ref[:, c * SSM_WIDTH:(c + 1) * SSM_WIDTH], preferred_element_type=F32)

    def head_norm(p, g):
        sq = (p * p).astype(BF16)
        width = hsum_ref.shape[0]
        ss = jnp.concatenate([jnp.dot(sq[:, c:c + width], hsum_ref[...], preferred_element_type=F32)
                              for c in range(0, ATT_WIDTH, width)], axis=1)
        return p * lax.rsqrt(ss * (1.0 / HEAD_DIM) + EPS) * g

    u_ref[...] = proj(0)
    zs_ref[...] = _silu(proj(1)).astype(BF16)
    q = head_norm(proj(2), qg_ref[...])
    q_ref[...] = (q * (HEAD_DIM ** -0.5 * LOG2_E)).astype(BF16)
    k = head_norm(proj(3), kg_ref[...])
    store_kv(k_ref, k)
    kb_ref[...] = k.astype(BF16)
    v = proj(4)
    store_kv(v_ref, v)
    vb_ref[...] = v.astype(BF16)
    za_ref[...] = _silu(proj(5)).astype(BF16)


def _s5_kernel(u_ref, h0r_ref, h0i_ref, bm_ref, kt_ref, ct_ref, mr_ref, mi_ref, pr_ref, pi_ref,
               y_ref, hfr_ref, hfi_ref,
               uflat, wre, wim, hpr, hpi, *, n_seq, lane_chunk):
    rows = u_ref.shape[0]
    n_chunks = rows // S5_CHUNK
    seq_chunks = n_chunks // n_seq
    half = GROUPS_PER_LANE_BLOCK * SSM_STATE

    def token_rows(c, t):
        if n_seq == 1:
            return pl.ds(t, n_chunks, stride=S5_CHUNK)
        return pl.ds(c * S5_CHUNK + t, n_seq, stride=seq_chunks * S5_CHUNK)

    fold_blocks = [(0, slice(None))] if n_seq == 1 else [(c, slice(c * n_seq, (c + 1) * n_seq))
                                                        for c in range(seq_chunks)]
    for c, frows in fold_blocks:
        for t in range(S5_CHUNK):
            uflat[frows, t * LANES:(t + 1) * LANES] = u_ref[token_rows(c, t), :].astype(BF16)
    w = jnp.dot(uflat[...], bm_ref[...], preferred_element_type=F32)
    wre[...] = w[:, :half]
    wim[...] = w[:, half:]

    if n_seq == 1:
        hpr[0:SUBLANES, :] = jnp.broadcast_to(h0r_ref[0], (SUBLANES, half))
        hpi[0:SUBLANES, :] = jnp.broadcast_to(h0i_ref[0], (SUBLANES, half))
        for ch in range(half // lane_chunk):
            sl = slice(ch * lane_chunk, (ch + 1) * lane_chunk)
            prc = pr_ref[:, sl]
            pic = pi_ref[:, sl]

            def group_body(r, carry):
                cr, ci = carry
                row = pl.multiple_of(r * SUBLANES, SUBLANES)
                vr = wre[pl.ds(row, SUBLANES), sl]
                vi = wim[pl.ds(row, SUBLANES), sl]
                for idx, k in enumerate((1, 2, 4)):
                    mr = mr_ref[idx, :, sl]
                    mi = mi_ref[idx, :, sl]
                    sr = pltpu.roll(vr, k, 0)
                    si = pltpu.roll(vi, k, 0)
                    vr, vi = vr + (mr * sr - mi * si), vi + (mr * si + mi * sr)
                hr = vr + (prc * cr - pic * ci)
                hi = vi + (prc * ci + pic * cr)
                hpr[pl.ds(row + SUBLANES, SUBLANES), sl] = hr
                hpi[pl.ds(row + SUBLANES, SUBLANES), sl] = hi
                last = SUBLANES - 1
                return (jnp.broadcast_to(hr[last:last + 1], hr.shape),
                        jnp.broadcast_to(hi[last:last + 1], hi.shape))

            cr, ci = lax.fori_loop(0, n_chunks // SUBLANES, group_body,
                                   (hpr[0:SUBLANES, sl], hpi[0:SUBLANES, sl]), unroll=2)
            hfr_ref[0, :, sl] = cr[0:1]
            hfi_ref[0, :, sl] = ci[0:1]
        h_prev_r = hpr[SUBLANES - 1:SUBLANES - 1 + n_chunks, :]
        h_prev_i = hpi[SUBLANES - 1:SUBLANES - 1 + n_chunks, :]
    else:
        a8r = pr_ref[0:1, :]
        a8i = pi_ref[0:1, :]
        hr = h0r_ref[...].reshape(n_seq, half)
        hi = h0i_ref[...].reshape(n_seq, half)
        for c, frows in fold_blocks:
            hpr[frows, :] = hr
            hpi[frows, :] = hi
            wr = wre[frows, :]
            wi = wim[frows, :]
            hr, hi = a8r * hr - a8i * hi + wr, a8r * hi + a8i * hr + wi
        hfr_ref[...] = hr.reshape(hfr_ref.shape)
        hfi_ref[...] = hi.reshape(hfi_ref.shape)
        h_prev_r = hpr[0:n_chunks, :]
        h_prev_i = hpi[0:n_chunks, :]

    h_prev = jnp.concatenate([h_prev_r.astype(BF16), h_prev_i.astype(BF16)], axis=1)
    yf = jnp.concatenate([jnp.dot(uflat[:, :c + MXU_DIM], kt_ref[:c + MXU_DIM, c:c + MXU_DIM],
                                  preferred_element_type=F32)
                          for c in range(0, S5_CHUNK * LANES, MXU_DIM)], axis=1)
    yf = yf + lax.dot_general(h_prev, ct_ref[...], (((1,), (1,)), ((), ())), preferred_element_type=F32)
    for c, frows in fold_blocks:
        for t in range(S5_CHUNK):
            y_ref[token_rows(c, t), :] = yf[frows, t * LANES:(t + 1) * LANES]


def _s5_branch(u, h0_re, h0_im, bmat, ktoep, ctm, mr, mi, pr, pi_, layer, n_seq, seq_rows):
    n = u.shape[0]
    tile = n_seq * seq_rows
    n_total = h0_re.shape[0]
    n_chunks = tile // S5_CHUNK
    half = GROUPS_PER_LANE_BLOCK * SSM_STATE
    row_spec = pl.BlockSpec((tile, LANES), lambda lb, b: (b, lb))
    st_spec = pl.BlockSpec((n_seq, 1, half), lambda lb, b: (b, 0, lb))
    hp_rows = n_chunks + SUBLANES if n_seq == 1 else n_chunks
    mat_spec = pl.BlockSpec((None,) + bmat.shape[1:], lambda lb, b: (layer * N_LANE_BLOCKS + lb, 0, 0))
    kern = functools.partial(_s5_kernel, n_seq=n_seq, lane_chunk=256)
    return pl.pallas_call(
        kern,
        grid=(N_LANE_BLOCKS, n // tile),
        in_specs=[row_spec, st_spec, st_spec,
                  mat_spec, mat_spec, mat_spec,
                  pl.BlockSpec((None, 3, SUBLANES, half), lambda lb, b: (layer, 0, 0, lb)),
                  pl.BlockSpec((None, 3, SUBLANES, half), lambda lb, b: (layer, 0, 0, lb)),
                  pl.BlockSpec((None, SUBLANES, half), lambda lb, b: (layer, 0, lb)),
                  pl.BlockSpec((None, SUBLANES, half), lambda lb, b: (layer, 0, lb))],
        out_specs=[row_spec, st_spec, st_spec],
        out_shape=(jax.ShapeDtypeStruct((n, SSM_WIDTH), F32),
                   jax.ShapeDtypeStruct((n_total, 1, STATE_LANES), F32),
                   jax.ShapeDtypeStruct((n_total, 1, STATE_LANES), F32)),
        scratch_shapes=[pltpu.VMEM((n_chunks, S5_CHUNK * LANES), BF16),
                        pltpu.VMEM((n_chunks, half), F32), pltpu.VMEM((n_chunks, half), F32),
                        pltpu.VMEM((hp_rows, half), F32), pltpu.VMEM((hp_rows, half), F32)],
        compiler_params=_cparams(2),
        name="s5_branch",
    )(u, h0_re, h0_im, bmat, ktoep, ctm, mr, mi, pr, pi_)


def _sb_mask_queries(q_rows, qm_ref, tq):
    lane = lax.broadcasted_iota(jnp.int32, (tq, LANES), 1)
    for j in range(N_PAIRS):
        qpair = q_rows(j).astype(F32)
        for hh in range(HEADS_PER_LANE_BLOCK):
            h = j * HEADS_PER_LANE_BLOCK + hh
            in_head = (lane >= hh * HEAD_DIM) & (lane < (hh + 1) * HEAD_DIM)
            qm_ref[h * tq:(h + 1) * tq, :] = jnp.where(in_head, qpair, 0.0).astype(BF16)


class _SbTile(NamedTuple):
    tq: int
    k_tile: Any
    v_tile: Any
    keys_minor: bool
    valid: Any
    first: bool
    qm: Any
    lb: Any
    hl: Any
    a: Any
    car: Any
    acc: Any


def _sb_scores(t):
    pair = HEADS_PER_LANE_BLOCK * t.tq
    valid2 = None if t.valid is None else jnp.concatenate([t.valid] * HEADS_PER_LANE_BLOCK, axis=0)
    for j in range(N_PAIRS):
        rows = slice(j * pair, (j + 1) * pair)
        contract_k = 0 if t.keys_minor else 1
        z = lax.dot_general(t.qm[rows, :], t.k_tile(j), (((1,), (contract_k,)), ((), ())),
                            preferred_element_type=F32)
        sp = jnp.maximum(z, 0.0) + jnp.log2(1.0 + jnp.exp2(-jnp.abs(z)))
        t.lb[rows, :] = z - sp
        if valid2 is not None:
            sp = jnp.where(valid2, sp, 0.0)
        hi = sp.astype(BF16)
        t.hl[rows, :KEY_TILE] = hi
        t.hl[rows, KEY_TILE:] = (sp - hi.astype(F32)).astype(BF16)


def _sb_weights(t, tri_ref):
    cs = jnp.dot(t.hl[...], tri_ref[...], preferred_element_type=F32)
    cmax = None
    for h in range(N_HEADS):
        rows = slice(h * t.tq, (h + 1) * t.tq)
        after = cs[rows, :KEY_TILE]
        total = cs[rows, KEY_TILE:]
        if not t.first:
            carry = t.car[rows, :]
            after = after + carry
            total = total + carry
        a = jnp.exp2(t.lb[rows, :] + after)
        if t.valid is not None:
            a = jnp.where(t.valid, a, 0.0)
        t.a[rows, :] = a.astype(BF16)
        t.car[rows, :] = total
        cmax = total if cmax is None else jnp.maximum(cmax, total)
    return jnp.max(cmax)


def _sb_output(t):
    pair = HEADS_PER_LANE_BLOCK * t.tq
    lane = lax.broadcasted_iota(jnp.int32, (t.tq, LANES), 1)
    for j in range(N_PAIRS):
        contract_v = 1 if t.keys_minor else 0
        pv = lax.dot_general(t.a[j * pair:(j + 1) * pair, :], t.v_tile(j), (((1,), (contract_v,)), ((), ())),
                             preferred_element_type=F32)
        out = jnp.where(lane < HEAD_DIM, pv[:t.tq], pv[t.tq:])
        if t.first:
            t.acc[:, j * LANES:(j + 1) * LANES] = out
        else:
            t.acc[:, j * LANES:(j + 1) * LANES] += out


def _sb_tiles(tiles, tri_ref):
    for t in tiles:
        _sb_scores(t)
    ms = [_sb_weights(t, tri_ref) for t in tiles]
    for t in tiles:
        _sb_output(t)
    return ms


def _sb_scratch(n_blocks, tq):
    rows = N_HEADS * tq
    return [pltpu.VMEM((n_blocks, rows, LANES), BF16),
            pltpu.VMEM((n_blocks, rows, LANES), F32),
            pltpu.VMEM((n_blocks, rows, 2 * KEY_TILE), BF16),
            pltpu.VMEM((n_blocks, rows, LANES), BF16),
            pltpu.VMEM((n_blocks, rows, LANES), F32),
            pltpu.VMEM((n_blocks, tq, ATT_WIDTH), F32),
            pltpu.SMEM((n_blocks,), F32)]


def _sb_prompt_kernel(q_ref, k_ref, v_ref, za_ref, tri_ref, o_ref,
                      qm_ref, lb_ref, hl_ref, a_ref, car_ref, acc_ref, m_ref):
    tq = SB_Q_ROWS
    n_blocks = q_ref.shape[0] // tq
    step_row0 = pl.program_id(1) * (n_blocks * tq)
    row = lax.broadcasted_iota(jnp.int32, (tq, KEY_TILE), 0)
    col = lax.broadcasted_iota(jnp.int32, (tq, KEY_TILE), 1)

    def tiles(ref, start):
        return lambda j: ref[pl.ds(start, KEY_TILE), j * LANES:(j + 1) * LANES]

    def band_start(s):
        return pl.multiple_of(jnp.maximum(step_row0 + (s + 1) * tq - KEY_TILE, 0), tq)

    def tile(s, lo, valid, first):
        return _SbTile(tq, tiles(k_ref, lo), tiles(v_ref, lo), False, valid, first, qm_ref.at[s], lb_ref.at[s],
                       hl_ref.at[s], a_ref.at[s], car_ref.at[s], acc_ref.at[s])

    band = []
    for s in range(n_blocks):
        lo = band_start(s)
        _sb_mask_queries(lambda j: q_ref[s * tq:(s + 1) * tq, j * LANES:(j + 1) * LANES], qm_ref.at[s], tq)
        valid = (col - row) < (step_row0 + s * tq - lo)
        band.append(tile(s, lo, valid, True))
    for s, m in enumerate(_sb_tiles(band, tri_ref)):
        m_ref[s] = m

    def more_tiles(s, _):
        def cond(state):
            hi, m = state
            return (hi > 0) & (m > SB_LOG2_CUTOFF)

        def body(state):
            hi, _ = state
            lo = pl.multiple_of(jnp.maximum(hi - KEY_TILE, 0), tq)
            valid = col < (hi - lo)
            (m,) = _sb_tiles([tile(s, lo, valid, False)], tri_ref)
            return lo, m

        lax.while_loop(cond, body, (band_start(s), m_ref[s]))
        return 0

    lax.fori_loop(0, n_blocks, more_tiles, 0)
    for s in range(n_blocks):
        rows = slice(s * tq, (s + 1) * tq)
        o_ref[rows, :] = (acc_ref[s] * za_ref[rows, :].astype(F32)).astype(BF16)


def _sb_prompt(q, kb, vb, za, tri, n_batch, seq_len):
    step_rows = SB_STEP_BLOCKS * SB_Q_ROWS
    nq = seq_len // step_rows
    row_spec = pl.BlockSpec((step_rows, ATT_WIDTH), lambda b, i: (b * nq + i, 0))
    seq_spec = pl.BlockSpec((seq_len, ATT_WIDTH), lambda b, i: (b, 0))
    return pl.pallas_call(
        _sb_prompt_kernel,
        grid=(n_batch, nq),
        in_specs=[row_spec, seq_spec, seq_spec, row_spec, pl.BlockSpec(tri.shape, lambda b, i: (0, 0))],
        out_specs=row_spec,
        out_shape=jax.ShapeDtypeStruct(q.shape, BF16),
        scratch_shapes=_sb_scratch(SB_STEP_BLOCKS, SB_Q_ROWS),
        compiler_params=_cparams(2),
        name="sb_prompt",
    )(q, kb, vb, za, tri)


def _sb_sample_kernel(q_ref, kn_ref, vn_ref, kc_ref, vc_ref, za_ref, tri_ref, o_ref,
                      qm_ref, lb_ref, hl_ref, a_ref, car_ref, acc_ref, m_ref):
    tq = q_ref.shape[0]
    n_past = kc_ref.shape[-1] // KEY_TILE
    _sb_mask_queries(lambda j: q_ref[:, j * LANES:(j + 1) * LANES], qm_ref.at[0], tq)
    row = lax.broadcasted_iota(jnp.int32, (tq, KEY_TILE), 0)
    col = lax.broadcasted_iota(jnp.int32, (tq, KEY_TILE), 1)
    scratch = (qm_ref.at[0], lb_ref.at[0], hl_ref.at[0], a_ref.at[0], car_ref.at[0], acc_ref.at[0])

    new_tile = lambda ref: (lambda j: ref[:, j * LANES:(j + 1) * LANES])
    (m0,) = _sb_tiles([_SbTile(tq, new_tile(kn_ref), new_tile(vn_ref), False, col < row, True, *scratch)],
                      tri_ref)

    def past_tile(ref, t):
        start = pl.multiple_of(t * KEY_TILE, KEY_TILE)
        return lambda j: ref[0, 0, j * HEADS_PER_LANE_BLOCK:(j + 1) * HEADS_PER_LANE_BLOCK, :,
                             pl.ds(start, KEY_TILE)].reshape(LANES, KEY_TILE).astype(BF16)

    def cond(state):
        t, m = state
        return (t >= 0) & (m > SB_LOG2_CUTOFF)

    def body(state):
        t, _ = state
        (m,) = _sb_tiles([_SbTile(tq, past_tile(kc_ref, t), past_tile(vc_ref, t), True, None, False,
                                  *scratch)], tri_ref)
        return t - 1, m

    lax.while_loop(cond, body, (n_past - 1, m0))
    o_ref[...] = (acc_ref[0] * za_ref[...].astype(F32)).astype(BF16)


def _sb_sample(q, k_new, v_new, k_past, v_past, layer, za, tri, n_batch, t_new):
    past = k_past.shape[-1]
    row_spec = pl.BlockSpec((t_new, ATT_WIDTH), lambda b: (b, 0))
    new_spec = pl.BlockSpec((KEY_TILE, ATT_WIDTH), lambda b: (b, 0))
    past_spec = pl.BlockSpec((1, 1, N_HEADS, HEAD_DIM, past), lambda b: (layer, b, 0, 0, 0))
    return pl.pallas_call(
        _sb_sample_kernel,
        grid=(n_batch,),
        in_specs=[row_spec, new_spec, new_spec, past_spec, past_spec, row_spec,
                  pl.BlockSpec(tri.shape, lambda b: (0, 0))],
        out_specs=row_spec,
        out_shape=jax.ShapeDtypeStruct(q.shape, BF16),
        scratch_shapes=_sb_scratch(1, t_new),
        compiler_params=_cparams(1),
        name="sb_sample",
    )(q, k_new, v_new, k_past, v_past, za, tri)


def _outproj_body(x, ysc_ref, u_ref, zs_ref, ya_ref, gate_ref, d_ref, wglu_ref, bglu_ref, w_ref):
    y = ysc_ref[...] + d_ref[...] * u_ref[...]
    g = _gelu_tanh(y)
    glu = _sigmoid(jnp.dot(g.astype(BF16), wglu_ref[...], preferred_element_type=F32) + bglu_ref[...])
    ys = (g * glu * zs_ref[...].astype(F32)).astype(BF16)
    mix = jnp.dot(ys, w_ref[:SSM_WIDTH, :], preferred_element_type=F32)
    mix = mix + jnp.dot(ya_ref[...], w_ref[SSM_WIDTH:, :], preferred_element_type=F32)
    return x + gate_ref[0] * mix


class _OutProj(NamedTuple):
    ysc: Any
    u: Any
    zs: Any
    ya: Any
    gate: Any
    d_skip: Any
    w_glu_b: Any
    b_glu: Any
    w_out_b: Any


class _InProj(NamedTuple):
    shift: Any
    scale: Any
    norm_g: Any
    w_in_b: Any
    hsum: Any
    q_g: Any
    k_g: Any
    layer: int
    kv_all: Any


N_OUTPROJ_REFS = len(_OutProj._fields)
N_INPROJ_REFS = len(_InProj._fields) - 2


def _layer_kernel(x_ref, *refs, closes, opens, kv_aliased, kv_slot, position_minor):
    refs = list(refs)
    x = x_ref[...]
    if closes:
        out_in, refs = refs[:N_OUTPROJ_REFS], refs[N_OUTPROJ_REFS:]
    if opens:
        in_in, refs = refs[:N_INPROJ_REFS], refs[N_INPROJ_REFS + (2 if kv_aliased else 0):]
    if closes:
        x = _outproj_body(x, *out_in)
        refs.pop(0)[...] = x
    if opens:
        _inproj_body(x, *in_in, refs, kv_slot, position_minor)


def _layer_call(x2d, closing, opening, tm, rows_per_seq, depth):
    n = x2d.shape[0]
    if rows_per_seq is None:
        mod_spec = pl.BlockSpec((1, tm, D_MODEL), lambda i: (0, i, 0))
    else:
        tps = rows_per_seq // tm
        mod_spec = pl.BlockSpec((1, 1, D_MODEL), lambda i: (i // tps, 0, 0))
    row_spec = lambda w: pl.BlockSpec((tm, w), lambda i: (i, 0))
    full = lambda a: pl.BlockSpec(a.shape, lambda i: (0,) * a.ndim)
    position_minor = rows_per_seq is not None
    args, in_specs, out_specs, out_shape, aliases = [x2d], [row_spec(D_MODEL)], [], [], {}
    if closing is not None:
        args += list(closing)
        in_specs += [row_spec(SSM_WIDTH)] * 3 + [row_spec(ATT_WIDTH), mod_spec] + [full(a) for a in closing[5:]]
        out_specs.append(row_spec(D_MODEL))
        out_shape.append(jax.ShapeDtypeStruct(x2d.shape, F32))
    if opening is not None:
        kv_aliased = opening.kv_all is not None
        kv_layers, kv_first = (1, opening.layer) if kv_aliased else (depth, 0)
        bf = jax.ShapeDtypeStruct((n, SSM_WIDTH), BF16)
        if position_minor:
            f3 = jax.ShapeDtypeStruct((depth, n // rows_per_seq, N_HEADS, HEAD_DIM, rows_per_seq), F32)
            kv_spec = pl.BlockSpec((kv_layers, 1, N_HEADS, HEAD_DIM, tm),
                                   lambda i: (kv_first, i // tps, 0, 0, i % tps))
        else:
            f3 = jax.ShapeDtypeStruct((depth, n, N_HEADS, HEAD_DIM), F32)
            kv_spec = pl.BlockSpec((kv_layers, tm, N_HEADS, HEAD_DIM), lambda i: (kv_first, i, 0, 0))
        args += list(opening[:N_INPROJ_REFS])
        in_specs += [mod_spec, mod_spec] + [full(a) for a in opening[2:N_INPROJ_REFS]]
        if kv_aliased:
            assert opening.kv_all[0].shape == f3.shape and opening.kv_all[1].shape == f3.shape
            args += list(opening.kv_all)
            in_specs += [pl.BlockSpec(memory_space=pl.ANY)] * 2
            aliases = {len(args) - 2: len(out_shape) + 3, len(args) - 1: len(out_shape) + 4}
        out_specs += [row_spec(SSM_WIDTH)] * 3 + [kv_spec] * 2 + [row_spec(SSM_WIDTH)] * 3
        out_shape += [jax.ShapeDtypeStruct((n, SSM_WIDTH), F32), bf, bf, f3, f3, bf, bf, bf]
    outs = pl.pallas_call(
        functools.partial(_layer_kernel, closes=closing is not None, opens=opening is not None,
                          kv_aliased=opening is not None and opening.kv_all is not None,
                          kv_slot=None if opening is None else opening.layer,
                          position_minor=position_minor),
        grid=(n // tm,),
        in_specs=in_specs,
        out_specs=out_specs,
        out_shape=out_shape,
        input_output_aliases=aliases,
        compiler_params=_cparams(1),
        name="layer_" + "_".join(["close"] * (closing is not None) + ["open"] * (opening is not None)),
    )(*args)
    x_new = outs[0] if closing is not None else None
    opened = tuple(outs[len(outs) - 8:]) if opening is not None else None
    return x_new, opened


def _sb_sum_matrix():
    j = jnp.arange(2 * KEY_TILE)[:, None] % KEY_TILE
    s = jnp.arange(2 * KEY_TILE)[None, :]
    return -((s >= KEY_TILE) | (j > s)).astype(BF16)


def _head_sum_matrix():
    a = jnp.arange(MXU_DIM)
    return (a[:, None] // HEAD_DIM == a[None, :] // HEAD_DIM).astype(BF16)


def kernel(x_prompt, x_sample, c_prompt, c_sample, cache_k, cache_v, state_ssm_re, state_ssm_im, norm_g, w_mod, b_mod, w_in, ssm_a_re, ssm_a_im, ssm_log_dt, ssm_b_re, ssm_b_im, ssm_c_re, ssm_c_im, ssm_d, w_glu, b_glu, q_norm_g, k_norm_g, w_out):
    depth = w_in.shape[0]
    nb, seq_len, _ = x_prompt.shape
    ns, t_new, _ = x_sample.shape
    tm = 512

    mod = _modulation(jnp.concatenate([c_prompt, c_sample], axis=0), w_mod, b_mod)
    s5_ops = _s5_operators(ssm_a_re, ssm_a_im, ssm_log_dt, ssm_b_re, ssm_b_im, ssm_c_re, ssm_c_im)

    w_in_b = w_in.astype(BF16)
    w_out_b = w_out.astype(BF16)
    w_glu_b = w_glu.astype(BF16)
    tri = _sb_sum_matrix()
    hsum = _head_sum_matrix()

    xp = x_prompt.reshape(nb * seq_len, D_MODEL)
    xs = x_sample.reshape(ns * t_new, D_MODEL)
    zero_state = jnp.zeros((nb, 1, STATE_LANES), F32)
    kc = jnp.transpose(cache_k, (0, 1, 3, 4, 2))
    vc = jnp.transpose(cache_v, (0, 1, 3, 4, 2))
    outs = {name: [] for name in ("pr", "pi", "sr", "si")}
    pkv = skv = None

    def in_proj_args(l, shift, scale, kv_all):
        return _InProj(shift, scale, norm_g[l].reshape(1, D_MODEL), w_in_b[l], hsum,
                       jnp.tile(q_norm_g[l], N_HEADS).reshape(1, ATT_WIDTH),
                       jnp.tile(k_norm_g[l], N_HEADS).reshape(1, ATT_WIDTH), l, kv_all)

    def prompt_mod(l):
        mp = mod[l, :nb].reshape(nb, 1, 3 * D_MODEL)
        return tuple(mp[:, :, i * D_MODEL:(i + 1) * D_MODEL] for i in range(3))

    shift, scale, gate = prompt_mod(0)
    _, opened = _layer_call(xp, None, in_proj_args(0, shift, scale, pkv), tm, seq_len, depth)
    for l in range(depth):
        glu_w = (ssm_d[l].reshape(1, SSM_WIDTH), w_glu_b[l], b_glu[l].reshape(1, SSM_WIDTH))

        u, zs, q, k_all, v_all, kb, vb, za = opened
        pkv = (k_all, v_all)
        ysc, hfr, hfi = _s5_branch(u, zero_state, zero_state, *s5_ops, l, 1, seq_len)
        ya = _sb_prompt(q, kb, vb, za, tri, nb, seq_len)
        closing = _OutProj(ysc, u, zs, ya, gate, *glu_w, w_out_b[l])
        opening = None
        if l + 1 < depth:
            shift, scale, gate = prompt_mod(l + 1)
            opening = in_proj_args(l + 1, shift, scale, pkv)
        xp, opened = _layer_call(xp, closing, opening, tm, seq_len, depth)
        outs["pr"].append(hfr.reshape(nb, SSM_GROUPS, SSM_STATE))
        outs["pi"].append(hfi.reshape(nb, SSM_GROUPS, SSM_STATE))

        ms = jnp.repeat(mod[l, nb:], t_new, axis=0).reshape(1, ns * t_new, 3 * D_MODEL)
        s_shift, s_scale, s_gate = (ms[:, :, i * D_MODEL:(i + 1) * D_MODEL] for i in range(3))
        ts = ns * t_new
        _, (u, zs, q, k_all, v_all, kb, vb, za) = _layer_call(
            xs, None, in_proj_args(l, s_shift, s_scale, skv), ts, None, depth)
        skv = (k_all, v_all)
        h0r = state_ssm_re[l].reshape(ns, 1, STATE_LANES)
        h0i = state_ssm_im[l].reshape(ns, 1, STATE_LANES)
        ysc, hfr, hfi = _s5_branch(u, h0r, h0i, *s5_ops, l, ns, t_new)
        pad = lambda a: jnp.pad(a.reshape(ns, t_new, ATT_WIDTH),
                                ((0, 0), (0, KEY_TILE - t_new), (0, 0))).reshape(ns * KEY_TILE, ATT_WIDTH)
        ya = _sb_sample(q, pad(kb), pad(vb), kc, vc, l, za, tri, ns, t_new)
        xs, _ = _layer_call(xs, _OutProj(ysc, u, zs, ya, s_gate, *glu_w, w_out_b[l]), None, ts, None, depth)
        outs["sr"].append(hfr.reshape(ns, SSM_GROUPS, SSM_STATE))
        outs["si"].append(hfi.reshape(ns, SSM_GROUPS, SSM_STATE))

    st = lambda name: jnp.stack(outs[name])
    heads_p = lambda a: jnp.transpose(a, (0, 1, 4, 2, 3))
    heads_s = lambda a: a.reshape(depth, ns, t_new, N_HEADS, HEAD_DIM)
    return (xp.reshape(nb, seq_len, D_MODEL), xs.reshape(ns, t_new, D_MODEL),
            heads_p(pkv[0]), heads_p(pkv[1]), st("pr"), st("pi"),
            heads_s(skv[0]), heads_s(skv[1]), st("sr"), st("si"))
```

```python
import functools
import math
from typing import Any, NamedTuple

import jax
import jax.numpy as jnp
from jax import lax
from jax.experimental import pallas as pl
from jax.experimental.pallas import tpu as pltpu

F32 = jnp.float32
BF16 = jnp.bfloat16

D_MODEL = 1024
SSM_WIDTH = 512
SSM_GROUP = 16
SSM_GROUPS = 32
SSM_STATE = 64
STATE_LANES = SSM_GROUPS * SSM_STATE
ATT_WIDTH = 512
HEAD_DIM = 64
N_HEADS = 8
IN_WIDTH = 2 * SSM_WIDTH + 4 * ATT_WIDTH
EPS = 1e-6

LANES = 128
SUBLANES = 8
MXU_DIM = 256
HEADS_PER_LANE_BLOCK = LANES // HEAD_DIM
GROUPS_PER_LANE_BLOCK = LANES // SSM_GROUP
N_LANE_BLOCKS = SSM_WIDTH // LANES
N_PAIRS = N_HEADS // HEADS_PER_LANE_BLOCK
S5_CHUNK = SUBLANES
KEY_TILE = 128
SB_Q_ROWS = 32
SB_STEP_BLOCKS = 16
LOG2_E = math.log2(math.e)
SB_LOG2_CUTOFF = -40.0 * LOG2_E
VMEM_LIMIT = 56 * 1024 * 1024


def _cparams(n_axes):
    return pltpu.CompilerParams(dimension_semantics=("arbitrary",) * n_axes,
                                vmem_limit_bytes=VMEM_LIMIT)


def _silu(x):
    return x * (1.0 / (1.0 + jnp.exp(-x)))


def _sigmoid(x):
    return 1.0 / (1.0 + jnp.exp(-x))


def _gelu_tanh(x):
    c = math.sqrt(2.0 / math.pi)
    return 0.5 * x * (1.0 + jnp.tanh(c * (x + 0.044715 * (x * x * x))))


def _mod_kernel(c_ref, w_ref, b_ref, o_ref):
    c = c_ref[...]
    a = _silu(c)
    o_ref[0] = jnp.dot(a, w_ref[0], preferred_element_type=F32,
                       precision=lax.Precision.HIGHEST) + b_ref[0]


def _modulation(c_all, w_mod, b_mod):
    depth = w_mod.shape[0]
    n = c_all.shape[0]
    nt = 3
    return pl.pallas_call(
        _mod_kernel,
        grid=(depth, nt),
        in_specs=[pl.BlockSpec((n, D_MODEL), lambda l, j: (0, 0)),
                  pl.BlockSpec((1, D_MODEL, D_MODEL), lambda l, j: (l, 0, j)),
                  pl.BlockSpec((1, 1, D_MODEL), lambda l, j: (l, 0, j))],
        out_specs=pl.BlockSpec((1, n, D_MODEL), lambda l, j: (l, 0, j)),
        out_shape=jax.ShapeDtypeStruct((depth, n, 3 * D_MODEL), F32),
        compiler_params=_cparams(2),
        name="modulation",
    )(c_all, w_mod, b_mod.reshape(depth, 1, 3 * D_MODEL))


def _disc_kernel(lr_ref, li_ref, ldt_ref, bre_ref, bim_ref, cre_ref, cim_ref,
                 pw8r_ref, pw8i_ref, lbr_ref, lbi_ref, clr_ref, cli_ref):
    lr = lr_ref[...]
    li = li_ref[...]
    dt = jnp.exp(ldt_ref[...])
    mag = jnp.exp(lr * dt)
    ang = li * dt
    ab_re = mag * jnp.cos(ang)
    ab_im = mag * jnp.sin(ang)
    den = lr * lr + li * li
    nr = ab_re - 1.0
    f_re = (nr * lr + ab_im * li) / den
    f_im = (ab_im * lr - nr * li) / den
    bre = bre_ref[...]
    bim = bim_ref[...]
    cre = cre_ref[...]
    cim = cim_ref[...]
    xr = f_re[:, None, :] * bre - f_im[:, None, :] * bim
    xi = f_re[:, None, :] * bim + f_im[:, None, :] * bre
    pr, pi_ = ab_re, ab_im
    for t in range(S5_CHUNK):
        lbr_ref[t] = xr
        lbi_ref[t] = xi
        clr_ref[t] = cre * pr[:, None, :] - cim * pi_[:, None, :]
        cli_ref[t] = cre * pi_[:, None, :] + cim * pr[:, None, :]
        if t + 1 < S5_CHUNK:
            xr, xi = (xr * ab_re[:, None, :] - xi * ab_im[:, None, :],
                      xr * ab_im[:, None, :] + xi * ab_re[:, None, :])
            pr, pi_ = pr * ab_re - pi_ * ab_im, pr * ab_im + pi_ * ab_re
    a8r, a8i = pr, pi_
    qr, qi = a8r, a8i
    for m in range(SUBLANES):
        pw8r_ref[m] = qr
        pw8i_ref[m] = qi
        qr, qi = qr * a8r - qi * a8i, qr * a8i + qi * a8r


def _place_kernel(lbr_ref, lbi_ref, clr_ref, cli_ref, cre_ref, cim_ref, bm_ref, ct_ref, kt_ref, kcat):
    gb = GROUPS_PER_LANE_BLOCK
    half = gb * SSM_STATE
    rep_p = (lax.broadcasted_iota(jnp.int32, (SSM_STATE, half), 0)
             == lax.broadcasted_iota(jnp.int32, (SSM_STATE, half), 1) % SSM_STATE).astype(BF16)
    rep_c = (lax.broadcasted_iota(jnp.int32, (SSM_GROUP, LANES), 0)
             == lax.broadcasted_iota(jnp.int32, (SSM_GROUP, LANES), 1) % SSM_GROUP).astype(BF16)
    same_p = (lax.broadcasted_iota(jnp.int32, (LANES, half), 0) // SSM_GROUP
              == lax.broadcasted_iota(jnp.int32, (LANES, half), 1) // SSM_STATE)
    same_c = (lax.broadcasted_iota(jnp.int32, (LANES, LANES), 0) // SSM_GROUP
              == lax.broadcasted_iota(jnp.int32, (LANES, LANES), 1) // SSM_GROUP)

    def place_states(x):
        tiled = jnp.dot(x.reshape(LANES, SSM_STATE).astype(BF16), rep_p, preferred_element_type=F32)
        return jnp.where(same_p, tiled, 0.0).astype(BF16)

    for s in range(S5_CHUNK):
        rows = slice(s * LANES, (s + 1) * LANES)
        bm_ref[0, rows, :half] = place_states(lbr_ref[S5_CHUNK - 1 - s])
        bm_ref[0, rows, half:] = place_states(lbi_ref[S5_CHUNK - 1 - s])
        ct_ref[0, rows, :half] = place_states(clr_ref[s])
        ct_ref[0, rows, half:] = place_states(-cli_ref[s])

    for g in range(gb):
        b_cat = jnp.concatenate([lbr_ref[:, g].reshape(LANES, SSM_STATE),
                                 lbi_ref[:, g].reshape(LANES, SSM_STATE)], axis=1)
        c_cat = jnp.concatenate([cre_ref[g], -cim_ref[g]], axis=1)
        kg = lax.dot_general(b_cat, c_cat, (((1,), (1,)), ((), ())), preferred_element_type=F32,
                             precision=lax.Precision.HIGHEST)
        for j in range(S5_CHUNK):
            kcat[j, g * SSM_GROUP:(g + 1) * SSM_GROUP, :] = kg[j * SSM_GROUP:(j + 1) * SSM_GROUP, :]
    zero = jnp.zeros((LANES, LANES), BF16)
    for j in range(S5_CHUNK):
        tiled = jnp.dot(kcat[j].astype(BF16), rep_c, preferred_element_type=F32)
        blk = jnp.where(same_c, tiled, 0.0).astype(BF16)
        for s in range(S5_CHUNK - j):
            kt_ref[0, s * LANES:(s + 1) * LANES, (s + j) * LANES:(s + j + 1) * LANES] = blk
        if j > 0:
            for t in range(S5_CHUNK - j):
                kt_ref[0, (t + j) * LANES:(t + j + 1) * LANES, t * LANES:(t + 1) * LANES] = zero


def _s5_operators(a_re, a_im, log_dt, b_re, b_im, c_re, c_im):
    depth = a_re.shape[0]
    rows = depth * SSM_GROUPS
    lr = a_re.reshape(rows, SSM_STATE)
    li = a_im.reshape(rows, SSM_STATE)
    ldt = jnp.broadcast_to(log_dt.reshape(rows, 1), (rows, SSM_STATE))
    bre = b_re.transpose(0, 1, 3, 2).reshape(rows, SSM_GROUP, SSM_STATE)
    bim = b_im.transpose(0, 1, 3, 2).reshape(rows, SSM_GROUP, SSM_STATE)
    cre = c_re.reshape(rows, SSM_GROUP, SSM_STATE)
    cim = c_im.reshape(rows, SSM_GROUP, SSM_STATE)
    pw_shape = jax.ShapeDtypeStruct((SUBLANES, rows, SSM_STATE), F32)
    op_shape = jax.ShapeDtypeStruct((S5_CHUNK, rows, SSM_GROUP, SSM_STATE), F32)
    pw8r, pw8i, lbr, lbi, clr, cli = pl.pallas_call(
        _disc_kernel,
        out_shape=(pw_shape, pw_shape, op_shape, op_shape, op_shape, op_shape),
        compiler_params=_cparams(0),
        name="s5_discretise",
    )(lr, li, ldt, bre, bim, cre, cim)
    gb = GROUPS_PER_LANE_BLOCK
    n_blk = rows // gb
    width = S5_CHUNK * LANES
    op_spec = pl.BlockSpec((S5_CHUNK, gb, SSM_GROUP, SSM_STATE), lambda i: (0, i, 0, 0))
    c_spec = pl.BlockSpec((gb, SSM_GROUP, SSM_STATE), lambda i: (i, 0, 0))
    mat_spec = pl.BlockSpec((1, width, width), lambda i: (i, 0, 0))
    mat_shape = jax.ShapeDtypeStruct((n_blk, width, width), BF16)
    bmat, ctm, ktoep = pl.pallas_call(
        _place_kernel,
        grid=(n_blk,),
        in_specs=[op_spec] * 4 + [c_spec] * 2,
        out_specs=[mat_spec] * 3,
        out_shape=(mat_shape,) * 3,
        scratch_shapes=[pltpu.VMEM((S5_CHUNK, LANES, SSM_GROUP), F32)],
        compiler_params=_cparams(1),
        name="s5_place_operators",
    )(lbr, lbi, clr, cli, cre, cim)

    pw8r = pw8r.reshape(SUBLANES, depth, STATE_LANES).transpose(1, 0, 2)
    pw8i = pw8i.reshape(SUBLANES, depth, STATE_LANES).transpose(1, 0, 2)
    row = jnp.arange(SUBLANES)[None, :, None]

    def step_mult(pw):
        return jnp.stack([jnp.where(row >= k, pw[:, k - 1][:, None, :], 0.0) for k in (1, 2, 4)], axis=1)

    return bmat, ktoep, ctm, step_mult(pw8r), step_mult(pw8i), pw8r, pw8i


def _heads_major(x, position_minor):
    if position_minor:
        return x.T.reshape(N_HEADS, HEAD_DIM, x.shape[0])
    heads = jnp.stack([x[:, h * HEAD_DIM:(h + 1) * HEAD_DIM] for h in range(N_HEADS)], axis=0)
    return pltpu.einshape("htd->thd", heads)


def _inproj_body(x, shift_ref, scale_ref, g_ref, w_ref, hsum_ref, qg_ref, kg_ref, outs, kv_slot,
                 position_minor):
    u_ref, zs_ref, q_ref, k_ref, v_ref, kb_ref, vb_ref, za_ref = outs

    def store_kv(ref, value):
        slot = kv_slot if ref.shape[0] > 1 else 0
        for s in range(ref.shape[0]):
            ref[s] = (_heads_major(value, position_minor).reshape(ref.shape[1:]) if s == slot
                      else jnp.zeros(ref.shape[1:], F32))

    ms = jnp.mean(x * x, axis=-1, keepdims=True)
    h = x * lax.rsqrt(ms + EPS) * g_ref[...]
    h = h * (1.0 + scale_ref[0]) + shift_ref[0]
    hb = h.astype(BF16)

    def proj(c):
        return jnp.dot(hb, w_ref[:, c * SSM_WIDTH:(c + 1) * SSM_WIDTH], preferred_element_type=F32)

    def head_norm(p, g):
        sq = (p * p).astype(BF16)
        width = hsum_ref.shape[0]
        ss = jnp.concatenate([jnp.dot(sq[:, c:c + width], hsum_ref[...], preferred_element_type=F32)
                              for c in range(0, ATT_WIDTH, width)], axis=1)
        return p * lax.rsqrt(ss * (1.0 / HEAD_DIM) + EPS) * g

    u_ref[...] = proj(0)
    zs_ref[...] = _silu(proj(1)).astype(BF16)
    q = head_norm(proj(2), qg_ref[...])
    q_ref[...] = (q * (HEAD_DIM ** -0.5 * LOG2_E)).astype(BF16)
    k = head_norm(proj(3), kg_ref[...])
    store_kv(k_ref, k)
    kb_ref[...] = k.astype(BF16)
    v = proj(4)
    store_kv(v_ref, v)
    vb_ref[...] = v.astype(BF16)
    za_ref[...] = _silu(proj(5)).astype(BF16)


def _s5_kernel(u_ref, h0r_ref, h0i_ref, bm_ref, kt_ref, ct_ref, mr_ref, mi_ref, pr_ref, pi_ref,
               y_ref, hfr_ref, hfi_ref,
               uflat, wre, wim, hpr, hpi, *, n_seq, lane_chunk):
    rows = u_ref.shape[0]
    n_chunks = rows // S5_CHUNK
    seq_chunks = n_chunks // n_seq
    half = GROUPS_PER_LANE_BLOCK * SSM_STATE

    def token_rows(c, t):
        if n_seq == 1:
            return pl.ds(t, n_chunks, stride=S5_CHUNK)
        return pl.ds(c * S5_CHUNK + t, n_seq, stride=seq_chunks * S5_CHUNK)

    fold_blocks = [(0, slice(None))] if n_seq == 1 else [(c, slice(c * n_seq, (c + 1) * n_seq))
                                                        for c in range(seq_chunks)]
    for c, frows in fold_blocks:
        for t in range(S5_CHUNK):
            uflat[frows, t * LANES:(t + 1) * LANES] = u_ref[token_rows(c, t), :].astype(BF16)
    w = jnp.dot(uflat[...], bm_ref[...], preferred_element_type=F32)
    wre[...] = w[:, :half]
    wim[...] = w[:, half:]
    yf = jnp.concatenate([jnp.dot(uflat[:, :c + MXU_DIM], kt_ref[:c + MXU_DIM, c:c + MXU_DIM],
                                  preferred_element_type=F32)
                          for c in range(0, S5_CHUNK * LANES, MXU_DIM)], axis=1)

    if n_seq == 1:
        hpr[0:SUBLANES, :] = jnp.broadcast_to(h0r_ref[0], (SUBLANES, half))
        hpi[0:SUBLANES, :] = jnp.broadcast_to(h0i_ref[0], (SUBLANES, half))
        for ch in range(half // lane_chunk):
            sl = slice(ch * lane_chunk, (ch + 1) * lane_chunk)
            prc = pr_ref[:, sl]
            pic = pi_ref[:, sl]

            def group_body(r, carry):
                cr, ci = carry
                row = pl.multiple_of(r * SUBLANES, SUBLANES)
                vr = wre[pl.ds(row, SUBLANES), sl]
                vi = wim[pl.ds(row, SUBLANES), sl]
                for idx, k in enumerate((1, 2, 4)):
                    mr = mr_ref[idx, :, sl]
                    mi = mi_ref[idx, :, sl]
                    sr = pltpu.roll(vr, k, 0)
                    si = pltpu.roll(vi, k, 0)
                    vr, vi = vr + (mr * sr - mi * si), vi + (mr * si + mi * sr)
                hr = vr + (prc * cr - pic * ci)
                hi = vi + (prc * ci + pic * cr)
                hpr[pl.ds(row + SUBLANES, SUBLANES), sl] = hr
                hpi[pl.ds(row + SUBLANES, SUBLANES), sl] = hi
                last = SUBLANES - 1
                return (jnp.broadcast_to(hr[last:last + 1], hr.shape),
                        jnp.broadcast_to(hi[last:last + 1], hi.shape))

            cr, ci = lax.fori_loop(0, n_chunks // SUBLANES, group_body,
                                   (hpr[0:SUBLANES, sl], hpi[0:SUBLANES, sl]), unroll=True)
            hfr_ref[0, :, sl] = cr[0:1]
            hfi_ref[0, :, sl] = ci[0:1]
        h_prev_r = hpr[SUBLANES - 1:SUBLANES - 1 + n_chunks, :]
        h_prev_i = hpi[SUBLANES - 1:SUBLANES - 1 + n_chunks, :]
    else:
        a8r = pr_ref[0:1, :]
        a8i = pi_ref[0:1, :]
        hr = h0r_ref[...].reshape(n_seq, half)
        hi = h0i_ref[...].reshape(n_seq, half)
        for c, frows in fold_blocks:
            hpr[frows, :] = hr
            hpi[frows, :] = hi
            wr = wre[frows, :]
            wi = wim[frows, :]
            hr, hi = a8r * hr - a8i * hi + wr, a8r * hi + a8i * hr + wi
        hfr_ref[...] = hr.reshape(hfr_ref.shape)
        hfi_ref[...] = hi.reshape(hfi_ref.shape)
        h_prev_r = hpr[0:n_chunks, :]
        h_prev_i = hpi[0:n_chunks, :]

    h_prev = jnp.concatenate([h_prev_r.astype(BF16), h_prev_i.astype(BF16)], axis=1)
    yf = yf + lax.dot_general(h_prev, ct_ref[...], (((1,), (1,)), ((), ())), preferred_element_type=F32)
    for c, frows in fold_blocks:
        for t in range(S5_CHUNK):
            y_ref[token_rows(c, t), :] = yf[frows, t * LANES:(t + 1) * LANES]


def _s5_branch(u, h0_re, h0_im, bmat, ktoep, ctm, mr, mi, pr, pi_, layer, n_seq, seq_rows):
    n = u.shape[0]
    tile = n_seq * seq_rows
    n_total = h0_re.shape[0]
    n_chunks = tile // S5_CHUNK
    half = GROUPS_PER_LANE_BLOCK * SSM_STATE
    row_spec = pl.BlockSpec((tile, LANES), lambda lb, b: (b, lb))
    st_spec = pl.BlockSpec((n_seq, 1, half), lambda lb, b: (b, 0, lb))
    hp_rows = n_chunks + SUBLANES if n_seq == 1 else n_chunks
    mat_spec = pl.BlockSpec((None,) + bmat.shape[1:], lambda lb, b: (layer * N_LANE_BLOCKS + lb, 0, 0))
    kern = functools.partial(_s5_kernel, n_seq=n_seq, lane_chunk=256)
    return pl.pallas_call(
        kern,
        grid=(N_LANE_BLOCKS, n // tile),
        in_specs=[row_spec, st_spec, st_spec,
                  mat_spec, mat_spec, mat_spec,
                  pl.BlockSpec((None, 3, SUBLANES, half), lambda lb, b: (layer, 0, 0, lb)),
                  pl.BlockSpec((None, 3, SUBLANES, half), lambda lb, b: (layer, 0, 0, lb)),
                  pl.BlockSpec((None, SUBLANES, half), lambda lb, b: (layer, 0, lb)),
                  pl.BlockSpec((None, SUBLANES, half), lambda lb, b: (layer, 0, lb))],
        out_specs=[row_spec, st_spec, st_spec],
        out_shape=(jax.ShapeDtypeStruct((n, SSM_WIDTH), F32),
                   jax.ShapeDtypeStruct((n_total, 1, STATE_LANES), F32),
                   jax.ShapeDtypeStruct((n_total, 1, STATE_LANES), F32)),
        scratch_shapes=[pltpu.VMEM((n_chunks, S5_CHUNK * LANES), BF16),
                        pltpu.VMEM((n_chunks, half), F32), pltpu.VMEM((n_chunks, half), F32),
                        pltpu.VMEM((hp_rows, half), F32), pltpu.VMEM((hp_rows, half), F32)],
        compiler_params=_cparams(2),
        name="s5_branch",
    )(u, h0_re, h0_im, bmat, ktoep, ctm, mr, mi, pr, pi_)


def _sb_mask_queries(q_rows, qm_ref, tq):
    lane = lax.broadcasted_iota(jnp.int32, (tq, LANES), 1)
    for j in range(N_PAIRS):
        qpair = q_rows(j).astype(F32)
        for hh in range(HEADS_PER_LANE_BLOCK):
            h = j * HEADS_PER_LANE_BLOCK + hh
            in_head = (lane >= hh * HEAD_DIM) & (lane < (hh + 1) * HEAD_DIM)
            qm_ref[h * tq:(h + 1) * tq, :] = jnp.where(in_head, qpair, 0.0).astype(BF16)


class _SbTile(NamedTuple):
    tq: int
    k_tile: Any
    v_tile: Any
    keys_minor: bool
    valid: Any
    first: bool
    qm: Any
    lb: Any
    hl: Any
    a: Any
    car: Any
    acc: Any


def _sb_scores(t):
    pair = HEADS_PER_LANE_BLOCK * t.tq
    valid2 = None if t.valid is None else jnp.concatenate([t.valid] * HEADS_PER_LANE_BLOCK, axis=0)
    for j in range(N_PAIRS):
        rows = slice(j * pair, (j + 1) * pair)
        contract_k = 0 if t.keys_minor else 1
        z = lax.dot_general(t.qm[rows, :], t.k_tile(j), (((1,), (contract_k,)), ((), ())),
                            preferred_element_type=F32)
        sp = jnp.maximum(z, 0.0) + jnp.log2(1.0 + jnp.exp2(-jnp.abs(z)))
        t.lb[rows, :] = z - sp
        if valid2 is not None:
            sp = jnp.where(valid2, sp, 0.0)
        hi = sp.astype(BF16)
        t.hl[rows, :KEY_TILE] = hi
        t.hl[rows, KEY_TILE:] = (sp - hi.astype(F32)).astype(BF16)


def _sb_weights(t, tri_ref):
    cs = jnp.dot(t.hl[...], tri_ref[...], preferred_element_type=F32)
    cmax = None
    for h in range(N_HEADS):
        rows = slice(h * t.tq, (h + 1) * t.tq)
        after = cs[rows, :KEY_TILE]
        total = cs[rows, KEY_TILE:]
        if not t.first:
            carry = t.car[rows, :]
            after = after + carry
            total = total + carry
        a = jnp.exp2(t.lb[rows, :] + after)
        if t.valid is not None:
            a = jnp.where(t.valid, a, 0.0)
        t.a[rows, :] = a.astype(BF16)
        t.car[rows, :] = total
        cmax = total if cmax is None else jnp.maximum(cmax, total)
    return jnp.max(cmax)


def _sb_output(t):
    pair = HEADS_PER_LANE_BLOCK * t.tq
    lane = lax.broadcasted_iota(jnp.int32, (t.tq, LANES), 1)
    for j in range(N_PAIRS):
        contract_v = 1 if t.keys_minor else 0
        pv = lax.dot_general(t.a[j * pair:(j + 1) * pair, :], t.v_tile(j), (((1,), (contract_v,)), ((), ())),
                             preferred_element_type=F32)
        out = jnp.where(lane < HEAD_DIM, pv[:t.tq], pv[t.tq:])
        if t.first:
            t.acc[:, j * LANES:(j + 1) * LANES] = out
        else:
            t.acc[:, j * LANES:(j + 1) * LANES] += out


def _sb_tiles(tiles, tri_ref):
    for t in tiles:
        _sb_scores(t)
    ms = [_sb_weights(t, tri_ref) for t in tiles]
    for t in tiles:
        _sb_output(t)
    return ms


def _sb_scratch(n_blocks, tq):
    rows = N_HEADS * tq
    return [pltpu.VMEM((n_blocks, rows, LANES), BF16),
            pltpu.VMEM((n_blocks, rows, LANES), F32),
            pltpu.VMEM((n_blocks, rows, 2 * KEY_TILE), BF16),
            pltpu.VMEM((n_blocks, rows, LANES), BF16),
            pltpu.VMEM((n_blocks, rows, LANES), F32),
            pltpu.VMEM((n_blocks, tq, ATT_WIDTH), F32),
            pltpu.SMEM((n_blocks,), F32)]


def _sb_prompt_kernel(q_ref, k_ref, v_ref, za_ref, tri_ref, o_ref,
                      qm_ref, lb_ref, hl_ref, a_ref, car_ref, acc_ref, m_ref):
    tq = SB_Q_ROWS
    n_blocks = q_ref.shape[0] // tq
    step_row0 = pl.program_id(1) * (n_blocks * tq)
    row = lax.broadcasted_iota(jnp.int32, (tq, KEY_TILE), 0)
    col = lax.broadcasted_iota(jnp.int32, (tq, KEY_TILE), 1)

    def tiles(ref, start):
        return lambda j: ref[pl.ds(start, KEY_TILE), j * LANES:(j + 1) * LANES]

    def band_start(s):
        return pl.multiple_of(jnp.maximum(step_row0 + (s + 1) * tq - KEY_TILE, 0), tq)

    def tile(s, lo, valid, first):
        return _SbTile(tq, tiles(k_ref, lo), tiles(v_ref, lo), False, valid, first, qm_ref.at[s], lb_ref.at[s],
                       hl_ref.at[s], a_ref.at[s], car_ref.at[s], acc_ref.at[s])

    band = []
    for s in range(n_blocks):
        lo = band_start(s)
        _sb_mask_queries(lambda j: q_ref[s * tq:(s + 1) * tq, j * LANES:(j + 1) * LANES], qm_ref.at[s], tq)
        valid = (col - row) < (step_row0 + s * tq - lo)
        band.append(tile(s, lo, valid, True))
    for s, m in enumerate(_sb_tiles(band, tri_ref)):
        m_ref[s] = m

    def more_tiles(s, _):
        def cond(state):
            hi, m = state
            return (hi > 0) & (m > SB_LOG2_CUTOFF)

        def body(state):
            hi, _ = state
            lo = pl.multiple_of(jnp.maximum(hi - KEY_TILE, 0), tq)
            valid = col < (hi - lo)
            (m,) = _sb_tiles([tile(s, lo, valid, False)], tri_ref)
            return lo, m

        lax.while_loop(cond, body, (band_start(s), m_ref[s]))
        return 0

    lax.fori_loop(0, n_blocks, more_tiles, 0)
    for s in range(n_blocks):
        rows = slice(s * tq, (s + 1) * tq)
        o_ref[rows, :] = (acc_ref[s] * za_ref[rows, :].astype(F32)).astype(BF16)


def _sb_prompt(q, kb, vb, za, tri, n_batch, seq_len):
    step_rows = SB_STEP_BLOCKS * SB_Q_ROWS
    nq = seq_len // step_rows
    row_spec = pl.BlockSpec((step_rows, ATT_WIDTH), lambda b, i: (b * nq + i, 0))
    seq_spec = pl.BlockSpec((seq_len, ATT_WIDTH), lambda b, i: (b, 0))
    return pl.pallas_call(
        _sb_prompt_kernel,
        grid=(n_batch, nq),
        in_specs=[row_spec, seq_spec, seq_spec, row_spec, pl.BlockSpec(tri.shape, lambda b, i: (0, 0))],
        out_specs=row_spec,
        out_shape=jax.ShapeDtypeStruct(q.shape, BF16),
        scratch_shapes=_sb_scratch(SB_STEP_BLOCKS, SB_Q_ROWS),
        compiler_params=_cparams(2),
        name="sb_prompt",
    )(q, kb, vb, za, tri)


def _sb_sample_kernel(q_ref, kn_ref, vn_ref, kc_ref, vc_ref, za_ref, tri_ref, o_ref,
                      qm_ref, lb_ref, hl_ref, a_ref, car_ref, acc_ref, m_ref):
    tq = q_ref.shape[0]
    n_past = kc_ref.shape[-1] // KEY_TILE
    _sb_mask_queries(lambda j: q_ref[:, j * LANES:(j + 1) * LANES], qm_ref.at[0], tq)
    row = lax.broadcasted_iota(jnp.int32, (tq, KEY_TILE), 0)
    col = lax.broadcasted_iota(jnp.int32, (tq, KEY_TILE), 1)
    scratch = (qm_ref.at[0], lb_ref.at[0], hl_ref.at[0], a_ref.at[0], car_ref.at[0], acc_ref.at[0])

    new_tile = lambda ref: (lambda j: ref[:, j * LANES:(j + 1) * LANES])
    (m0,) = _sb_tiles([_SbTile(tq, new_tile(kn_ref), new_tile(vn_ref), False, col < row, True, *scratch)],
                      tri_ref)

    def past_tile(ref, t):
        start = pl.multiple_of(t * KEY_TILE, KEY_TILE)
        return lambda j: ref[0, 0, j * HEADS_PER_LANE_BLOCK:(j + 1) * HEADS_PER_LANE_BLOCK, :,
                             pl.ds(start, KEY_TILE)].reshape(LANES, KEY_TILE).astype(BF16)

    def cond(state):
        t, m = state
        return (t >= 0) & (m > SB_LOG2_CUTOFF)

    def body(state):
        t, _ = state
        (m,) = _sb_tiles([_SbTile(tq, past_tile(kc_ref, t), past_tile(vc_ref, t), True, None, False,
                                  *scratch)], tri_ref)
        return t - 1, m

    lax.while_loop(cond, body, (n_past - 1, m0))
    o_ref[...] = (acc_ref[0] * za_ref[...].astype(F32)).astype(BF16)


def _sb_sample(q, k_new, v_new, k_past, v_past, layer, za, tri, n_batch, t_new):
    past = k_past.shape[-1]
    row_spec = pl.BlockSpec((t_new, ATT_WIDTH), lambda b: (b, 0))
    new_spec = pl.BlockSpec((KEY_TILE, ATT_WIDTH), lambda b: (b, 0))
    past_spec = pl.BlockSpec((1, 1, N_HEADS, HEAD_DIM, past), lambda b: (layer, b, 0, 0, 0))
    return pl.pallas_call(
        _sb_sample_kernel,
        grid=(n_batch,),
        in_specs=[row_spec, new_spec, new_spec, past_spec, past_spec, row_spec,
                  pl.BlockSpec(tri.shape, lambda b: (0, 0))],
        out_specs=row_spec,
        out_shape=jax.ShapeDtypeStruct(q.shape, BF16),
        scratch_shapes=_sb_scratch(1, t_new),
        compiler_params=_cparams(1),
        name="sb_sample",
    )(q, k_new, v_new, k_past, v_past, za, tri)


def _outproj_body(x, ysc_ref, u_ref, zs_ref, ya_ref, gate_ref, d_ref, wglu_ref, bglu_ref, w_ref):
    y = ysc_ref[...] + d_ref[...] * u_ref[...]
    g = _gelu_tanh(y)
    glu = _sigmoid(jnp.dot(g.astype(BF16), wglu_ref[...], preferred_element_type=F32) + bglu_ref[...])
    ys = (g * glu * zs_ref[...].astype(F32)).astype(BF16)
    mix = jnp.dot(ys, w_ref[:SSM_WIDTH, :], preferred_element_type=F32)
    mix = mix + jnp.dot(ya_ref[...], w_ref[SSM_WIDTH:, :], preferred_element_type=F32)
    return x + gate_ref[0] * mix


class _OutProj(NamedTuple):
    ysc: Any
    u: Any
    zs: Any
    ya: Any
    gate: Any
    d_skip: Any
    w_glu_b: Any
    b_glu: Any
    w_out_b: Any


class _InProj(NamedTuple):
    shift: Any
    scale: Any
    norm_g: Any
    w_in_b: Any
    hsum: Any
    q_g: Any
    k_g: Any
    layer: int
    kv_all: Any


N_OUTPROJ_REFS = len(_OutProj._fields)
N_INPROJ_REFS = len(_InProj._fields) - 2


def _layer_kernel(x_ref, *refs, closes, opens, kv_aliased, kv_slot, position_minor):
    refs = list(refs)
    x = x_ref[...]
    if closes:
        out_in, refs = refs[:N_OUTPROJ_REFS], refs[N_OUTPROJ_REFS:]
    if opens:
        in_in, refs = refs[:N_INPROJ_REFS], refs[N_INPROJ_REFS + (2 if kv_aliased else 0):]
    if closes:
        x = _outproj_body(x, *out_in)
        refs.pop(0)[...] = x
    if opens:
        _inproj_body(x, *in_in, refs, kv_slot, position_minor)


def _layer_call(x2d, closing, opening, tm, rows_per_seq, depth):
    n = x2d.shape[0]
    if rows_per_seq is None:
        mod_spec = pl.BlockSpec((1, tm, D_MODEL), lambda i: (0, i, 0))
    else:
        tps = rows_per_seq // tm
        mod_spec = pl.BlockSpec((1, 1, D_MODEL), lambda i: (i // tps, 0, 0))
    row_spec = lambda w: pl.BlockSpec((tm, w), lambda i: (i, 0))
    full = lambda a: pl.BlockSpec(a.shape, lambda i: (0,) * a.ndim)
    position_minor = rows_per_seq is not None
    args, in_specs, out_specs, out_shape, aliases = [x2d], [row_spec(D_MODEL)], [], [], {}
    if closing is not None:
        args += list(closing)
        in_specs += [row_spec(SSM_WIDTH)] * 3 + [row_spec(ATT_WIDTH), mod_spec] + [full(a) for a in closing[5:]]
        out_specs.append(row_spec(D_MODEL))
        out_shape.append(jax.ShapeDtypeStruct(x2d.shape, F32))
    if opening is not None:
        kv_aliased = opening.kv_all is not None
        kv_layers, kv_first = (1, opening.layer) if kv_aliased else (depth, 0)
        bf = jax.ShapeDtypeStruct((n, SSM_WIDTH), BF16)
        if position_minor:
            f3 = jax.ShapeDtypeStruct((depth, n // rows_per_seq, N_HEADS, HEAD_DIM, rows_per_seq), F32)
            kv_spec = pl.BlockSpec((kv_layers, 1, N_HEADS, HEAD_DIM, tm),
                                   lambda i: (kv_first, i // tps, 0, 0, i % tps))
        else:
            f3 = jax.ShapeDtypeStruct((depth, n, N_HEADS, HEAD_DIM), F32)
            kv_spec = pl.BlockSpec((kv_layers, tm, N_HEADS, HEAD_DIM), lambda i: (kv_first, i, 0, 0))
        args += list(opening[:N_INPROJ_REFS])
        in_specs += [mod_spec, mod_spec] + [full(a) for a in opening[2:N_INPROJ_REFS]]
        if kv_aliased:
            assert opening.kv_all[0].shape == f3.shape and opening.kv_all[1].shape == f3.shape
            args += list(opening.kv_all)
            in_specs += [pl.BlockSpec(memory_space=pl.ANY)] * 2
            aliases = {len(args) - 2: len(out_shape) + 3, len(args) - 1: len(out_shape) + 4}
        out_specs += [row_spec(SSM_WIDTH)] * 3 + [kv_spec] * 2 + [row_spec(SSM_WIDTH)] * 3
        out_shape += [jax.ShapeDtypeStruct((n, SSM_WIDTH), F32), bf, bf, f3, f3, bf, bf, bf]
    outs = pl.pallas_call(
        functools.partial(_layer_kernel, closes=closing is not None, opens=opening is not None,
                          kv_aliased=opening is not None and opening.kv_all is not None,
                          kv_slot=None if opening is None else opening.layer,
                          position_minor=position_minor),
        grid=(n // tm,),
        in_specs=in_specs,
        out_specs=out_specs,
        out_shape=out_shape,
        input_output_aliases=aliases,
        compiler_params=_cparams(1),
        name="layer_" + "_".join(["close"] * (closing is not None) + ["open"] * (opening is not None)),
    )(*args)
    x_new = outs[0] if closing is not None else None
    opened = tuple(outs[len(outs) - 8:]) if opening is not None else None
    return x_new, opened


def _sb_sum_matrix():
    j = jnp.arange(2 * KEY_TILE)[:, None] % KEY_TILE
    s = jnp.arange(2 * KEY_TILE)[None, :]
    return -((s >= KEY_TILE) | (j > s)).astype(BF16)


def _head_sum_matrix():
    a = jnp.arange(MXU_DIM)
    return (a[:, None] // HEAD_DIM == a[None, :] // HEAD_DIM).astype(BF16)


def kernel(x_prompt, x_sample, c_prompt, c_sample, cache_k, cache_v, state_ssm_re, state_ssm_im, norm_g, w_mod, b_mod, w_in, ssm_a_re, ssm_a_im, ssm_log_dt, ssm_b_re, ssm_b_im, ssm_c_re, ssm_c_im, ssm_d, w_glu, b_glu, q_norm_g, k_norm_g, w_out):
    depth = w_in.shape[0]
    nb, seq_len, _ = x_prompt.shape
    ns, t_new, _ = x_sample.shape
    tm = 512

    mod = _modulation(jnp.concatenate([c_prompt, c_sample], axis=0), w_mod, b_mod)
    s5_ops = _s5_operators(ssm_a_re, ssm_a_im, ssm_log_dt, ssm_b_re, ssm_b_im, ssm_c_re, ssm_c_im)

    w_in_b = w_in.astype(BF16)
    w_out_b = w_out.astype(BF16)
    w_glu_b = w_glu.astype(BF16)
    tri = _sb_sum_matrix()
    hsum = _head_sum_matrix()

    xp = x_prompt.reshape(nb * seq_len, D_MODEL)
    xs = x_sample.reshape(ns * t_new, D_MODEL)
    zero_state = jnp.zeros((nb, 1, STATE_LANES), F32)
    kc = jnp.transpose(cache_k, (0, 1, 3, 4, 2))
    vc = jnp.transpose(cache_v, (0, 1, 3, 4, 2))
    outs = {name: [] for name in ("pr", "pi", "sr", "si")}
    pkv = skv = None

    def in_proj_args(l, shift, scale, kv_all):
        return _InProj(shift, scale, norm_g[l].reshape(1, D_MODEL), w_in_b[l], hsum,
                       jnp.tile(q_norm_g[l], N_HEADS).reshape(1, ATT_WIDTH),
                       jnp.tile(k_norm_g[l], N_HEADS).reshape(1, ATT_WIDTH), l, kv_all)

    def prompt_mod(l):
        mp = mod[l, :nb].reshape(nb, 1, 3 * D_MODEL)
        return tuple(mp[:, :, i * D_MODEL:(i + 1) * D_MODEL] for i in range(3))

    shift, scale, gate = prompt_mod(0)
    _, opened = _layer_call(xp, None, in_proj_args(0, shift, scale, pkv), tm, seq_len, depth)
    for l in range(depth):
        glu_w = (ssm_d[l].reshape(1, SSM_WIDTH), w_glu_b[l], b_glu[l].reshape(1, SSM_WIDTH))

        u, zs, q, k_all, v_all, kb, vb, za = opened
        pkv = (k_all, v_all)
        ysc, hfr, hfi = _s5_branch(u, zero_state, zero_state, *s5_ops, l, 1, seq_len)
        ya = _sb_prompt(q, kb, vb, za, tri, nb, seq_len)
        closing = _OutProj(ysc, u, zs, ya, gate, *glu_w, w_out_b[l])
        opening = None
        if l + 1 < depth:
            shift, scale, gate = prompt_mod(l + 1)
            opening = in_proj_args(l + 1, shift, scale, pkv)
        xp, opened = _layer_call(xp, closing, opening, tm, seq_len, depth)
        outs["pr"].append(hfr.reshape(nb, SSM_GROUPS, SSM_STATE))
        outs["pi"].append(hfi.reshape(nb, SSM_GROUPS, SSM_STATE))

        ms = jnp.repeat(mod[l, nb:], t_new, axis=0).reshape(1, ns * t_new, 3 * D_MODEL)
        s_shift, s_scale, s_gate = (ms[:, :, i * D_MODEL:(i + 1) * D_MODEL] for i in range(3))
        ts = ns * t_new
        _, (u, zs, q, k_all, v_all, kb, vb, za) = _layer_call(
            xs, None, in_proj_args(l, s_shift, s_scale, skv), ts, None, depth)
        skv = (k_all, v_all)
        h0r = state_ssm_re[l].reshape(ns, 1, STATE_LANES)
        h0i = state_ssm_im[l].reshape(ns, 1, STATE_LANES)
        ysc, hfr, hfi = _s5_branch(u, h0r, h0i, *s5_ops, l, ns, t_new)
        pad = lambda a: jnp.pad(a.reshape(ns, t_new, ATT_WIDTH),
                                ((0, 0), (0, KEY_TILE - t_new), (0, 0))).reshape(ns * KEY_TILE, ATT_WIDTH)
        ya = _sb_sample(q, pad(kb), pad(vb), kc, vc, l, za, tri, ns, t_new)
        xs, _ = _layer_call(xs, _OutProj(ysc, u, zs, ya, s_gate, *glu_w, w_out_b[l]), None, ts, None, depth)
        outs["sr"].append(hfr.reshape(ns, SSM_GROUPS, SSM_STATE))
        outs["si"].append(hfi.reshape(ns, SSM_GROUPS, SSM_STATE))

    st = lambda name: jnp.stack(outs[name])
    heads_p = lambda a: jnp.transpose(a, (0, 1, 4, 2, 3))
    heads_s = lambda a: a.reshape(depth, ns, t_new, N_HEADS, HEAD_DIM)
    return (xp.reshape(nb, seq_len, D_MODEL), xs.reshape(ns, t_new, D_MODEL),
            heads_p(pkv[0]), heads_p(pkv[1]), st("pr"), st("pi"),
            heads_s(skv[0]), heads_s(skv[1]), st("sr"), st("si"))
```

```python
import functools
import math
from typing import Any, NamedTuple

import jax
import jax.numpy as jnp
from jax import lax
from jax.experimental import pallas as pl
from jax.experimental.pallas import tpu as pltpu

F32 = jnp.float32
BF16 = jnp.bfloat16

D_MODEL = 1024
SSM_WIDTH = 512
SSM_GROUP = 16
SSM_GROUPS = 32
SSM_STATE = 64
STATE_LANES = SSM_GROUPS * SSM_STATE
ATT_WIDTH = 512
HEAD_DIM = 64
N_HEADS = 8
IN_WIDTH = 2 * SSM_WIDTH + 4 * ATT_WIDTH
EPS = 1e-6

LANES = 128
SUBLANES = 8
MXU_DIM = 256
HEADS_PER_LANE_BLOCK = LANES // HEAD_DIM
GROUPS_PER_LANE_BLOCK = LANES // SSM_GROUP
N_LANE_BLOCKS = SSM_WIDTH // LANES
N_PAIRS = N_HEADS // HEADS_PER_LANE_BLOCK
S5_CHUNK = SUBLANES
KEY_TILE = 128
SB_Q_ROWS = 32
SB_STEP_BLOCKS = 16
LOG2_E = math.log2(math.e)
SB_LOG2_CUTOFF = -40.0 * LOG2_E
VMEM_LIMIT = 56 * 1024 * 1024


def _cparams(n_axes):
    return pltpu.CompilerParams(dimension_semantics=("arbitrary",) * n_axes,
                                vmem_limit_bytes=VMEM_LIMIT)


def _silu(x):
    return x * (1.0 / (1.0 + jnp.exp(-x)))


def _sigmoid(x):
    return 1.0 / (1.0 + jnp.exp(-x))


def _gelu_tanh(x):
    c = math.sqrt(2.0 / math.pi)
    return 0.5 * x * (1.0 + jnp.tanh(c * (x + 0.044715 * (x * x * x))))


def _mod_kernel(c_ref, w_ref, b_ref, o_ref):
    c = c_ref[...]
    a = _silu(c)
    o_ref[0] = jnp.dot(a, w_ref[0], preferred_element_type=F32,
                       precision=lax.Precision.HIGHEST) + b_ref[0]


def _modulation(c_all, w_mod, b_mod):
    depth = w_mod.shape[0]
    n = c_all.shape[0]
    nt = 3
    return pl.pallas_call(
        _mod_kernel,
        grid=(depth, nt),
        in_specs=[pl.BlockSpec((n, D_MODEL), lambda l, j: (0, 0)),
                  pl.BlockSpec((1, D_MODEL, D_MODEL), lambda l, j: (l, 0, j)),
                  pl.BlockSpec((1, 1, D_MODEL), lambda l, j: (l, 0, j))],
        out_specs=pl.BlockSpec((1, n, D_MODEL), lambda l, j: (l, 0, j)),
        out_shape=jax.ShapeDtypeStruct((depth, n, 3 * D_MODEL), F32),
        compiler_params=_cparams(2),
        name="modulation",
    )(c_all, w_mod, b_mod.reshape(depth, 1, 3 * D_MODEL))


def _disc_kernel(lr_ref, li_ref, ldt_ref, bre_ref, bim_ref, cre_ref, cim_ref,
                 pw8r_ref, pw8i_ref, lbr_ref, lbi_ref, clr_ref, cli_ref):
    lr = lr_ref[...]
    li = li_ref[...]
    dt = jnp.exp(ldt_ref[...])
    mag = jnp.exp(lr * dt)
    ang = li * dt
    ab_re = mag * jnp.cos(ang)
    ab_im = mag * jnp.sin(ang)
    den = lr * lr + li * li
    nr = ab_re - 1.0
    f_re = (nr * lr + ab_im * li) / den
    f_im = (ab_im * lr - nr * li) / den
    bre = bre_ref[...]
    bim = bim_ref[...]
    cre = cre_ref[...]
    cim = cim_ref[...]
    xr = f_re[:, None, :] * bre - f_im[:, None, :] * bim
    xi = f_re[:, None, :] * bim + f_im[:, None, :] * bre
    pr, pi_ = ab_re, ab_im
    for t in range(S5_CHUNK):
        lbr_ref[t] = xr
        lbi_ref[t] = xi
        clr_ref[t] = cre * pr[:, None, :] - cim * pi_[:, None, :]
        cli_ref[t] = cre * pi_[:, None, :] + cim * pr[:, None, :]
        if t + 1 < S5_CHUNK:
            xr, xi = (xr * ab_re[:, None, :] - xi * ab_im[:, None, :],
                      xr * ab_im[:, None, :] + xi * ab_re[:, None, :])
            pr, pi_ = pr * ab_re - pi_ * ab_im, pr * ab_im + pi_ * ab_re
    a8r, a8i = pr, pi_
    qr, qi = a8r, a8i
    for m in range(SUBLANES):
        pw8r_ref[m] = qr
        pw8i_ref[m] = qi
        qr, qi = qr * a8r - qi * a8i, qr * a8i + qi * a8r


def _place_kernel(lbr_ref, lbi_ref, clr_ref, cli_ref, cre_ref, cim_ref, bm_ref, ct_ref, kt_ref, kcat):
    gb = GROUPS_PER_LANE_BLOCK
    half = gb * SSM_STATE
    rep_p = (lax.broadcasted_iota(jnp.int32, (SSM_STATE, half), 0)
             == lax.broadcasted_iota(jnp.int32, (SSM_STATE, half), 1) % SSM_STATE).astype(BF16)
    rep_c = (lax.broadcasted_iota(jnp.int32, (SSM_GROUP, LANES), 0)
             == lax.broadcasted_iota(jnp.int32, (SSM_GROUP, LANES), 1) % SSM_GROUP).astype(BF16)
    same_p = (lax.broadcasted_iota(jnp.int32, (LANES, half), 0) // SSM_GROUP
              == lax.broadcasted_iota(jnp.int32, (LANES, half), 1) // SSM_STATE)
    same_c = (lax.broadcasted_iota(jnp.int32, (LANES, LANES), 0) // SSM_GROUP
              == lax.broadcasted_iota(jnp.int32, (LANES, LANES), 1) // SSM_GROUP)

    def place_states(x):
        tiled = jnp.dot(x.reshape(LANES, SSM_STATE).astype(BF16), rep_p, preferred_element_type=F32)
        return jnp.where(same_p, tiled, 0.0).astype(BF16)

    for s in range(S5_CHUNK):
        rows = slice(s * LANES, (s + 1) * LANES)
        bm_ref[0, rows, :half] = place_states(lbr_ref[S5_CHUNK - 1 - s])
        bm_ref[0, rows, half:] = place_states(lbi_ref[S5_CHUNK - 1 - s])
        ct_ref[0, rows, :half] = place_states(clr_ref[s])
        ct_ref[0, rows, half:] = place_states(-cli_ref[s])

    for g in range(gb):
        b_cat = jnp.concatenate([lbr_ref[:, g].reshape(LANES, SSM_STATE),
                                 lbi_ref[:, g].reshape(LANES, SSM_STATE)], axis=1)
        c_cat = jnp.concatenate([cre_ref[g], -cim_ref[g]], axis=1)
        kg = lax.dot_general(b_cat, c_cat, (((1,), (1,)), ((), ())), preferred_element_type=F32,
                             precision=lax.Precision.HIGHEST)
        for j in range(S5_CHUNK):
            kcat[j, g * SSM_GROUP:(g + 1) * SSM_GROUP, :] = kg[j * SSM_GROUP:(j + 1) * SSM_GROUP, :]
    zero = jnp.zeros((LANES, LANES), BF16)
    for j in range(S5_CHUNK):
        tiled = jnp.dot(kcat[j].astype(BF16), rep_c, preferred_element_type=F32)
        blk = jnp.where(same_c, tiled, 0.0).astype(BF16)
        for s in range(S5_CHUNK - j):
            kt_ref[0, s * LANES:(s + 1) * LANES, (s + j) * LANES:(s + j + 1) * LANES] = blk
        if j > 0:
            for t in range(S5_CHUNK - j):
                kt_ref[0, (t + j) * LANES:(t + j + 1) * LANES, t * LANES:(t + 1) * LANES] = zero


def _s5_operators(a_re, a_im, log_dt, b_re, b_im, c_re, c_im):
    depth = a_re.shape[0]
    rows = depth * SSM_GROUPS
    lr = a_re.reshape(rows, SSM_STATE)
    li = a_im.reshape(rows, SSM_STATE)
    ldt = jnp.broadcast_to(log_dt.reshape(rows, 1), (rows, SSM_STATE))
    bre = b_re.transpose(0, 1, 3, 2).reshape(rows, SSM_GROUP, SSM_STATE)
    bim = b_im.transpose(0, 1, 3, 2).reshape(rows, SSM_GROUP, SSM_STATE)
    cre = c_re.reshape(rows, SSM_GROUP, SSM_STATE)
    cim = c_im.reshape(rows, SSM_GROUP, SSM_STATE)
    pw_shape = jax.ShapeDtypeStruct((SUBLANES, rows, SSM_STATE), F32)
    op_shape = jax.ShapeDtypeStruct((S5_CHUNK, rows, SSM_GROUP, SSM_STATE), F32)
    pw8r, pw8i, lbr, lbi, clr, cli = pl.pallas_call(
        _disc_kernel,
        out_shape=(pw_shape, pw_shape, op_shape, op_shape, op_shape, op_shape),
        compiler_params=_cparams(0),
        name="s5_discretise",
    )(lr, li, ldt, bre, bim, cre, cim)
    gb = GROUPS_PER_LANE_BLOCK
    n_blk = rows // gb
    width = S5_CHUNK * LANES
    op_spec = pl.BlockSpec((S5_CHUNK, gb, SSM_GROUP, SSM_STATE), lambda i: (0, i, 0, 0))
    c_spec = pl.BlockSpec((gb, SSM_GROUP, SSM_STATE), lambda i: (i, 0, 0))
    mat_spec = pl.BlockSpec((1, width, width), lambda i: (i, 0, 0))
    mat_shape = jax.ShapeDtypeStruct((n_blk, width, width), BF16)
    bmat, ctm, ktoep = pl.pallas_call(
        _place_kernel,
        grid=(n_blk,),
        in_specs=[op_spec] * 4 + [c_spec] * 2,
        out_specs=[mat_spec] * 3,
        out_shape=(mat_shape,) * 3,
        scratch_shapes=[pltpu.VMEM((S5_CHUNK, LANES, SSM_GROUP), F32)],
        compiler_params=_cparams(1),
        name="s5_place_operators",
    )(lbr, lbi, clr, cli, cre, cim)

    pw8r = pw8r.reshape(SUBLANES, depth, STATE_LANES).transpose(1, 0, 2)
    pw8i = pw8i.reshape(SUBLANES, depth, STATE_LANES).transpose(1, 0, 2)
    row = jnp.arange(SUBLANES)[None, :, None]

    def step_mult(pw):
        return jnp.stack([jnp.where(row >= k, pw[:, k - 1][:, None, :], 0.0) for k in (1, 2, 4)], axis=1)

    return bmat, ktoep, ctm, step_mult(pw8r), step_mult(pw8i), pw8r, pw8i


def _heads_major(x, position_minor):
    if position_minor:
        return x.T.reshape(N_HEADS, HEAD_DIM, x.shape[0])
    heads = jnp.stack([x[:, h * HEAD_DIM:(h + 1) * HEAD_DIM] for h in range(N_HEADS)], axis=0)
    return pltpu.einshape("htd->thd", heads)


def _inproj_body(x, shift_ref, scale_ref, g_ref, w_ref, hsum_ref, qg_ref, kg_ref, outs, kv_slot,
                 position_minor):
    u_ref, zs_ref, q_ref, k_ref, v_ref, kb_ref, vb_ref, za_ref = outs

    def store_kv(ref, value):
        slot = kv_slot if ref.shape[0] > 1 else 0
        for s in range(ref.shape[0]):
            ref[s] = (_heads_major(value, position_minor).reshape(ref.shape[1:]) if s == slot
                      else jnp.zeros(ref.shape[1:], F32))

    ms = jnp.mean(x * x, axis=-1, keepdims=True)
    h = x * lax.rsqrt(ms + EPS) * g_ref[...]
    h = h * (1.0 + scale_ref[0]) + shift_ref[0]
    hb = h.astype(BF16)

    def proj(c):
        return jnp.dot(hb, w_ref[:, c * SSM_WIDTH:(c + 1) * SSM_WIDTH], preferred_element_type=F32)

    def head_norm(p, g):
        sq = (p * p).astype(BF16)
        width = hsum_ref.shape[0]
        ss = jnp.concatenate([jnp.dot(sq[:, c:c + width], hsum_ref[...], preferred_element_type=F32)
                              for c in range(0, ATT_WIDTH, width)], axis=1)
        return p * lax.rsqrt(ss * (1.0 / HEAD_DIM) + EPS) * g

    u_ref[...] = proj(0)
    zs_ref[...] = _silu(proj(1)).astype(BF16)
    q = head_norm(proj(2), qg_ref[...])
    q_ref[...] = (q * (HEAD_DIM ** -0.5 * LOG2_E)).astype(BF16)
    k = head_norm(proj(3), kg_ref[...])
    store_kv(k_ref, k)
    kb_ref[...] = k.astype(BF16)
    v = proj(4)
    store_kv(v_ref, v)
    vb_ref[...] = v.astype(BF16)
    za_ref[...] = _silu(proj(5)).astype(BF16)


def _s5_kernel(u_ref, h0r_ref, h0i_ref, bm_ref, kt_ref, ct_ref, mr_ref, mi_ref, pr_ref, pi_ref,
               y_ref, hfr_ref, hfi_ref,
               uflat, wre, wim, hpr, hpi, *, n_seq, lane_chunk):
    rows = u_ref.shape[0]
    n_chunks = rows // S5_CHUNK
    seq_chunks = n_chunks // n_seq
    half = GROUPS_PER_LANE_BLOCK * SSM_STATE

    def token_rows(c, t):
        if n_seq == 1:
            return pl.ds(t, n_chunks, stride=S5_CHUNK)
        return pl.ds(c * S5_CHUNK + t, n_seq, stride=seq_chunks * S5_CHUNK)

    fold_blocks = [(0, slice(None))] if n_seq == 1 else [(c, slice(c * n_seq, (c + 1) * n_seq))
                                                        for c in range(seq_chunks)]
    for c, frows in fold_blocks:
        for t in range(S5_CHUNK):
            uflat[frows, t * LANES:(t + 1) * LANES] = u_ref[token_rows(c, t), :].astype(BF16)
    w = jnp.dot(uflat[...], bm_ref[...], preferred_element_type=F32)
    wre[...] = w[:, :half]
    wim[...] = w[:, half:]
    yf = jnp.concatenate([jnp.dot(uflat[:, :c + MXU_DIM], kt_ref[:c + MXU_DIM, c:c + MXU_DIM],
                                  preferred_element_type=F32)
                          for c in range(0, S5_CHUNK * LANES, MXU_DIM)], axis=1)

    if n_seq == 1:
        hpr[0:SUBLANES, :] = jnp.broadcast_to(h0r_ref[0], (SUBLANES, half))
        hpi[0:SUBLANES, :] = jnp.broadcast_to(h0i_ref[0], (SUBLANES, half))
        for ch in range(half // lane_chunk):
            sl = slice(ch * lane_chunk, (ch + 1) * lane_chunk)
            prc = pr_ref[:, sl]
            pic = pi_ref[:, sl]

            def group_body(r, carry):
                cr, ci = carry
                row = pl.multiple_of(r * SUBLANES, SUBLANES)
                vr = wre[pl.ds(row, SUBLANES), sl]
                vi = wim[pl.ds(row, SUBLANES), sl]
                for idx, k in enumerate((1, 2, 4)):
                    mr = mr_ref[idx, :, sl]
                    mi = mi_ref[idx, :, sl]
                    sr = pltpu.roll(vr, k, 0)
                    si = pltpu.roll(vi, k, 0)
                    vr, vi = vr + (mr * sr - mi * si), vi + (mr * si + mi * sr)
                hr = vr + (prc * cr - pic * ci)
                hi = vi + (prc * ci + pic * cr)
                hpr[pl.ds(row + SUBLANES, SUBLANES), sl] = hr
                hpi[pl.ds(row + SUBLANES, SUBLANES), sl] = hi
                last = SUBLANES - 1
                return (jnp.broadcast_to(hr[last:last + 1], hr.shape),
                        jnp.broadcast_to(hi[last:last + 1], hi.shape))

            cr, ci = lax.fori_loop(0, n_chunks // SUBLANES, group_body,
                                   (hpr[0:SUBLANES, sl], hpi[0:SUBLANES, sl]), unroll=True)
            hfr_ref[0, :, sl] = cr[0:1]
            hfi_ref[0, :, sl] = ci[0:1]
        h_prev_r = hpr[SUBLANES - 1:SUBLANES - 1 + n_chunks, :]
        h_prev_i = hpi[SUBLANES - 1:SUBLANES - 1 + n_chunks, :]
    else:
        a8r = pr_ref[0:1, :]
        a8i = pi_ref[0:1, :]
        hr = h0r_ref[...].reshape(n_seq, half)
        hi = h0i_ref[...].reshape(n_seq, half)
        for c, frows in fold_blocks:
            hpr[frows, :] = hr
            hpi[frows, :] = hi
            wr = wre[frows, :]
            wi = wim[frows, :]
            hr, hi = a8r * hr - a8i * hi + wr, a8r * hi + a8i * hr + wi
        hfr_ref[...] = hr.reshape(hfr_ref.shape)
        hfi_ref[...] = hi.reshape(hfi_ref.shape)
        h_prev_r = hpr[0:n_chunks, :]
        h_prev_i = hpi[0:n_chunks, :]

    h_prev = jnp.concatenate([h_prev_r.astype(BF16), h_prev_i.astype(BF16)], axis=1)
    yf = yf + lax.dot_general(h_prev, ct_ref[...], (((1,), (1,)), ((), ())), preferred_element_type=F32)
    for c, frows in fold_blocks:
        for t in range(S5_CHUNK):
            y_ref[token_rows(c, t), :] = yf[frows, t * LANES:(t + 1) * LANES]


def _s5_branch(u, h0_re, h0_im, bmat, ktoep, ctm, mr, mi, pr, pi_, layer, n_seq, seq_rows):
    n = u.shape[0]
    tile = n_seq * seq_rows
    n_total = h0_re.shape[0]
    n_chunks = tile // S5_CHUNK
    half = GROUPS_PER_LANE_BLOCK * SSM_STATE
    row_spec = pl.BlockSpec((tile, LANES), lambda lb, b: (b, lb))
    st_spec = pl.BlockSpec((n_seq, 1, half), lambda lb, b: (b, 0, lb))
    hp_rows = n_chunks + SUBLANES if n_seq == 1 else n_chunks
    mat_spec = pl.BlockSpec((None,) + bmat.shape[1:], lambda lb, b: (layer * N_LANE_BLOCKS + lb, 0, 0))
    kern = functools.partial(_s5_kernel, n_seq=n_seq, lane_chunk=256)
    return pl.pallas_call(
        kern,
        grid=(N_LANE_BLOCKS, n // tile),
        in_specs=[row_spec, st_spec, st_spec,
                  mat_spec, mat_spec, mat_spec,
                  pl.BlockSpec((None, 3, SUBLANES, half), lambda lb, b: (layer, 0, 0, lb)),
                  pl.BlockSpec((None, 3, SUBLANES, half), lambda lb, b: (layer, 0, 0, lb)),
                  pl.BlockSpec((None, SUBLANES, half), lambda lb, b: (layer, 0, lb)),
                  pl.BlockSpec((None, SUBLANES, half), lambda lb, b: (layer, 0, lb))],
        out_specs=[row_spec, st_spec, st_spec],
        out_shape=(jax.ShapeDtypeStruct((n, SSM_WIDTH), F32),
                   jax.ShapeDtypeStruct((n_total, 1, STATE_LANES), F32),
                   jax.ShapeDtypeStruct((n_total, 1, STATE_LANES), F32)),
        scratch_shapes=[pltpu.VMEM((n_chunks, S5_CHUNK * LANES), BF16),
                        pltpu.VMEM((n_chunks, half), F32), pltpu.VMEM((n_chunks, half), F32),
                        pltpu.VMEM((hp_rows, half), F32), pltpu.VMEM((hp_rows, half), F32)],
        compiler_params=_cparams(2),
        name="s5_branch",
    )(u, h0_re, h0_im, bmat, ktoep, ctm, mr, mi, pr, pi_)


def _sb_mask_queries(q_rows, qm_ref, tq):
    lane = lax.broadcasted_iota(jnp.int32, (tq, LANES), 1)
    for j in range(N_PAIRS):
        qpair = q_rows(j).astype(F32)
        for hh in range(HEADS_PER_LANE_BLOCK):
            h = j * HEADS_PER_LANE_BLOCK + hh
            in_head = (lane >= hh * HEAD_DIM) & (lane < (hh + 1) * HEAD_DIM)
            qm_ref[h * tq:(h + 1) * tq, :] = jnp.where(in_head, qpair, 0.0).astype(BF16)


class _SbTile(NamedTuple):
    tq: int
    k_tile: Any
    v_tile: Any
    keys_minor: bool
    valid: Any
    first: bool
    qm: Any
    lb: Any
    hl: Any
    a: Any
    car: Any
    acc: Any


def _sb_scores(t):
    pair = HEADS_PER_LANE_BLOCK * t.tq
    valid2 = None if t.valid is None else jnp.concatenate([t.valid] * HEADS_PER_LANE_BLOCK, axis=0)
    for j in range(N_PAIRS):
        rows = slice(j * pair, (j + 1) * pair)
        contract_k = 0 if t.keys_minor else 1
        z = lax.dot_general(t.qm[rows, :], t.k_tile(j), (((1,), (contract_k,)), ((), ())),
                            preferred_element_type=F32)
        sp = jnp.maximum(z, 0.0) + jnp.log2(1.0 + jnp.exp2(-jnp.abs(z)))
        t.lb[rows, :] = z - sp
        if valid2 is not None:
            sp = jnp.where(valid2, sp, 0.0)
        hi = sp.astype(BF16)
        t.hl[rows, :KEY_TILE] = hi
        t.hl[rows, KEY_TILE:] = (sp - hi.astype(F32)).astype(BF16)


def _sb_weights(t, tri_ref):
    cs = jnp.dot(t.hl[...], tri_ref[...], preferred_element_type=F32)
    cmax = None
    for h in range(N_HEADS):
        rows = slice(h * t.tq, (h + 1) * t.tq)
        after = cs[rows, :KEY_TILE]
        total = cs[rows, KEY_TILE:]
        if not t.first:
            carry = t.car[rows, :]
            after = after + carry
            total = total + carry
        a = jnp.exp2(t.lb[rows, :] + after)
        if t.valid is not None:
            a = jnp.where(t.valid, a, 0.0)
        t.a[rows, :] = a.astype(BF16)
        t.car[rows, :] = total
        cmax = total if cmax is None else jnp.maximum(cmax, total)
    return jnp.max(cmax)


def _sb_output(t):
    pair = HEADS_PER_LANE_BLOCK * t.tq
    lane = lax.broadcasted_iota(jnp.int32, (t.tq, LANES), 1)
    for j in range(N_PAIRS):
        contract_v = 1 if t.keys_minor else 0
        pv = lax.dot_general(t.a[j * pair:(j + 1) * pair, :], t.v_tile(j), (((1,), (contract_v,)), ((), ())),
                             preferred_element_type=F32)
        out = jnp.where(lane < HEAD_DIM, pv[:t.tq], pv[t.tq:])
        if t.first:
            t.acc[:, j * LANES:(j + 1) * LANES] = out
        else:
            t.acc[:, j * LANES:(j + 1) * LANES] += out


def _sb_tiles(tiles, tri_ref):
    for t in tiles:
        _sb_scores(t)
    ms = [_sb_weights(t, tri_ref) for t in tiles]
    for t in tiles:
        _sb_output(t)
    return ms


def _sb_scratch(n_blocks, tq):
    rows = N_HEADS * tq
    return [pltpu.VMEM((n_blocks, rows, LANES), BF16),
            pltpu.VMEM((n_blocks, rows, LANES), F32),
            pltpu.VMEM((n_blocks, rows, 2 * KEY_TILE), BF16),
            pltpu.VMEM((n_blocks, rows, LANES), BF16),
            pltpu.VMEM((n_blocks, rows, LANES), F32),
            pltpu.VMEM((n_blocks, tq, ATT_WIDTH), F32),
            pltpu.SMEM((n_blocks,), F32)]


def _sb_prompt_kernel(q_ref, k_ref, v_ref, za_ref, tri_ref, o_ref,
                      qm_ref, lb_ref, hl_ref, a_ref, car_ref, acc_ref, m_ref):
    tq = SB_Q_ROWS
    n_blocks = q_ref.shape[0] // tq
    step_row0 = pl.program_id(1) * (n_blocks * tq)
    row = lax.broadcasted_iota(jnp.int32, (tq, KEY_TILE), 0)
    col = lax.broadcasted_iota(jnp.int32, (tq, KEY_TILE), 1)

    def tiles(ref, start):
        return lambda j: ref[pl.ds(start, KEY_TILE), j * LANES:(j + 1) * LANES]

    def band_start(s):
        return pl.multiple_of(jnp.maximum(step_row0 + (s + 1) * tq - KEY_TILE, 0), tq)

    def tile(s, lo, valid, first):
        return _SbTile(tq, tiles(k_ref, lo), tiles(v_ref, lo), False, valid, first, qm_ref.at[s], lb_ref.at[s],
                       hl_ref.at[s], a_ref.at[s], car_ref.at[s], acc_ref.at[s])

    band = []
    for s in range(n_blocks):
        lo = band_start(s)
        _sb_mask_queries(lambda j: q_ref[s * tq:(s + 1) * tq, j * LANES:(j + 1) * LANES], qm_ref.at[s], tq)
        valid = (col - row) < (step_row0 + s * tq - lo)
        band.append(tile(s, lo, valid, True))
    for s, m in enumerate(_sb_tiles(band, tri_ref)):
        m_ref[s] = m

    def more_tiles(s, _):
        def cond(state):
            hi, m = state
            return (hi > 0) & (m > SB_LOG2_CUTOFF)

        def body(state):
            hi, _ = state
            lo = pl.multiple_of(jnp.maximum(hi - KEY_TILE, 0), tq)
            valid = col < (hi - lo)
            (m,) = _sb_tiles([tile(s, lo, valid, False)], tri_ref)
            return lo, m

        lax.while_loop(cond, body, (band_start(s), m_ref[s]))
        return 0

    lax.fori_loop(0, n_blocks, more_tiles, 0)
    for s in range(n_blocks):
        rows = slice(s * tq, (s + 1) * tq)
        o_ref[rows, :] = (acc_ref[s] * za_ref[rows, :].astype(F32)).astype(BF16)


def _sb_prompt(q, kb, vb, za, tri, n_batch, seq_len):
    step_rows = SB_STEP_BLOCKS * SB_Q_ROWS
    nq = seq_len // step_rows
    row_spec = pl.BlockSpec((step_rows, ATT_WIDTH), lambda b, i: (b * nq + i, 0))
    seq_spec = pl.BlockSpec((seq_len, ATT_WIDTH), lambda b, i: (b, 0))
    return pl.pallas_call(
        _sb_prompt_kernel,
        grid=(n_batch, nq),
        in_specs=[row_spec, seq_spec, seq_spec, row_spec, pl.BlockSpec(tri.shape, lambda b, i: (0, 0))],
        out_specs=row_spec,
        out_shape=jax.ShapeDtypeStruct(q.shape, BF16),
        scratch_shapes=_sb_scratch(SB_STEP_BLOCKS, SB_Q_ROWS),
        compiler_params=_cparams(2),
        name="sb_prompt",
    )(q, kb, vb, za, tri)


def _sb_sample_kernel(q_ref, kn_ref, vn_ref, kc_hbm, vc_hbm, za_ref, tri_ref, o_ref,
                      qm_ref, lb_ref, hl_ref, a_ref, car_ref, acc_ref, m_ref, kbuf, vbuf, sem, *, layer):
    tq = q_ref.shape[0]
    n_past = kc_hbm.shape[-1] // KEY_TILE
    b = pl.program_id(0)

    def tile_copies(t):
        start = pl.multiple_of(t * KEY_TILE, KEY_TILE)
        return (pltpu.make_async_copy(kc_hbm.at[layer, b, :, :, pl.ds(start, KEY_TILE)], kbuf, sem.at[0]),
                pltpu.make_async_copy(vc_hbm.at[layer, b, :, :, pl.ds(start, KEY_TILE)], vbuf, sem.at[1]))

    def fetch(t):
        for c in tile_copies(t):
            c.start()

    def arrive(t):
        for c in tile_copies(t):
            c.wait()

    fetch(n_past - 1)
    _sb_mask_queries(lambda j: q_ref[:, j * LANES:(j + 1) * LANES], qm_ref.at[0], tq)
    row = lax.broadcasted_iota(jnp.int32, (tq, KEY_TILE), 0)
    col = lax.broadcasted_iota(jnp.int32, (tq, KEY_TILE), 1)
    scratch = (qm_ref.at[0], lb_ref.at[0], hl_ref.at[0], a_ref.at[0], car_ref.at[0], acc_ref.at[0])

    new_tile = lambda ref: (lambda j: ref[:, j * LANES:(j + 1) * LANES])
    (m0,) = _sb_tiles([_SbTile(tq, new_tile(kn_ref), new_tile(vn_ref), False, col < row, True, *scratch)],
                      tri_ref)

    arrive(n_past - 1)

    def past_tile(buf):
        return lambda j: buf[j * HEADS_PER_LANE_BLOCK:(j + 1) * HEADS_PER_LANE_BLOCK].reshape(
            LANES, KEY_TILE).astype(BF16)

    def cond(state):
        t, m = state
        return (t >= 0) & (m > SB_LOG2_CUTOFF)

    def body(state):
        t, _ = state
        (m,) = _sb_tiles([_SbTile(tq, past_tile(kbuf), past_tile(vbuf), True, None, False, *scratch)],
                         tri_ref)

        @pl.when(cond((t - 1, m)))
        def _():
            fetch(t - 1)
            arrive(t - 1)

        return t - 1, m

    lax.while_loop(cond, body, (n_past - 1, m0))
    o_ref[...] = (acc_ref[0] * za_ref[...].astype(F32)).astype(BF16)


def _sb_sample(q, k_new, v_new, k_past, v_past, layer, za, tri, n_batch, t_new):
    row_spec = pl.BlockSpec((t_new, ATT_WIDTH), lambda b: (b, 0))
    new_spec = pl.BlockSpec((KEY_TILE, ATT_WIDTH), lambda b: (b, 0))
    past_spec = pl.BlockSpec(memory_space=pl.ANY)
    tile_buf = pltpu.VMEM((N_HEADS, HEAD_DIM, KEY_TILE), F32)
    return pl.pallas_call(
        functools.partial(_sb_sample_kernel, layer=layer),
        grid=(n_batch,),
        in_specs=[row_spec, new_spec, new_spec, past_spec, past_spec, row_spec,
                  pl.BlockSpec(tri.shape, lambda b: (0, 0))],
        out_specs=row_spec,
        out_shape=jax.ShapeDtypeStruct(q.shape, BF16),
        scratch_shapes=_sb_scratch(1, t_new) + [tile_buf, tile_buf, pltpu.SemaphoreType.DMA((2,))],
        compiler_params=_cparams(1),
        name="sb_sample",
    )(q, k_new, v_new, k_past, v_past, za, tri)


def _outproj_body(x, ysc_ref, u_ref, zs_ref, ya_ref, gate_ref, d_ref, wglu_ref, bglu_ref, w_ref):
    y = ysc_ref[...] + d_ref[...] * u_ref[...]
    g = _gelu_tanh(y)
    glu = _sigmoid(jnp.dot(g.astype(BF16), wglu_ref[...], preferred_element_type=F32) + bglu_ref[...])
    ys = (g * glu * zs_ref[...].astype(F32)).astype(BF16)
    mix = jnp.dot(ys, w_ref[:SSM_WIDTH, :], preferred_element_type=F32)
    mix = mix + jnp.dot(ya_ref[...], w_ref[SSM_WIDTH:, :], preferred_element_type=F32)
    return x + gate_ref[0] * mix


class _OutProj(NamedTuple):
    ysc: Any
    u: Any
    zs: Any
    ya: Any
    gate: Any
    d_skip: Any
    w_glu_b: Any
    b_glu: Any
    w_out_b: Any


class _InProj(NamedTuple):
    shift: Any
    scale: Any
    norm_g: Any
    w_in_b: Any
    hsum: Any
    q_g: Any
    k_g: Any
    layer: int
    kv_all: Any


N_OUTPROJ_REFS = len(_OutProj._fields)
N_INPROJ_REFS = len(_InProj._fields) - 2


def _layer_kernel(x_ref, *refs, closes, opens, kv_aliased, kv_slot, position_minor):
    refs = list(refs)
    x = x_ref[...]
    if closes:
        out_in, refs = refs[:N_OUTPROJ_REFS], refs[N_OUTPROJ_REFS:]
    if opens:
        in_in, refs = refs[:N_INPROJ_REFS], refs[N_INPROJ_REFS + (2 if kv_aliased else 0):]
    if closes:
        x = _outproj_body(x, *out_in)
        refs.pop(0)[...] = x
    if opens:
        _inproj_body(x, *in_in, refs, kv_slot, position_minor)


def _layer_call(x2d, closing, opening, tm, rows_per_seq, depth):
    n = x2d.shape[0]
    if rows_per_seq is None:
        mod_spec = pl.BlockSpec((1, tm, D_MODEL), lambda i: (0, i, 0))
    else:
        tps = rows_per_seq // tm
        mod_spec = pl.BlockSpec((1, 1, D_MODEL), lambda i: (i // tps, 0, 0))
    row_spec = lambda w: pl.BlockSpec((tm, w), lambda i: (i, 0))
    full = lambda a: pl.BlockSpec(a.shape, lambda i: (0,) * a.ndim)
    position_minor = rows_per_seq is not None
    args, in_specs, out_specs, out_shape, aliases = [x2d], [row_spec(D_MODEL)], [], [], {}
    if closing is not None:
        args += list(closing)
        in_specs += [row_spec(SSM_WIDTH)] * 3 + [row_spec(ATT_WIDTH), mod_spec] + [full(a) for a in closing[5:]]
        out_specs.append(row_spec(D_MODEL))
        out_shape.append(jax.ShapeDtypeStruct(x2d.shape, F32))
    if opening is not None:
        kv_aliased = opening.kv_all is not None
        kv_layers, kv_first = (1, opening.layer) if kv_aliased else (depth, 0)
        bf = jax.ShapeDtypeStruct((n, SSM_WIDTH), BF16)
        if position_minor:
            f3 = jax.ShapeDtypeStruct((depth, n // rows_per_seq, N_HEADS, HEAD_DIM, rows_per_seq), F32)
            kv_spec = pl.BlockSpec((kv_layers, 1, N_HEADS, HEAD_DIM, tm),
                                   lambda i: (kv_first, i // tps, 0, 0, i % tps))
        else:
            f3 = jax.ShapeDtypeStruct((depth, n, N_HEADS, HEAD_DIM), F32)
            kv_spec = pl.BlockSpec((kv_layers, tm, N_HEADS, HEAD_DIM), lambda i: (kv_first, i, 0, 0))
        args += list(opening[:N_INPROJ_REFS])
        in_specs += [mod_spec, mod_spec] + [full(a) for a in opening[2:N_INPROJ_REFS]]
        if kv_aliased:
            assert opening.kv_all[0].shape == f3.shape and opening.kv_all[1].shape == f3.shape
            args += list(opening.kv_all)
            in_specs += [pl.BlockSpec(memory_space=pl.ANY)] * 2
            aliases = {len(args) - 2: len(out_shape) + 3, len(args) - 1: len(out_shape) + 4}
        out_specs += [row_spec(SSM_WIDTH)] * 3 + [kv_spec] * 2 + [row_spec(SSM_WIDTH)] * 3
        out_shape += [jax.ShapeDtypeStruct((n, SSM_WIDTH), F32), bf, bf, f3, f3, bf, bf, bf]
    outs = pl.pallas_call(
        functools.partial(_layer_kernel, closes=closing is not None, opens=opening is not None,
                          kv_aliased=opening is not None and opening.kv_all is not None,
                          kv_slot=None if opening is None else opening.layer,
                          position_minor=position_minor),
        grid=(n // tm,),
        in_specs=in_specs,
        out_specs=out_specs,
        out_shape=out_shape,
        input_output_aliases=aliases,
        compiler_params=_cparams(1),
        name="layer_" + "_".join(["close"] * (closing is not None) + ["open"] * (opening is not None)),
    )(*args)
    x_new = outs[0] if closing is not None else None
    opened = tuple(outs[len(outs) - 8:]) if opening is not None else None
    return x_new, opened


def _sb_sum_matrix():
    j = jnp.arange(2 * KEY_TILE)[:, None] % KEY_TILE
    s = jnp.arange(2 * KEY_TILE)[None, :]
    return -((s >= KEY_TILE) | (j > s)).astype(BF16)


def _head_sum_matrix():
    a = jnp.arange(MXU_DIM)
    return (a[:, None] // HEAD_DIM == a[None, :] // HEAD_DIM).astype(BF16)


def kernel(x_prompt, x_sample, c_prompt, c_sample, cache_k, cache_v, state_ssm_re, state_ssm_im, norm_g, w_mod, b_mod, w_in, ssm_a_re, ssm_a_im, ssm_log_dt, ssm_b_re, ssm_b_im, ssm_c_re, ssm_c_im, ssm_d, w_glu, b_glu, q_norm_g, k_norm_g, w_out):
    depth = w_in.shape[0]
    nb, seq_len, _ = x_prompt.shape
    ns, t_new, _ = x_sample.shape
    tm = 512

    mod = _modulation(jnp.concatenate([c_prompt, c_sample], axis=0), w_mod, b_mod)
    s5_ops = _s5_operators(ssm_a_re, ssm_a_im, ssm_log_dt, ssm_b_re, ssm_b_im, ssm_c_re, ssm_c_im)

    w_in_b = w_in.astype(BF16)
    w_out_b = w_out.astype(BF16)
    w_glu_b = w_glu.astype(BF16)
    tri = _sb_sum_matrix()
    hsum = _head_sum_matrix()

    xp = x_prompt.reshape(nb * seq_len, D_MODEL)
    xs = x_sample.reshape(ns * t_new, D_MODEL)
    zero_state = jnp.zeros((nb, 1, STATE_LANES), F32)
    kc = jnp.transpose(cache_k, (0, 1, 3, 4, 2))
    vc = jnp.transpose(cache_v, (0, 1, 3, 4, 2))
    outs = {name: [] for name in ("pr", "pi", "sr", "si")}
    pkv = skv = None

    def in_proj_args(l, shift, scale, kv_all):
        return _InProj(shift, scale, norm_g[l].reshape(1, D_MODEL), w_in_b[l], hsum,
                       jnp.tile(q_norm_g[l], N_HEADS).reshape(1, ATT_WIDTH),
                       jnp.tile(k_norm_g[l], N_HEADS).reshape(1, ATT_WIDTH), l, kv_all)

    def prompt_mod(l):
        mp = mod[l, :nb].reshape(nb, 1, 3 * D_MODEL)
        return tuple(mp[:, :, i * D_MODEL:(i + 1) * D_MODEL] for i in range(3))

    shift, scale, gate = prompt_mod(0)
    _, opened = _layer_call(xp, None, in_proj_args(0, shift, scale, pkv), tm, seq_len, depth)
    for l in range(depth):
        glu_w = (ssm_d[l].reshape(1, SSM_WIDTH), w_glu_b[l], b_glu[l].reshape(1, SSM_WIDTH))

        u, zs, q, k_all, v_all, kb, vb, za = opened
        pkv = (k_all, v_all)
        ysc, hfr, hfi = _s5_branch(u, zero_state, zero_state, *s5_ops, l, 1, seq_len)
        ya = _sb_prompt(q, kb, vb, za, tri, nb, seq_len)
        closing = _OutProj(ysc, u, zs, ya, gate, *glu_w, w_out_b[l])
        opening = None
        if l + 1 < depth:
            shift, scale, gate = prompt_mod(l + 1)
            opening = in_proj_args(l + 1, shift, scale, pkv)
        xp, opened = _layer_call(xp, closing, opening, tm, seq_len, depth)
        outs["pr"].append(hfr.reshape(nb, SSM_GROUPS, SSM_STATE))
        outs["pi"].append(hfi.reshape(nb, SSM_GROUPS, SSM_STATE))

        ms = jnp.repeat(mod[l, nb:], t_new, axis=0).reshape(1, ns * t_new, 3 * D_MODEL)
        s_shift, s_scale, s_gate = (ms[:, :, i * D_MODEL:(i + 1) * D_MODEL] for i in range(3))
        ts = ns * t_new
        _, (u, zs, q, k_all, v_all, kb, vb, za) = _layer_call(
            xs, None, in_proj_args(l, s_shift, s_scale, skv), ts, None, depth)
        skv = (k_all, v_all)
        h0r = state_ssm_re[l].reshape(ns, 1, STATE_LANES)
        h0i = state_ssm_im[l].reshape(ns, 1, STATE_LANES)
        ysc, hfr, hfi = _s5_branch(u, h0r, h0i, *s5_ops, l, ns, t_new)
        pad = lambda a: jnp.pad(a.reshape(ns, t_new, ATT_WIDTH),
                                ((0, 0), (0, KEY_TILE - t_new), (0, 0))).reshape(ns * KEY_TILE, ATT_WIDTH)
        ya = _sb_sample(q, pad(kb), pad(vb), kc, vc, l, za, tri, ns, t_new)
        xs, _ = _layer_call(xs, _OutProj(ysc, u, zs, ya, s_gate, *glu_w, w_out_b[l]), None, ts, None, depth)
        outs["sr"].append(hfr.reshape(ns, SSM_GROUPS, SSM_STATE))
        outs["si"].append(hfi.reshape(ns, SSM_GROUPS, SSM_STATE))

    st = lambda name: jnp.stack(outs[name])
    heads_p = lambda a: jnp.transpose(a, (0, 1, 4, 2, 3))
    heads_s = lambda a: a.reshape(depth, ns, t_new, N_HEADS, HEAD_DIM)
    return (xp.reshape(nb, seq_len, D_MODEL), xs.reshape(ns, t_new, D_MODEL),
            heads_p(pkv[0]), heads_p(pkv[1]), st("pr"), st("pi"),
            heads_s(skv[0]), heads_s(skv[1]), st("sr"), st("si"))
```

```python
import functools
import math
from typing import Any, NamedTuple

import jax
import jax.numpy as jnp
from jax import lax
from jax.experimental import pallas as pl
from jax.experimental.pallas import tpu as pltpu

F32 = jnp.float32
BF16 = jnp.bfloat16

D_MODEL = 1024
SSM_WIDTH = 512
SSM_GROUP = 16
SSM_GROUPS = 32
SSM_STATE = 64
STATE_LANES = SSM_GROUPS * SSM_STATE
ATT_WIDTH = 512
HEAD_DIM = 64
N_HEADS = 8
IN_WIDTH = 2 * SSM_WIDTH + 4 * ATT_WIDTH
EPS = 1e-6

LANES = 128
SUBLANES = 8
MXU_DIM = 256
HEADS_PER_LANE_BLOCK = LANES // HEAD_DIM
GROUPS_PER_LANE_BLOCK = LANES // SSM_GROUP
N_LANE_BLOCKS = SSM_WIDTH // LANES
N_PAIRS = N_HEADS // HEADS_PER_LANE_BLOCK
S5_CHUNK = SUBLANES
KEY_TILE = 128
SB_Q_ROWS = 32
SB_STEP_BLOCKS = 32
LOG2_E = math.log2(math.e)
SB_LOG2_CUTOFF = -40.0 * LOG2_E
VMEM_LIMIT = 56 * 1024 * 1024


def _cparams(n_axes):
    return pltpu.CompilerParams(dimension_semantics=("arbitrary",) * n_axes,
                                vmem_limit_bytes=VMEM_LIMIT)


def _silu(x):
    return x * (1.0 / (1.0 + jnp.exp(-x)))


def _sigmoid(x):
    return 1.0 / (1.0 + jnp.exp(-x))


def _gelu_tanh(x):
    c = math.sqrt(2.0 / math.pi)
    return 0.5 * x * (1.0 + jnp.tanh(c * (x + 0.044715 * (x * x * x))))


def _mod_kernel(c_ref, w_ref, b_ref, o_ref):
    c = c_ref[...]
    a = _silu(c)
    o_ref[0] = jnp.dot(a, w_ref[0], preferred_element_type=F32,
                       precision=lax.Precision.HIGHEST) + b_ref[0]


def _modulation(c_all, w_mod, b_mod):
    depth = w_mod.shape[0]
    n = c_all.shape[0]
    nt = 3
    return pl.pallas_call(
        _mod_kernel,
        grid=(depth, nt),
        in_specs=[pl.BlockSpec((n, D_MODEL), lambda l, j: (0, 0)),
                  pl.BlockSpec((1, D_MODEL, D_MODEL), lambda l, j: (l, 0, j)),
                  pl.BlockSpec((1, 1, D_MODEL), lambda l, j: (l, 0, j))],
        out_specs=pl.BlockSpec((1, n, D_MODEL), lambda l, j: (l, 0, j)),
        out_shape=jax.ShapeDtypeStruct((depth, n, 3 * D_MODEL), F32),
        compiler_params=_cparams(2),
        name="modulation",
    )(c_all, w_mod, b_mod.reshape(depth, 1, 3 * D_MODEL))


def _disc_kernel(lr_ref, li_ref, ldt_ref, bre_ref, bim_ref, cre_ref, cim_ref,
                 pw8r_ref, pw8i_ref, lbr_ref, lbi_ref, clr_ref, cli_ref):
    lr = lr_ref[...]
    li = li_ref[...]
    dt = jnp.exp(ldt_ref[...])
    mag = jnp.exp(lr * dt)
    ang = li * dt
    ab_re = mag * jnp.cos(ang)
    ab_im = mag * jnp.sin(ang)
    den = lr * lr + li * li
    nr = ab_re - 1.0
    f_re = (nr * lr + ab_im * li) / den
    f_im = (ab_im * lr - nr * li) / den
    bre = bre_ref[...]
    bim = bim_ref[...]
    cre = cre_ref[...]
    cim = cim_ref[...]
    xr = f_re[:, None, :] * bre - f_im[:, None, :] * bim
    xi = f_re[:, None, :] * bim + f_im[:, None, :] * bre
    pr, pi_ = ab_re, ab_im
    for t in range(S5_CHUNK):
        lbr_ref[t] = xr
        lbi_ref[t] = xi
        clr_ref[t] = cre * pr[:, None, :] - cim * pi_[:, None, :]
        cli_ref[t] = cre * pi_[:, None, :] + cim * pr[:, None, :]
        if t + 1 < S5_CHUNK:
            xr, xi = (xr * ab_re[:, None, :] - xi * ab_im[:, None, :],
                      xr * ab_im[:, None, :] + xi * ab_re[:, None, :])
            pr, pi_ = pr * ab_re - pi_ * ab_im, pr * ab_im + pi_ * ab_re
    a8r, a8i = pr, pi_
    qr, qi = a8r, a8i
    for m in range(SUBLANES):
        pw8r_ref[m] = qr
        pw8i_ref[m] = qi
        qr, qi = qr * a8r - qi * a8i, qr * a8i + qi * a8r


def _place_kernel(lbr_ref, lbi_ref, clr_ref, cli_ref, cre_ref, cim_ref, bm_ref, ct_ref, kt_ref, kcat):
    gb = GROUPS_PER_LANE_BLOCK
    half = gb * SSM_STATE
    rep_p = (lax.broadcasted_iota(jnp.int32, (SSM_STATE, half), 0)
             == lax.broadcasted_iota(jnp.int32, (SSM_STATE, half), 1) % SSM_STATE).astype(BF16)
    rep_c = (lax.broadcasted_iota(jnp.int32, (SSM_GROUP, LANES), 0)
             == lax.broadcasted_iota(jnp.int32, (SSM_GROUP, LANES), 1) % SSM_GROUP).astype(BF16)
    same_p = (lax.broadcasted_iota(jnp.int32, (LANES, half), 0) // SSM_GROUP
              == lax.broadcasted_iota(jnp.int32, (LANES, half), 1) // SSM_STATE)
    same_c = (lax.broadcasted_iota(jnp.int32, (LANES, LANES), 0) // SSM_GROUP
              == lax.broadcasted_iota(jnp.int32, (LANES, LANES), 1) // SSM_GROUP)

    def place_states(x):
        tiled = jnp.dot(x.reshape(LANES, SSM_STATE).astype(BF16), rep_p, preferred_element_type=F32)
        return jnp.where(same_p, tiled, 0.0).astype(BF16)

    for s in range(S5_CHUNK):
        rows = slice(s * LANES, (s + 1) * LANES)
        bm_ref[0, rows, :half] = place_states(lbr_ref[S5_CHUNK - 1 - s])
        bm_ref[0, rows, half:] = place_states(lbi_ref[S5_CHUNK - 1 - s])
        ct_ref[0, rows, :half] = place_states(clr_ref[s])
        ct_ref[0, rows, half:] = place_states(-cli_ref[s])

    for g in range(gb):
        b_cat = jnp.concatenate([lbr_ref[:, g].reshape(LANES, SSM_STATE),
                                 lbi_ref[:, g].reshape(LANES, SSM_STATE)], axis=1)
        c_cat = jnp.concatenate([cre_ref[g], -cim_ref[g]], axis=1)
        kg = lax.dot_general(b_cat, c_cat, (((1,), (1,)), ((), ())), preferred_element_type=F32,
                             precision=lax.Precision.HIGHEST)
        for j in range(S5_CHUNK):
            kcat[j, g * SSM_GROUP:(g + 1) * SSM_GROUP, :] = kg[j * SSM_GROUP:(j + 1) * SSM_GROUP, :]
    zero = jnp.zeros((LANES, LANES), BF16)
    for j in range(S5_CHUNK):
        tiled = jnp.dot(kcat[j].astype(BF16), rep_c, preferred_element_type=F32)
        blk = jnp.where(same_c, tiled, 0.0).astype(BF16)
        for s in range(S5_CHUNK - j):
            kt_ref[0, s * LANES:(s + 1) * LANES, (s + j) * LANES:(s + j + 1) * LANES] = blk
        if j > 0:
            for t in range(S5_CHUNK - j):
                kt_ref[0, (t + j) * LANES:(t + j + 1) * LANES, t * LANES:(t + 1) * LANES] = zero


def _s5_operators(a_re, a_im, log_dt, b_re, b_im, c_re, c_im):
    depth = a_re.shape[0]
    rows = depth * SSM_GROUPS
    lr = a_re.reshape(rows, SSM_STATE)
    li = a_im.reshape(rows, SSM_STATE)
    ldt = jnp.broadcast_to(log_dt.reshape(rows, 1), (rows, SSM_STATE))
    bre = b_re.transpose(0, 1, 3, 2).reshape(rows, SSM_GROUP, SSM_STATE)
    bim = b_im.transpose(0, 1, 3, 2).reshape(rows, SSM_GROUP, SSM_STATE)
    cre = c_re.reshape(rows, SSM_GROUP, SSM_STATE)
    cim = c_im.reshape(rows, SSM_GROUP, SSM_STATE)
    pw_shape = jax.ShapeDtypeStruct((SUBLANES, rows, SSM_STATE), F32)
    op_shape = jax.ShapeDtypeStruct((S5_CHUNK, rows, SSM_GROUP, SSM_STATE), F32)
    pw8r, pw8i, lbr, lbi, clr, cli = pl.pallas_call(
        _disc_kernel,
        out_shape=(pw_shape, pw_shape, op_shape, op_shape, op_shape, op_shape),
        compiler_params=_cparams(0),
        name="s5_discretise",
    )(lr, li, ldt, bre, bim, cre, cim)
    gb = GROUPS_PER_LANE_BLOCK
    n_blk = rows // gb
    width = S5_CHUNK * LANES
    op_spec = pl.BlockSpec((S5_CHUNK, gb, SSM_GROUP, SSM_STATE), lambda i: (0, i, 0, 0))
    c_spec = pl.BlockSpec((gb, SSM_GROUP, SSM_STATE), lambda i: (i, 0, 0))
    mat_spec = pl.BlockSpec((1, width, width), lambda i: (i, 0, 0))
    mat_shape = jax.ShapeDtypeStruct((n_blk, width, width), BF16)
    bmat, ctm, ktoep = pl.pallas_call(
        _place_kernel,
        grid=(n_blk,),
        in_specs=[op_spec] * 4 + [c_spec] * 2,
        out_specs=[mat_spec] * 3,
        out_shape=(mat_shape,) * 3,
        scratch_shapes=[pltpu.VMEM((S5_CHUNK, LANES, SSM_GROUP), F32)],
        compiler_params=_cparams(1),
        name="s5_place_operators",
    )(lbr, lbi, clr, cli, cre, cim)

    pw8r = pw8r.reshape(SUBLANES, depth, STATE_LANES).transpose(1, 0, 2)
    pw8i = pw8i.reshape(SUBLANES, depth, STATE_LANES).transpose(1, 0, 2)
    row = jnp.arange(SUBLANES)[None, :, None]

    def step_mult(pw):
        return jnp.stack([jnp.where(row >= k, pw[:, k - 1][:, None, :], 0.0) for k in (1, 2, 4)], axis=1)

    return bmat, ktoep, ctm, step_mult(pw8r), step_mult(pw8i), pw8r, pw8i


def _heads_major(x, position_minor):
    if position_minor:
        return x.T.reshape(N_HEADS, HEAD_DIM, x.shape[0])
    heads = jnp.stack([x[:, h * HEAD_DIM:(h + 1) * HEAD_DIM] for h in range(N_HEADS)], axis=0)
    return pltpu.einshape("htd->thd", heads)


def _inproj_body(x, shift_ref, scale_ref, g_ref, w_ref, hsum_ref, qg_ref, kg_ref, outs, kv_slot,
                 position_minor):
    u_ref, zs_ref, q_ref, k_ref, v_ref, kb_ref, vb_ref, za_ref = outs

    def store_kv(ref, value):
        slot = kv_slot if ref.shape[0] > 1 else 0
        for s in range(ref.shape[0]):
            ref[s] = (_heads_major(value, position_minor).reshape(ref.shape[1:]) if s == slot
                      else jnp.zeros(ref.shape[1:], F32))

    ms = jnp.mean(x * x, axis=-1, keepdims=True)
    h = x * lax.rsqrt(ms + EPS) * g_ref[...]
    h = h * (1.0 + scale_ref[0]) + shift_ref[0]
    hb = h.astype(BF16)

    def proj(c):
        return jnp.dot(hb, w_ref[:, c * SSM_WIDTH:(c + 1) * SSM_WIDTH], preferred_element_type=F32)

    def head_norm(p, g):
        sq = (p * p).astype(BF16)
        width = hsum_ref.shape[0]
        ss = jnp.concatenate([jnp.dot(sq[:, c:c + width], hsum_ref[...], preferred_element_type=F32)
                              for c in range(0, ATT_WIDTH, width)], axis=1)
        return p * lax.rsqrt(ss * (1.0 / HEAD_DIM) + EPS) * g

    u_ref[...] = proj(0)
    zs_ref[...] = _silu(proj(1)).astype(BF16)
    q = head_norm(proj(2), qg_ref[...])
    q_ref[...] = (q * (HEAD_DIM ** -0.5 * LOG2_E)).astype(BF16)
    k = head_norm(proj(3), kg_ref[...])
    store_kv(k_ref, k)
    kb_ref[...] = k.astype(BF16)
    v = proj(4)
    store_kv(v_ref, v)
    vb_ref[...] = v.astype(BF16)
    za_ref[...] = _silu(proj(5)).astype(BF16)


def _s5_kernel(u_ref, h0r_ref, h0i_ref, bm_ref, kt_ref, ct_ref, mr_ref, mi_ref, pr_ref, pi_ref,
               y_ref, hfr_ref, hfi_ref,
               uflat, wre, wim, hpr, hpi, *, n_seq, lane_chunk):
    rows = u_ref.shape[0]
    n_chunks = rows // S5_CHUNK
    seq_chunks = n_chunks // n_seq
    half = GROUPS_PER_LANE_BLOCK * SSM_STATE

    def token_rows(c, t):
        if n_seq == 1:
            return pl.ds(t, n_chunks, stride=S5_CHUNK)
        return pl.ds(c * S5_CHUNK + t, n_seq, stride=seq_chunks * S5_CHUNK)

    fold_blocks = [(0, slice(None))] if n_seq == 1 else [(c, slice(c * n_seq, (c + 1) * n_seq))
                                                        for c in range(seq_chunks)]
    for c, frows in fold_blocks:
        for t in range(S5_CHUNK):
            uflat[frows, t * LANES:(t + 1) * LANES] = u_ref[token_rows(c, t), :].astype(BF16)
    w = jnp.dot(uflat[...], bm_ref[...], preferred_element_type=F32)
    wre[...] = w[:, :half]
    wim[...] = w[:, half:]
    yf = jnp.concatenate([jnp.dot(uflat[:, :c + MXU_DIM], kt_ref[:c + MXU_DIM, c:c + MXU_DIM],
                                  preferred_element_type=F32)
                          for c in range(0, S5_CHUNK * LANES, MXU_DIM)], axis=1)

    if n_seq == 1:
        hpr[0:SUBLANES, :] = jnp.broadcast_to(h0r_ref[0], (SUBLANES, half))
        hpi[0:SUBLANES, :] = jnp.broadcast_to(h0i_ref[0], (SUBLANES, half))
        for ch in range(half // lane_chunk):
            sl = slice(ch * lane_chunk, (ch + 1) * lane_chunk)
            prc = pr_ref[:, sl]
            pic = pi_ref[:, sl]

            def group_body(r, carry):
                cr, ci = carry
                row = pl.multiple_of(r * SUBLANES, SUBLANES)
                vr = wre[pl.ds(row, SUBLANES), sl]
                vi = wim[pl.ds(row, SUBLANES), sl]
                for idx, k in enumerate((1, 2, 4)):
                    mr = mr_ref[idx, :, sl]
                    mi = mi_ref[idx, :, sl]
                    sr = pltpu.roll(vr, k, 0)
                    si = pltpu.roll(vi, k, 0)
                    vr, vi = vr + (mr * sr - mi * si), vi + (mr * si + mi * sr)
                hr = vr + (prc * cr - pic * ci)
                hi = vi + (prc * ci + pic * cr)
                hpr[pl.ds(row + SUBLANES, SUBLANES), sl] = hr
                hpi[pl.ds(row + SUBLANES, SUBLANES), sl] = hi
                last = SUBLANES - 1
                return (jnp.broadcast_to(hr[last:last + 1], hr.shape),
                        jnp.broadcast_to(hi[last:last + 1], hi.shape))

            cr, ci = lax.fori_loop(0, n_chunks // SUBLANES, group_body,
                                   (hpr[0:SUBLANES, sl], hpi[0:SUBLANES, sl]), unroll=True)
            hfr_ref[0, :, sl] = cr[0:1]
            hfi_ref[0, :, sl] = ci[0:1]
        h_prev_r = hpr[SUBLANES - 1:SUBLANES - 1 + n_chunks, :]
        h_prev_i = hpi[SUBLANES - 1:SUBLANES - 1 + n_chunks, :]
    else:
        a8r = pr_ref[0:1, :]
        a8i = pi_ref[0:1, :]
        hr = h0r_ref[...].reshape(n_seq, half)
        hi = h0i_ref[...].reshape(n_seq, half)
        for c, frows in fold_blocks:
            hpr[frows, :] = hr
            hpi[frows, :] = hi
            wr = wre[frows, :]
            wi = wim[frows, :]
            hr, hi = a8r * hr - a8i * hi + wr, a8r * hi + a8i * hr + wi
        hfr_ref[...] = hr.reshape(hfr_ref.shape)
        hfi_ref[...] = hi.reshape(hfi_ref.shape)
        h_prev_r = hpr[0:n_chunks, :]
        h_prev_i = hpi[0:n_chunks, :]

    h_prev = jnp.concatenate([h_prev_r.astype(BF16), h_prev_i.astype(BF16)], axis=1)
    yf = yf + lax.dot_general(h_prev, ct_ref[...], (((1,), (1,)), ((), ())), preferred_element_type=F32)
    for c, frows in fold_blocks:
        for t in range(S5_CHUNK):
            y_ref[token_rows(c, t), :] = yf[frows, t * LANES:(t + 1) * LANES]


def _s5_branch(u, h0_re, h0_im, bmat, ktoep, ctm, mr, mi, pr, pi_, layer, n_seq, seq_rows):
    n = u.shape[0]
    tile = n_seq * seq_rows
    n_total = h0_re.shape[0]
    n_chunks = tile // S5_CHUNK
    half = GROUPS_PER_LANE_BLOCK * SSM_STATE
    row_spec = pl.BlockSpec((tile, LANES), lambda lb, b: (b, lb))
    st_spec = pl.BlockSpec((n_seq, 1, half), lambda lb, b: (b, 0, lb))
    hp_rows = n_chunks + SUBLANES if n_seq == 1 else n_chunks
    mat_spec = pl.BlockSpec((None,) + bmat.shape[1:], lambda lb, b: (layer * N_LANE_BLOCKS + lb, 0, 0))
    kern = functools.partial(_s5_kernel, n_seq=n_seq, lane_chunk=256)
    return pl.pallas_call(
        kern,
        grid=(N_LANE_BLOCKS, n // tile),
        in_specs=[row_spec, st_spec, st_spec,
                  mat_spec, mat_spec, mat_spec,
                  pl.BlockSpec((None, 3, SUBLANES, half), lambda lb, b: (layer, 0, 0, lb)),
                  pl.BlockSpec((None, 3, SUBLANES, half), lambda lb, b: (layer, 0, 0, lb)),
                  pl.BlockSpec((None, SUBLANES, half), lambda lb, b: (layer, 0, lb)),
                  pl.BlockSpec((None, SUBLANES, half), lambda lb, b: (layer, 0, lb))],
        out_specs=[row_spec, st_spec, st_spec],
        out_shape=(jax.ShapeDtypeStruct((n, SSM_WIDTH), F32),
                   jax.ShapeDtypeStruct((n_total, 1, STATE_LANES), F32),
                   jax.ShapeDtypeStruct((n_total, 1, STATE_LANES), F32)),
        scratch_shapes=[pltpu.VMEM((n_chunks, S5_CHUNK * LANES), BF16),
                        pltpu.VMEM((n_chunks, half), F32), pltpu.VMEM((n_chunks, half), F32),
                        pltpu.VMEM((hp_rows, half), F32), pltpu.VMEM((hp_rows, half), F32)],
        compiler_params=_cparams(2),
        name="s5_branch",
    )(u, h0_re, h0_im, bmat, ktoep, ctm, mr, mi, pr, pi_)


def _sb_mask_queries(q_rows, qm_ref, tq):
    lane = lax.broadcasted_iota(jnp.int32, (tq, LANES), 1)
    for j in range(N_PAIRS):
        qpair = q_rows(j).astype(F32)
        for hh in range(HEADS_PER_LANE_BLOCK):
            h = j * HEADS_PER_LANE_BLOCK + hh
            in_head = (lane >= hh * HEAD_DIM) & (lane < (hh + 1) * HEAD_DIM)
            qm_ref[h * tq:(h + 1) * tq, :] = jnp.where(in_head, qpair, 0.0).astype(BF16)


class _SbTile(NamedTuple):
    tq: int
    k_tile: Any
    v_tile: Any
    keys_minor: bool
    valid: Any
    first: bool
    qm: Any
    lb: Any
    hl: Any
    a: Any
    car: Any
    acc: Any


def _sb_scores(t):
    pair = HEADS_PER_LANE_BLOCK * t.tq
    valid2 = None if t.valid is None else jnp.concatenate([t.valid] * HEADS_PER_LANE_BLOCK, axis=0)
    for j in range(N_PAIRS):
        rows = slice(j * pair, (j + 1) * pair)
        contract_k = 0 if t.keys_minor else 1
        z = lax.dot_general(t.qm[rows, :], t.k_tile(j), (((1,), (contract_k,)), ((), ())),
                            preferred_element_type=F32)
        sp = jnp.maximum(z, 0.0) + jnp.log2(1.0 + jnp.exp2(-jnp.abs(z)))
        t.lb[rows, :] = z - sp
        if valid2 is not None:
            sp = jnp.where(valid2, sp, 0.0)
        hi = sp.astype(BF16)
        t.hl[rows, :KEY_TILE] = hi
        t.hl[rows, KEY_TILE:] = (sp - hi.astype(F32)).astype(BF16)


def _sb_weights(t, tri_ref):
    cs = jnp.dot(t.hl[...], tri_ref[...], preferred_element_type=F32)
    cmax = None
    for h in range(N_HEADS):
        rows = slice(h * t.tq, (h + 1) * t.tq)
        after = cs[rows, :KEY_TILE]
        total = cs[rows, KEY_TILE:]
        if not t.first:
            carry = t.car[rows, :]
            after = after + carry
            total = total + carry
        a = jnp.exp2(t.lb[rows, :] + after)
        if t.valid is not None:
            a = jnp.where(t.valid, a, 0.0)
        t.a[rows, :] = a.astype(BF16)
        t.car[rows, :] = total
        cmax = total if cmax is None else jnp.maximum(cmax, total)
    return jnp.max(cmax)


def _sb_output(t):
    pair = HEADS_PER_LANE_BLOCK * t.tq
    lane = lax.broadcasted_iota(jnp.int32, (t.tq, LANES), 1)
    for j in range(N_PAIRS):
        contract_v = 1 if t.keys_minor else 0
        pv = lax.dot_general(t.a[j * pair:(j + 1) * pair, :], t.v_tile(j), (((1,), (contract_v,)), ((), ())),
                             preferred_element_type=F32)
        out = jnp.where(lane < HEAD_DIM, pv[:t.tq], pv[t.tq:])
        if t.first:
            t.acc[:, j * LANES:(j + 1) * LANES] = out
        else:
            t.acc[:, j * LANES:(j + 1) * LANES] += out


def _sb_tiles(tiles, tri_ref):
    for t in tiles:
        _sb_scores(t)
    ms = [_sb_weights(t, tri_ref) for t in tiles]
    for t in tiles:
        _sb_output(t)
    return ms


def _sb_scratch(n_blocks, tq):
    rows = N_HEADS * tq
    return [pltpu.VMEM((n_blocks, rows, LANES), BF16),
            pltpu.VMEM((n_blocks, rows, LANES), F32),
            pltpu.VMEM((n_blocks, rows, 2 * KEY_TILE), BF16),
            pltpu.VMEM((n_blocks, rows, LANES), BF16),
            pltpu.VMEM((n_blocks, rows, LANES), F32),
            pltpu.VMEM((n_blocks, tq, ATT_WIDTH), F32),
            pltpu.SMEM((n_blocks,), F32)]


def _sb_prompt_kernel(q_ref, k_ref, v_ref, za_ref, tri_ref, o_ref,
                      qm_ref, lb_ref, hl_ref, a_ref, car_ref, acc_ref, m_ref):
    tq = SB_Q_ROWS
    n_blocks = q_ref.shape[0] // tq
    step_row0 = pl.program_id(1) * (n_blocks * tq)
    row = lax.broadcasted_iota(jnp.int32, (tq, KEY_TILE), 0)
    col = lax.broadcasted_iota(jnp.int32, (tq, KEY_TILE), 1)

    def tiles(ref, start):
        return lambda j: ref[pl.ds(start, KEY_TILE), j * LANES:(j + 1) * LANES]

    def band_start(s):
        return pl.multiple_of(jnp.maximum(step_row0 + (s + 1) * tq - KEY_TILE, 0), tq)

    def tile(s, lo, valid, first):
        return _SbTile(tq, tiles(k_ref, lo), tiles(v_ref, lo), False, valid, first, qm_ref.at[s], lb_ref.at[s],
                       hl_ref.at[s], a_ref.at[s], car_ref.at[s], acc_ref.at[s])

    band = []
    for s in range(n_blocks):
        lo = band_start(s)
        _sb_mask_queries(lambda j: q_ref[s * tq:(s + 1) * tq, j * LANES:(j + 1) * LANES], qm_ref.at[s], tq)
        valid = (col - row) < (step_row0 + s * tq - lo)
        band.append(tile(s, lo, valid, True))
    for s, m in enumerate(_sb_tiles(band, tri_ref)):
        m_ref[s] = m

    def more_tiles(s, _):
        def cond(state):
            hi, m = state
            return (hi > 0) & (m > SB_LOG2_CUTOFF)

        def body(state):
            hi, _ = state
            lo = pl.multiple_of(jnp.maximum(hi - KEY_TILE, 0), tq)
            valid = col < (hi - lo)
            (m,) = _sb_tiles([tile(s, lo, valid, False)], tri_ref)
            return lo, m

        lax.while_loop(cond, body, (band_start(s), m_ref[s]))
        return 0

    lax.fori_loop(0, n_blocks, more_tiles, 0)
    for s in range(n_blocks):
        rows = slice(s * tq, (s + 1) * tq)
        o_ref[rows, :] = (acc_ref[s] * za_ref[rows, :].astype(F32)).astype(BF16)


def _sb_prompt(q, kb, vb, za, tri, n_batch, seq_len):
    step_rows = SB_STEP_BLOCKS * SB_Q_ROWS
    nq = seq_len // step_rows
    row_spec = pl.BlockSpec((step_rows, ATT_WIDTH), lambda b, i: (b * nq + i, 0))
    seq_spec = pl.BlockSpec((seq_len, ATT_WIDTH), lambda b, i: (b, 0))
    return pl.pallas_call(
        _sb_prompt_kernel,
        grid=(n_batch, nq),
        in_specs=[row_spec, seq_spec, seq_spec, row_spec, pl.BlockSpec(tri.shape, lambda b, i: (0, 0))],
        out_specs=row_spec,
        out_shape=jax.ShapeDtypeStruct(q.shape, BF16),
        scratch_shapes=_sb_scratch(SB_STEP_BLOCKS, SB_Q_ROWS),
        compiler_params=_cparams(2),
        name="sb_prompt",
    )(q, kb, vb, za, tri)


def _sb_sample_kernel(q_ref, kn_ref, vn_ref, kc_ref, vc_ref, za_ref, tri_ref, o_ref,
                      qm_ref, lb_ref, hl_ref, a_ref, car_ref, acc_ref, m_ref):
    tq = q_ref.shape[0]
    n_past = kc_ref.shape[-1] // KEY_TILE
    _sb_mask_queries(lambda j: q_ref[:, j * LANES:(j + 1) * LANES], qm_ref.at[0], tq)
    row = lax.broadcasted_iota(jnp.int32, (tq, KEY_TILE), 0)
    col = lax.broadcasted_iota(jnp.int32, (tq, KEY_TILE), 1)
    scratch = (qm_ref.at[0], lb_ref.at[0], hl_ref.at[0], a_ref.at[0], car_ref.at[0], acc_ref.at[0])

    new_tile = lambda ref: (lambda j: ref[:, j * LANES:(j + 1) * LANES])
    (m0,) = _sb_tiles([_SbTile(tq, new_tile(kn_ref), new_tile(vn_ref), False, col < row, True, *scratch)],
                      tri_ref)

    def past_tile(ref, t):
        start = pl.multiple_of(t * KEY_TILE, KEY_TILE)
        return lambda j: ref[0, 0, j * HEADS_PER_LANE_BLOCK:(j + 1) * HEADS_PER_LANE_BLOCK, :,
                             pl.ds(start, KEY_TILE)].reshape(LANES, KEY_TILE).astype(BF16)

    def cond(state):
        t, m = state
        return (t >= 0) & (m > SB_LOG2_CUTOFF)

    def body(state):
        t, _ = state
        (m,) = _sb_tiles([_SbTile(tq, past_tile(kc_ref, t), past_tile(vc_ref, t), True, None, False,
                                  *scratch)], tri_ref)
        return t - 1, m

    lax.while_loop(cond, body, (n_past - 1, m0))
    o_ref[...] = (acc_ref[0] * za_ref[...].astype(F32)).astype(BF16)


def _sb_sample(q, k_new, v_new, k_past, v_past, layer, za, tri, n_batch, t_new):
    past = k_past.shape[-1]
    row_spec = pl.BlockSpec((t_new, ATT_WIDTH), lambda b: (b, 0))
    new_spec = pl.BlockSpec((KEY_TILE, ATT_WIDTH), lambda b: (b, 0))
    past_spec = pl.BlockSpec((1, 1, N_HEADS, HEAD_DIM, past), lambda b: (layer, b, 0, 0, 0))
    return pl.pallas_call(
        _sb_sample_kernel,
        grid=(n_batch,),
        in_specs=[row_spec, new_spec, new_spec, past_spec, past_spec, row_spec,
                  pl.BlockSpec(tri.shape, lambda b: (0, 0))],
        out_specs=row_spec,
        out_shape=jax.ShapeDtypeStruct(q.shape, BF16),
        scratch_shapes=_sb_scratch(1, t_new),
        compiler_params=_cparams(1),
        name="sb_sample",
    )(q, k_new, v_new, k_past, v_past, za, tri)


def _outproj_body(x, ysc_ref, u_ref, zs_ref, ya_ref, gate_ref, d_ref, wglu_ref, bglu_ref, w_ref):
    y = ysc_ref[...] + d_ref[...] * u_ref[...]
    g = _gelu_tanh(y)
    glu = _sigmoid(jnp.dot(g.astype(BF16), wglu_ref[...], preferred_element_type=F32) + bglu_ref[...])
    ys = (g * glu * zs_ref[...].astype(F32)).astype(BF16)
    mix = jnp.dot(ys, w_ref[:SSM_WIDTH, :], preferred_element_type=F32)
    mix = mix + jnp.dot(ya_ref[...], w_ref[SSM_WIDTH:, :], preferred_element_type=F32)
    return x + gate_ref[0] * mix


class _OutProj(NamedTuple):
    ysc: Any
    u: Any
    zs: Any
    ya: Any
    gate: Any
    d_skip: Any
    w_glu_b: Any
    b_glu: Any
    w_out_b: Any


class _InProj(NamedTuple):
    shift: Any
    scale: Any
    norm_g: Any
    w_in_b: Any
    hsum: Any
    q_g: Any
    k_g: Any
    layer: int
    kv_all: Any


N_OUTPROJ_REFS = len(_OutProj._fields)
N_INPROJ_REFS = len(_InProj._fields) - 2


def _layer_kernel(x_ref, *refs, closes, opens, kv_aliased, kv_slot, position_minor):
    refs = list(refs)
    x = x_ref[...]
    if closes:
        out_in, refs = refs[:N_OUTPROJ_REFS], refs[N_OUTPROJ_REFS:]
    if opens:
        in_in, refs = refs[:N_INPROJ_REFS], refs[N_INPROJ_REFS + (2 if kv_aliased else 0):]
    if closes:
        x = _outproj_body(x, *out_in)
        refs.pop(0)[...] = x
    if opens:
        _inproj_body(x, *in_in, refs, kv_slot, position_minor)


def _layer_call(x2d, closing, opening, tm, rows_per_seq, depth):
    n = x2d.shape[0]
    if rows_per_seq is None:
        mod_spec = pl.BlockSpec((1, tm, D_MODEL), lambda i: (0, i, 0))
    else:
        tps = rows_per_seq // tm
        mod_spec = pl.BlockSpec((1, 1, D_MODEL), lambda i: (i // tps, 0, 0))
    row_spec = lambda w: pl.BlockSpec((tm, w), lambda i: (i, 0))
    full = lambda a: pl.BlockSpec(a.shape, lambda i: (0,) * a.ndim)
    position_minor = rows_per_seq is not None
    args, in_specs, out_specs, out_shape, aliases = [x2d], [row_spec(D_MODEL)], [], [], {}
    if closing is not None:
        args += list(closing)
        in_specs += [row_spec(SSM_WIDTH)] * 3 + [row_spec(ATT_WIDTH), mod_spec] + [full(a) for a in closing[5:]]
        out_specs.append(row_spec(D_MODEL))
        out_shape.append(jax.ShapeDtypeStruct(x2d.shape, F32))
    if opening is not None:
        kv_aliased = opening.kv_all is not None
        kv_layers, kv_first = (1, opening.layer) if kv_aliased else (depth, 0)
        bf = jax.ShapeDtypeStruct((n, SSM_WIDTH), BF16)
        if position_minor:
            f3 = jax.ShapeDtypeStruct((depth, n // rows_per_seq, N_HEADS, HEAD_DIM, rows_per_seq), F32)
            kv_spec = pl.BlockSpec((kv_layers, 1, N_HEADS, HEAD_DIM, tm),
                                   lambda i: (kv_first, i // tps, 0, 0, i % tps))
        else:
            f3 = jax.ShapeDtypeStruct((depth, n, N_HEADS, HEAD_DIM), F32)
            kv_spec = pl.BlockSpec((kv_layers, tm, N_HEADS, HEAD_DIM), lambda i: (kv_first, i, 0, 0))
        args += list(opening[:N_INPROJ_REFS])
        in_specs += [mod_spec, mod_spec] + [full(a) for a in opening[2:N_INPROJ_REFS]]
        if kv_aliased:
            assert opening.kv_all[0].shape == f3.shape and opening.kv_all[1].shape == f3.shape
            args += list(opening.kv_all)
            in_specs += [pl.BlockSpec(memory_space=pl.ANY)] * 2
            aliases = {len(args) - 2: len(out_shape) + 3, len(args) - 1: len(out_shape) + 4}
        out_specs += [row_spec(SSM_WIDTH)] * 3 + [kv_spec] * 2 + [row_spec(SSM_WIDTH)] * 3
        out_shape += [jax.ShapeDtypeStruct((n, SSM_WIDTH), F32), bf, bf, f3, f3, bf, bf, bf]
    outs = pl.pallas_call(
        functools.partial(_layer_kernel, closes=closing is not None, opens=opening is not None,
                          kv_aliased=opening is not None and opening.kv_all is not None,
                          kv_slot=None if opening is None else opening.layer,
                          position_minor=position_minor),
        grid=(n // tm,),
        in_specs=in_specs,
        out_specs=out_specs,
        out_shape=out_shape,
        input_output_aliases=aliases,
        compiler_params=_cparams(1),
        name="layer_" + "_".join(["close"] * (closing is not None) + ["open"] * (opening is not None)),
    )(*args)
    x_new = outs[0] if closing is not None else None
    opened = tuple(outs[len(outs) - 8:]) if opening is not None else None
    return x_new, opened


def _sb_sum_matrix():
    j = jnp.arange(2 * KEY_TILE)[:, None] % KEY_TILE
    s = jnp.arange(2 * KEY_TILE)[None, :]
    return -((s >= KEY_TILE) | (j > s)).astype(BF16)


def _head_sum_matrix():
    a = jnp.arange(MXU_DIM)
    return (a[:, None] // HEAD_DIM == a[None, :] // HEAD_DIM).astype(BF16)


def kernel(x_prompt, x_sample, c_prompt, c_sample, cache_k, cache_v, state_ssm_re, state_ssm_im, norm_g, w_mod, b_mod, w_in, ssm_a_re, ssm_a_im, ssm_log_dt, ssm_b_re, ssm_b_im, ssm_c_re, ssm_c_im, ssm_d, w_glu, b_glu, q_norm_g, k_norm_g, w_out):
    depth = w_in.shape[0]
    nb, seq_len, _ = x_prompt.shape
    ns, t_new, _ = x_sample.shape
    tm = 512

    mod = _modulation(jnp.concatenate([c_prompt, c_sample], axis=0), w_mod, b_mod)
    s5_ops = _s5_operators(ssm_a_re, ssm_a_im, ssm_log_dt, ssm_b_re, ssm_b_im, ssm_c_re, ssm_c_im)

    w_in_b = w_in.astype(BF16)
    w_out_b = w_out.astype(BF16)
    w_glu_b = w_glu.astype(BF16)
    tri = _sb_sum_matrix()
    hsum = _head_sum_matrix()

    xp = x_prompt.reshape(nb * seq_len, D_MODEL)
    xs = x_sample.reshape(ns * t_new, D_MODEL)
    zero_state = jnp.zeros((nb, 1, STATE_LANES), F32)
    kc = jnp.transpose(cache_k, (0, 1, 3, 4, 2))
    vc = jnp.transpose(cache_v, (0, 1, 3, 4, 2))
    outs = {name: [] for name in ("pr", "pi", "sr", "si")}
    pkv = skv = None

    def in_proj_args(l, shift, scale, kv_all):
        return _InProj(shift, scale, norm_g[l].reshape(1, D_MODEL), w_in_b[l], hsum,
                       jnp.tile(q_norm_g[l], N_HEADS).reshape(1, ATT_WIDTH),
                       jnp.tile(k_norm_g[l], N_HEADS).reshape(1, ATT_WIDTH), l, kv_all)

    def prompt_mod(l):
        mp = mod[l, :nb].reshape(nb, 1, 3 * D_MODEL)
        return tuple(mp[:, :, i * D_MODEL:(i + 1) * D_MODEL] for i in range(3))

    shift, scale, gate = prompt_mod(0)
    _, opened = _layer_call(xp, None, in_proj_args(0, shift, scale, pkv), tm, seq_len, depth)
    for l in range(depth):
        glu_w = (ssm_d[l].reshape(1, SSM_WIDTH), w_glu_b[l], b_glu[l].reshape(1, SSM_WIDTH))

        u, zs, q, k_all, v_all, kb, vb, za = opened
        pkv = (k_all, v_all)
        ysc, hfr, hfi = _s5_branch(u, zero_state, zero_state, *s5_ops, l, 1, seq_len)
        ya = _sb_prompt(q, kb, vb, za, tri, nb, seq_len)
        closing = _OutProj(ysc, u, zs, ya, gate, *glu_w, w_out_b[l])
        opening = None
        if l + 1 < depth:
            shift, scale, gate = prompt_mod(l + 1)
            opening = in_proj_args(l + 1, shift, scale, pkv)
        xp, opened = _layer_call(xp, closing, opening, tm, seq_len, depth)
        outs["pr"].append(hfr.reshape(nb, SSM_GROUPS, SSM_STATE))
        outs["pi"].append(hfi.reshape(nb, SSM_GROUPS, SSM_STATE))

        ms = jnp.repeat(mod[l, nb:], t_new, axis=0).reshape(1, ns * t_new, 3 * D_MODEL)
        s_shift, s_scale, s_gate = (ms[:, :, i * D_MODEL:(i + 1) * D_MODEL] for i in range(3))
        ts = ns * t_new
        _, (u, zs, q, k_all, v_all, kb, vb, za) = _layer_call(
            xs, None, in_proj_args(l, s_shift, s_scale, skv), ts, None, depth)
        skv = (k_all, v_all)
        h0r = state_ssm_re[l].reshape(ns, 1, STATE_LANES)
        h0i = state_ssm_im[l].reshape(ns, 1, STATE_LANES)
        ysc, hfr, hfi = _s5_branch(u, h0r, h0i, *s5_ops, l, ns, t_new)
        pad = lambda a: jnp.pad(a.reshape(ns, t_new, ATT_WIDTH),
                                ((0, 0), (0, KEY_TILE - t_new), (0, 0))).reshape(ns * KEY_TILE, ATT_WIDTH)
        ya = _sb_sample(q, pad(kb), pad(vb), kc, vc, l, za, tri, ns, t_new)
        xs, _ = _layer_call(xs, _OutProj(ysc, u, zs, ya, s_gate, *glu_w, w_out_b[l]), None, ts, None, depth)
        outs["sr"].append(hfr.reshape(ns, SSM_GROUPS, SSM_STATE))
        outs["si"].append(hfi.reshape(ns, SSM_GROUPS, SSM_STATE))

    st = lambda name: jnp.stack(outs[name])
    heads_p = lambda a: jnp.transpose(a, (0, 1, 4, 2, 3))
    heads_s = lambda a: a.reshape(depth, ns, t_new, N_HEADS, HEAD_DIM)
    return (xp.reshape(nb, seq_len, D_MODEL), xs.reshape(ns, t_new, D_MODEL),
            heads_p(pkv[0]), heads_p(pkv[1]), st("pr"), st("pi"),
            heads_s(skv[0]), heads_s(skv[1]), st("sr"), st("si"))
```
